```python
import math
import jax, jax.numpy as jnp
from jax import lax
import numpy as np

D_MODEL = 1024
BATCH = 4
SEQ = 4096
DEPTH = 4
DEC_BATCH = 128
DEC_SEQ = 4
PAST_LEN = 2048
PAGE_SIZE = 128

N_A_LAYERS = DEPTH // 2
N_B_LAYERS = DEPTH - N_A_LAYERS
CHUNK = 128
A_INNER = D_MODEL
A_GROUPS = 8
A_GROUP_DIM = A_INNER // A_GROUPS
N_HEADS = 8
HEAD_DIM = D_MODEL // (2 * N_HEADS)
QK_DIM = 2 * HEAD_DIM
V_DIM = 2 * HEAD_DIM
Q_BLOCK = 128
N_EXPERTS = 16
N_EXPERT_GROUPS = 4
EXPERTS_PER_GROUP = N_EXPERTS // N_EXPERT_GROUPS
TOP_K = 2
D_EXPERT = D_MODEL // 2
ALPHA = (2.0 * DEPTH) ** 0.25
BETA = (8.0 * DEPTH) ** -0.25
LN_EPS = 1e-5
NEG = -1e30

kernel_name = "yoco_gmlp_diffattn_grouped_moe_step"


def layer_norm(x, g, b):
    xf = x.astype(jnp.float32)
    mu = jnp.mean(xf, -1, keepdims=True)
    var = jnp.mean(jnp.square(xf - mu), -1, keepdims=True)
    y = (xf - mu) * lax.rsqrt(var + LN_EPS) * g.astype(jnp.float32) + b.astype(jnp.float32)
    return y.astype(x.dtype)


def chunk_gmlp(x, w_in, b_in, lnv_g, lnv_b, w_s, b_s, w_out):
    n, length, _ = x.shape
    z = jax.nn.gelu(x @ w_in + b_in)
    u, v = jnp.split(z, 2, axis=-1)
    v = layer_norm(v, lnv_g, lnv_b)
    c = min(CHUNK, length)
    n_chunks = -(-length // c)
    pad = n_chunks * c - length
    vc = jnp.pad(v, ((0, 0), (0, pad), (0, 0))).reshape(n, n_chunks, c, A_GROUPS, A_GROUP_DIM)
    causal = jnp.tril(jnp.ones((c, c), dtype=bool))
    w_mix = jnp.where(causal, w_s[:, :c, :c], jnp.zeros((), w_s.dtype))
    s = jnp.einsum('gts,ncsgd->nctgd', w_mix, vc) + b_s[:, :c].T[None, None, :, :, None]
    s = s.reshape(n, n_chunks * c, A_INNER)[:, :length]
    return (u * s) @ w_out, v


def grouped_moe(x, router_w, router_bias, w_up, w_down):
    probs = jax.nn.softmax((x @ router_w).astype(jnp.float32), axis=-1)
    sel = probs + router_bias.astype(jnp.float32)
    sel_g = sel.reshape(*sel.shape[:-1], N_EXPERT_GROUPS, EXPERTS_PER_GROUP)
    group_score = jnp.sum(lax.top_k(sel_g, TOP_K)[0], axis=-1)
    g_best = jnp.argmax(group_score, axis=-1)
    in_group = (jnp.arange(N_EXPERTS) // EXPERTS_PER_GROUP) == g_best[..., None]
    _, expert_idx = lax.top_k(jnp.where(in_group, sel, NEG), TOP_K)
    gate = jnp.take_along_axis(probs, expert_idx, axis=-1)
    gate = gate / jnp.sum(gate, axis=-1, keepdims=True)
    combine = jnp.sum(jax.nn.one_hot(expert_idx, N_EXPERTS, dtype=jnp.float32) * gate[..., None], axis=-2)
    combine = combine.astype(x.dtype)
    out = jnp.zeros_like(x)
    for e in range(N_EXPERTS):
        h_gate, h_up = jnp.split(x @ w_up[e], 2, axis=-1)
        out = out + combine[..., e:e + 1] * ((jax.nn.silu(h_gate) * h_up) @ w_down[e])
    return out


def proj_heads(x, w):
    return (x @ w).reshape(x.shape[0], x.shape[1], N_HEADS, -1)


def diff_lambda(lq1, lk1, lq2, lk2, lam_init):
    f = jnp.float32
    return (jnp.exp(jnp.sum(lq1.astype(f) * lk1.astype(f)))
            - jnp.exp(jnp.sum(lq2.astype(f) * lk2.astype(f))) + lam_init)


def masked_softmax(scores, mask):
    return jax.nn.softmax(jnp.where(mask, scores, NEG), axis=-1)


def diff_attn_prompt(q, k, v, lam):
    n, s = q.shape[:2]
    nb = s // Q_BLOCK
    scale = HEAD_DIM ** -0.5
    k1, k2 = jnp.split(k, 2, axis=-1)
    key_pos = jnp.arange(s)
    qb = q.reshape(n, nb, Q_BLOCK, N_HEADS, QK_DIM).transpose(1, 0, 2, 3, 4)

    def block(args):
        q_blk, start = args
        q1, q2 = jnp.split(q_blk, 2, axis=-1)
        mask = (start + jnp.arange(Q_BLOCK))[:, None] >= key_pos[None, :]
        p1 = masked_softmax(jnp.einsum('bqhd,bkhd->bhqk', q1, k1).astype(jnp.float32) * scale, mask)
        p2 = masked_softmax(jnp.einsum('bqhd,bkhd->bhqk', q2, k2).astype(jnp.float32) * scale, mask)
        p = (p1 - lam * p2).astype(v.dtype)
        return jnp.einsum('bhqk,bkhe->bqhe', p, v)

    o = lax.map(block, (qb, jnp.arange(nb, dtype=jnp.int32) * Q_BLOCK))
    return o.transpose(1, 0, 2, 3, 4).reshape(n, s, N_HEADS, V_DIM)


def diff_attn_sample(q, k_past, v_past, k_new, v_new, lam):
    t = q.shape[1]
    p_len = k_past.shape[1]
    scale = HEAD_DIM ** -0.5
    causal = jnp.tril(jnp.ones((t, t), dtype=bool))
    q1, q2 = jnp.split(q, 2, axis=-1)
    kp1, kp2 = jnp.split(k_past, 2, axis=-1)
    kn1, kn2 = jnp.split(k_new, 2, axis=-1)

    def probs(qi, kpi, kni):
        s_past = jnp.einsum('bqhd,bkhd->bhqk', qi, kpi).astype(jnp.float32) * scale
        s_new = jnp.where(causal, jnp.einsum('bqhd,bkhd->bhqk', qi, kni).astype(jnp.float32) * scale, NEG)
        return jax.nn.softmax(jnp.concatenate([s_past, s_new], axis=-1), axis=-1)

    p = (probs(q1, kp1, kn1) - lam * probs(q2, kp2, kn2)).astype(v_new.dtype)
    return (jnp.einsum('bhqk,bkhe->bqhe', p[..., :p_len], v_past)
            + jnp.einsum('bhqk,bkhe->bqhe', p[..., p_len:], v_new))


def diff_head_merge(o, subln_g, lam_init, w_o):
    of = o.astype(jnp.float32)
    of = of * lax.rsqrt(jnp.mean(of * of, -1, keepdims=True) + LN_EPS) * subln_g.astype(jnp.float32) * (1.0 - lam_init)
    n, length = o.shape[:2]
    return of.astype(o.dtype).reshape(n, length, N_HEADS * V_DIM) @ w_o


def setup_inputs(seed: int = 0) -> dict:
    key = jax.random.key(seed)
    ks = jax.random.split(key, 32)
    f32 = jnp.float32
    n_pages = PAST_LEN // PAGE_SIZE
    n_used = DEC_BATCH * n_pages
    n_phys = n_used + max(1, n_used // 4)

    def nrm(k, shape, scale):
        return jax.random.normal(k, shape, f32) * scale

    page_table = jax.random.permutation(ks[0], n_phys)[:n_used].reshape(DEC_BATCH, n_pages).astype(jnp.int32)
    return {
        "x_prompt": nrm(ks[1], (BATCH, SEQ, D_MODEL), 1.0),
        "x_sample": nrm(ks[2], (DEC_BATCH, DEC_SEQ, D_MODEL), 1.0),
        "cache_k": nrm(ks[3], (n_phys, PAGE_SIZE, N_HEADS, QK_DIM), 1.0),
        "cache_v": nrm(ks[4], (n_phys, PAGE_SIZE, N_HEADS, V_DIM), BETA),
        "page_table": page_table,
        "ln1_g": 1.0 + nrm(ks[5], (DEPTH, D_MODEL), 0.05),
        "ln1_b": nrm(ks[6], (DEPTH, D_MODEL), 0.02),
        "ln2_g": 1.0 + nrm(ks[7], (DEPTH, D_MODEL), 0.05),
        "ln2_b": nrm(ks[8], (DEPTH, D_MODEL), 0.02),
        "a_w_in": nrm(ks[9], (N_A_LAYERS, D_MODEL, 2 * A_INNER), D_MODEL ** -0.5),
        "a_b_in": nrm(ks[10], (N_A_LAYERS, 2 * A_INNER), 0.02),
        "a_lnv_g": 1.0 + nrm(ks[11], (N_A_LAYERS, A_INNER), 0.05),
        "a_lnv_b": nrm(ks[12], (N_A_LAYERS, A_INNER), 0.02),
        "a_w_s": nrm(ks[13], (N_A_LAYERS, A_GROUPS, CHUNK, CHUNK), CHUNK ** -0.5),
        "a_b_s": 1.0 + nrm(ks[14], (N_A_LAYERS, A_GROUPS, CHUNK), 0.1),
        "a_w_out": nrm(ks[15], (N_A_LAYERS, A_INNER, D_MODEL), BETA * A_INNER ** -0.5),
        "w_k": nrm(ks[16], (D_MODEL, N_HEADS * QK_DIM), D_MODEL ** -0.5),
        "w_v": nrm(ks[17], (D_MODEL, N_HEADS * V_DIM), BETA * D_MODEL ** -0.5),
        "b_w_q": nrm(ks[18], (N_B_LAYERS, D_MODEL, N_HEADS * QK_DIM), D_MODEL ** -0.5),
        "b_lam_q1": nrm(ks[19], (N_B_LAYERS, HEAD_DIM), 0.1),
        "b_lam_k1": nrm(ks[20], (N_B_LAYERS, HEAD_DIM), 0.1),
        "b_lam_q2": nrm(ks[21], (N_B_LAYERS, HEAD_DIM), 0.1),
        "b_lam_k2": nrm(ks[22], (N_B_LAYERS, HEAD_DIM), 0.1),
        "b_subln_g": 1.0 + nrm(ks[23], (N_B_LAYERS, V_DIM), 0.05),
        "b_w_o": nrm(ks[24], (N_B_LAYERS, N_HEADS * V_DIM, D_MODEL), BETA * (N_HEADS * V_DIM) ** -0.5),
        "router_w": nrm(ks[25], (D_MODEL, N_EXPERTS), D_MODEL ** -0.5),
        "router_bias": nrm(ks[26], (N_EXPERTS,), 0.01),
        "moe_w_up": nrm(ks[27], (DEPTH, N_EXPERTS, D_MODEL, 2 * D_EXPERT), D_MODEL ** -0.5),
        "moe_w_down": nrm(ks[28], (DEPTH, N_EXPERTS, D_EXPERT, D_MODEL), BETA * D_EXPERT ** -0.5),
    }


def reference(x_prompt, x_sample, cache_k, cache_v, page_table, ln1_g, ln1_b, ln2_g, ln2_b,
              a_w_in, a_b_in, a_lnv_g, a_lnv_b, a_w_s, a_b_s, a_w_out, w_k, w_v,
              b_w_q, b_lam_q1, b_lam_k1, b_lam_q2, b_lam_k2, b_subln_g, b_w_o,
              router_w, router_bias, moe_w_up, moe_w_down):
    xp, xs = x_prompt, x_sample
    chunk_v_rows = []
    for l in range(DEPTH):
        if l < N_A_LAYERS:
            a_args = (a_w_in[l], a_b_in[l], a_lnv_g[l], a_lnv_b[l], a_w_s[l], a_b_s[l], a_w_out[l])
            mp, _ = chunk_gmlp(xp, *a_args)
            ms, vs = chunk_gmlp(xs, *a_args)
            chunk_v_rows.append(vs)
        else:
            j = l - N_A_LAYERS
            if j == 0:
                k_prompt = proj_heads(xp, w_k)
                v_prompt = proj_heads(xp, w_v)
                k_sample = proj_heads(xs, w_k)
                v_sample = proj_heads(xs, w_v)
                n_seq, n_pg = page_table.shape
                k_past = cache_k[page_table].reshape(n_seq, n_pg * PAGE_SIZE, N_HEADS, QK_DIM)
                v_past = cache_v[page_table].reshape(n_seq, n_pg * PAGE_SIZE, N_HEADS, V_DIM)
            lam_init = 0.8 - 0.6 * math.exp(-0.3 * l)
            lam = diff_lambda(b_lam_q1[j], b_lam_k1[j], b_lam_q2[j], b_lam_k2[j], lam_init)
            qp = proj_heads(xp, b_w_q[j])
            qs = proj_heads(xs, b_w_q[j])
            mp = diff_head_merge(diff_attn_prompt(qp, k_prompt, v_prompt, lam), b_subln_g[j], lam_init, b_w_o[j])
            ms = diff_head_merge(diff_attn_sample(qs, k_past, v_past, k_sample, v_sample, lam),
                                 b_subln_g[j], lam_init, b_w_o[j])
        xp = layer_norm(ALPHA * xp + mp, ln1_g[l], ln1_b[l])
        xs = layer_norm(ALPHA * xs + ms, ln1_g[l], ln1_b[l])
        xp = layer_norm(ALPHA * xp + grouped_moe(xp, router_w, router_bias, moe_w_up[l], moe_w_down[l]),
                        ln2_g[l], ln2_b[l])
        xs = layer_norm(ALPHA * xs + grouped_moe(xs, router_w, router_bias, moe_w_up[l], moe_w_down[l]),
                        ln2_g[l], ln2_b[l])
    state_chunk_v_sample = jnp.stack(chunk_v_rows)
    return (xp, xs, k_prompt, v_prompt, k_sample, v_sample, state_chunk_v_sample)
```

```python
import functools
import math

import jax
import jax.numpy as jnp
from jax import lax
from jax.experimental import pallas as pl
from jax.experimental.pallas import tpu as pltpu

F32 = jnp.float32
BF16 = jnp.bfloat16

DEPTH = 4
N_A_LAYERS = DEPTH // 2
CHUNK = 128
A_GROUPS = 8
N_HEADS = 8
HEAD_DIM = 64
N_EXPERTS = 16
N_EXPERT_GROUPS = 4
EXPERTS_PER_GROUP = N_EXPERTS // N_EXPERT_GROUPS
ALPHA = (2.0 * DEPTH) ** 0.25
LN_EPS = 1e-5
NEG = -1e30
PAGE_SIZE = 128

LANES = 128
TOKEN_TILE = 512
ATTN_Q_TILE = 512
ATTN_K_TILE = 512
SAMPLE_SEQS_PER_STEP = 2
SAMPLE_PAGES_PER_STEP = 4
VMEM_LIMIT = 56 * 1024 * 1024


def _params(semantics):
    return pltpu.CompilerParams(dimension_semantics=semantics, vmem_limit_bytes=VMEM_LIMIT)


def _layer_norm(x, g, b):
    mu = jnp.mean(x, axis=-1, keepdims=True)
    xc = x - mu
    var = jnp.mean(xc * xc, axis=-1, keepdims=True)
    return xc * lax.rsqrt(var + LN_EPS) * g + b


def _dot(a, b):
    return jnp.dot(a, b, preferred_element_type=F32)


def _dot_nt(a, b):
    return lax.dot_general(a, b, (((1,), (1,)), ((), ())), preferred_element_type=F32)


def _gmlp_kernel(x_ref, win_ref, bin_ref, lvg_ref, lvb_ref, ws_ref, sb_ref,
                 wout_ref, g1_ref, b1_ref, xo_ref, v_ref, *, n_prompt_tiles, dec_seq):
    i = pl.program_id(0)
    is_sample = i >= n_prompt_tiles
    a_inner = wout_ref.shape[0]
    gd = a_inner // A_GROUPS
    tm = x_ref.shape[0]

    x = x_ref[...]
    z = _dot(x.astype(BF16), win_ref[...]) + bin_ref[...]
    z = jax.nn.gelu(z)
    u = z[:, :a_inner]
    v = _layer_norm(z[:, a_inner:], lvg_ref[...], lvb_ref[...])

    @pl.when(is_sample)
    def _():
        v_ref[...] = v

    row = lax.broadcasted_iota(jnp.int32, (CHUNK, CHUNK), 0)
    col = lax.broadcasted_iota(jnp.int32, (CHUNK, CHUNK), 1)
    mask = jnp.logical_and(col <= row, jnp.logical_or(jnp.logical_not(is_sample),
                                                      (row // dec_seq) == (col // dec_seq)))
    vb = v.astype(BF16)
    kind = is_sample.astype(jnp.int32)
    bias = sb_ref[kind]
    s_rows = []
    mixes = [jnp.where(mask, ws_ref[kind, g], 0.0).astype(BF16) for g in range(A_GROUPS)]
    for c in range(tm // CHUNK):
        cols = []
        for g in range(A_GROUPS):
            blk = vb[c * CHUNK:(c + 1) * CHUNK, g * gd:(g + 1) * gd]
            cols.append(_dot(mixes[g], blk))
        s_rows.append(jnp.concatenate(cols, axis=1) + bias)
    s = jnp.concatenate(s_rows, axis=0)
    m = _dot((u * s).astype(BF16), wout_ref[...])
    xo_ref[...] = _layer_norm(ALPHA * x + m, g1_ref[...], b1_ref[...])


def _gmlp_layer(x, w_in, b_in, lnv_g, lnv_b, w_s, b_s, w_out, ln_g, ln_b, *, n_prompt, dec_seq):
    n, d = x.shape
    a_inner = w_out.shape[0]
    tm = TOKEN_TILE
    n_tiles = n // tm
    n_prompt_tiles = n_prompt // tm
    n_sample = n - n_prompt
    reps = CHUNK // dec_seq
    ws = jnp.stack([w_s, jnp.tile(w_s[:, :dec_seq, :dec_seq], (1, reps, reps))])
    gd = a_inner // A_GROUPS
    sb_p = jnp.repeat(b_s.T, gd, axis=1)
    sb_s = jnp.repeat(jnp.tile(b_s[:, :dec_seq].T, (reps, 1)), gd, axis=1)
    sb = jnp.stack([sb_p, sb_s])
    const2 = lambda i: (0, 0)
    const3 = lambda i: (0, 0, 0)
    const4 = lambda i: (0, 0, 0, 0)
    kern = functools.partial(_gmlp_kernel, n_prompt_tiles=n_prompt_tiles, dec_seq=dec_seq)
    return pl.pallas_call(
        kern,
        grid=(n_tiles,),
        in_specs=[
            pl.BlockSpec((tm, d), lambda i: (i, 0)),
            pl.BlockSpec((d, 2 * a_inner), const2),
            pl.BlockSpec((1, 2 * a_inner), const2),
            pl.BlockSpec((1, a_inner), const2),
            pl.BlockSpec((1, a_inner), const2),
            pl.BlockSpec((2, A_GROUPS, CHUNK, CHUNK), const4),
            pl.BlockSpec((2, CHUNK, a_inner), const3),
            pl.BlockSpec((a_inner, d), const2),
            pl.BlockSpec((1, d), const2),
            pl.BlockSpec((1, d), const2),
        ],
        out_specs=[
            pl.BlockSpec((tm, d), lambda i: (i, 0)),
            pl.BlockSpec((tm, a_inner), lambda i: (jnp.maximum(i - n_prompt_tiles, 0), 0)),
        ],
        out_shape=[jax.ShapeDtypeStruct((n, d), F32), jax.ShapeDtypeStruct((n_sample, a_inner), F32)],
        compiler_params=_params(("arbitrary",)),
        name="gmlp_layer",
    )(x, w_in.astype(BF16), b_in[None], lnv_g[None], lnv_b[None], ws, sb,
      w_out.astype(BF16), ln_g[None], ln_b[None])


def _route(x_bf, rwt_ref, rbias_ref):
    logits = _dot_nt(rwt_ref[...], x_bf)
    mx = jnp.max(logits, axis=0, keepdims=True)
    ex = jnp.exp(logits - mx)
    probs = ex / jnp.sum(ex, axis=0, keepdims=True)
    sel = probs + rbias_ref[...]
    p = [probs[e:e + 1, :] for e in range(N_EXPERTS)]
    s = [sel[e:e + 1, :] for e in range(N_EXPERTS)]
    gscore = []
    for g in range(N_EXPERT_GROUPS):
        a, b, c, d = s[4 * g:4 * g + 4]
        hi1, lo1 = jnp.maximum(a, b), jnp.minimum(a, b)
        hi2, lo2 = jnp.maximum(c, d), jnp.minimum(c, d)
        top1 = jnp.maximum(hi1, hi2)
        top2 = jnp.maximum(jnp.minimum(hi1, hi2), jnp.maximum(lo1, lo2))
        gscore.append(top1 + top2)
    best = gscore[0]
    gbest = jnp.zeros_like(best, dtype=jnp.int32)
    for g in range(1, N_EXPERT_GROUPS):
        better = gscore[g] > best
        best = jnp.where(better, gscore[g], best)
        gbest = jnp.where(better, g, gbest)
    gates = []
    for e in range(N_EXPERTS):
        g = e // EXPERTS_PER_GROUP
        rank = jnp.zeros_like(gbest)
        for f in range(EXPERTS_PER_GROUP * g, EXPERTS_PER_GROUP * (g + 1)):
            if f == e:
                continue
            beats = (s[f] >= s[e]) if f < e else (s[f] > s[e])
            rank = rank + beats.astype(jnp.int32)
        chosen = (gbest == g) & (rank < 2)
        gates.append(jnp.where(chosen, p[e], 0.0))
    denom = gates[0]
    for e in range(1, N_EXPERTS):
        denom = denom + gates[e]
    return [gt / denom for gt in gates]


def _moe_kernel(x_ref, rwt_ref, rbias_ref, wup_ref, wdn_ref, g_ref, b_ref, o_ref, acc_ref, comb_ref):
    e = pl.program_id(1)
    d_exp = wdn_ref.shape[1]
    x = x_ref[...]
    xb = x.astype(BF16)

    @pl.when(e == 0)
    def _():
        comb = _route(xb, rwt_ref, rbias_ref)
        tm = x.shape[0]
        rows = jnp.concatenate(comb + [jnp.zeros((LANES - N_EXPERTS, tm), F32)], axis=0)
        comb_ref[...] = rows.T
        acc_ref[...] = jnp.zeros_like(acc_ref)

    lane = lax.broadcasted_iota(jnp.int32, comb_ref.shape, 1)
    c_e = jnp.sum(jnp.where(lane == e, comb_ref[...], 0.0), axis=1, keepdims=True)
    h = _dot(xb, wup_ref[0])
    act = jax.nn.silu(h[:, :d_exp]) * h[:, d_exp:]
    acc_ref[...] += _dot((act * c_e).astype(BF16), wdn_ref[0])

    @pl.when(e == N_EXPERTS - 1)
    def _():
        o_ref[...] = _layer_norm(ALPHA * x + acc_ref[...], g_ref[...], b_ref[...])


def _moe_layer(x, rwt, rbias, w_up, w_down, ln_g, ln_b):
    n, d = x.shape
    tm = TOKEN_TILE
    d_exp = w_down.shape[1]
    const2 = lambda i, e: (0, 0)
    return pl.pallas_call(
        _moe_kernel,
        grid=(n // tm, N_EXPERTS),
        in_specs=[
            pl.BlockSpec((tm, d), lambda i, e: (i, 0)),
            pl.BlockSpec((N_EXPERTS, d), const2),
            pl.BlockSpec((N_EXPERTS, 1), const2),
            pl.BlockSpec((1, d, 2 * d_exp), lambda i, e: (e, 0, 0)),
            pl.BlockSpec((1, d_exp, d), lambda i, e: (e, 0, 0)),
            pl.BlockSpec((1, d), const2),
            pl.BlockSpec((1, d), const2),
        ],
        out_specs=pl.BlockSpec((tm, d), lambda i, e: (i, 0)),
        out_shape=jax.ShapeDtypeStruct((n, d), F32),
        scratch_shapes=[pltpu.VMEM((tm, d), F32), pltpu.VMEM((tm, LANES), F32)],
        compiler_params=_params(("arbitrary", "arbitrary")),
        name="moe_layer",
    )(x, rwt, rbias, w_up.astype(BF16), w_down.astype(BF16), ln_g[None], ln_b[None])


def _kvq_kernel(x_ref, wk_ref, wv_ref, wq_ref, k_ref, v_ref, kb_ref, vb_ref, qb_ref):
    xb = x_ref[...].astype(BF16)
    k = _dot(xb, wk_ref[...])
    v = _dot(xb, wv_ref[...])
    k_ref[...] = k
    v_ref[...] = v
    kb_ref[...] = k.astype(BF16)
    vb_ref[...] = v.astype(BF16)
    qb_ref[...] = (_dot(xb, wq_ref[...]) * (HEAD_DIM ** -0.5)).astype(BF16)


def _kvq_proj(x, w_k, w_v, w_q):
    n, d = x.shape
    tm = TOKEN_TILE
    row = pl.BlockSpec((tm, d), lambda i: (i, 0))
    wspec = pl.BlockSpec((d, d), lambda i: (0, 0))
    return pl.pallas_call(
        _kvq_kernel,
        grid=(n // tm,),
        in_specs=[row, wspec, wspec, wspec],
        out_specs=[row, row, row, row, row],
        out_shape=[jax.ShapeDtypeStruct((n, d), F32), jax.ShapeDtypeStruct((n, d), F32),
                   jax.ShapeDtypeStruct((n, d), BF16), jax.ShapeDtypeStruct((n, d), BF16),
                   jax.ShapeDtypeStruct((n, d), BF16)],
        compiler_params=_params(("parallel",)),
        name="kvq_proj",
    )(x, w_k.astype(BF16), w_v.astype(BF16), w_q.astype(BF16))


def _q_kernel(x_ref, wq_ref, qb_ref):
    qb_ref[...] = (_dot(x_ref[...].astype(BF16), wq_ref[...]) * (HEAD_DIM ** -0.5)).astype(BF16)


def _q_proj(x, w_q, n_rows):
    d = x.shape[1]
    tm = TOKEN_TILE
    row = pl.BlockSpec((tm, d), lambda i: (i, 0))
    return pl.pallas_call(
        _q_kernel,
        grid=(n_rows // tm,),
        in_specs=[row, pl.BlockSpec((d, d), lambda i: (0, 0))],
        out_specs=row,
        out_shape=jax.ShapeDtypeStruct((n_rows, d), BF16),
        compiler_params=_params(("parallel",)),
        name="q_proj",
    )(x, w_q.astype(BF16))


def _diff_lambda(lam_ref, lam_init):
    lv = lam_ref[...]
    e1 = jnp.exp(jnp.sum(lv[0:1, :] * lv[1:2, :], axis=1, keepdims=True))
    e2 = jnp.exp(jnp.sum(lv[2:3, :] * lv[3:4, :], axis=1, keepdims=True))
    return e1 - e2 + lam_init


def _softmax_step(s, vc, m_ref, l_ref, acc_ref):
    m_old = m_ref[...]
    m_new = jnp.maximum(m_old, jnp.max(s, axis=1, keepdims=True))
    alpha = jnp.exp(m_old - m_new)
    p = jnp.exp(s - m_new)
    l_ref[...] = alpha * l_ref[...] + jnp.sum(p, axis=1, keepdims=True)
    acc_ref[...] = alpha * acc_ref[...] + _dot(p.astype(BF16), vc)
    m_ref[...] = m_new


def _head_norm(o, g, lam_init):
    return o * lax.rsqrt(jnp.mean(o * o, axis=-1, keepdims=True) + LN_EPS) * g * (1.0 - lam_init)


def _pattn_kernel(lam_ref, g_ref, q_ref, k_ref, v_ref, o_ref,
                  m1_ref, l1_ref, a1_ref, m2_ref, l2_ref, a2_ref, *, lam_init):
    qi = pl.program_id(2)
    tq = q_ref.shape[0]
    tk = ATTN_K_TILE
    hd = HEAD_DIM
    q = q_ref[...]
    q1, q2 = q[:, :hd], q[:, hd:]
    for m_ref, l_ref, a_ref in ((m1_ref, l1_ref, a1_ref), (m2_ref, l2_ref, a2_ref)):
        m_ref[...] = jnp.full_like(m_ref, NEG)
        l_ref[...] = jnp.zeros_like(l_ref)
        a_ref[...] = jnp.zeros_like(a_ref)

    def chunk(start, mask):
        kc = k_ref[pl.ds(start, tk), :]
        vc = v_ref[pl.ds(start, tk), :]
        s1 = _dot_nt(q1, kc[:, :hd])
        s2 = _dot_nt(q2, kc[:, hd:])
        if mask is not None:
            s1 = jnp.where(mask, s1, NEG)
            s2 = jnp.where(mask, s2, NEG)
        _softmax_step(s1, vc, m1_ref, l1_ref, a1_ref)
        _softmax_step(s2, vc, m2_ref, l2_ref, a2_ref)

    def body(j, carry):
        chunk(pl.multiple_of(j * tk, tk), None)
        return carry

    lax.fori_loop(0, qi * (tq // tk), body, 0)
    row = lax.broadcasted_iota(jnp.int32, (tq, tk), 0)
    col = lax.broadcasted_iota(jnp.int32, (tq, tk), 1)
    for jj in range(tq // tk):
        chunk(pl.multiple_of(qi * tq + jj * tk, tk), row >= col + jj * tk)

    lam = _diff_lambda(lam_ref, lam_init)
    o = a1_ref[...] / l1_ref[...] - lam * (a2_ref[...] / l2_ref[...])
    o_ref[...] = _head_norm(o, g_ref[...], lam_init).astype(o_ref.dtype)


def _prompt_attention(qb, kb, vb, lam_rows, subln_g, lam_init, *, batch, seq):
    d = qb.shape[1]
    vd = d // N_HEADS
    tq = ATTN_Q_TILE
    nq = seq // tq
    kern = functools.partial(_pattn_kernel, lam_init=lam_init)
    return pl.pallas_call(
        kern,
        grid=(batch, N_HEADS, nq),
        in_specs=[
            pl.BlockSpec((4, HEAD_DIM), lambda b, h, i: (0, 0)),
            pl.BlockSpec((1, vd), lambda b, h, i: (0, 0)),
            pl.BlockSpec((tq, vd), lambda b, h, i: (b * nq + i, h)),
            pl.BlockSpec((seq, vd), lambda b, h, i: (b, h)),
            pl.BlockSpec((seq, vd), lambda b, h, i: (b, h)),
        ],
        out_specs=pl.BlockSpec((tq, vd), lambda b, h, i: (b * nq + i, h)),
        out_shape=jax.ShapeDtypeStruct((batch * seq, d), BF16),
        scratch_shapes=[pltpu.VMEM((tq, 1), F32), pltpu.VMEM((tq, 1), F32), pltpu.VMEM((tq, vd), F32),
                        pltpu.VMEM((tq, 1), F32), pltpu.VMEM((tq, 1), F32), pltpu.VMEM((tq, vd), F32)],
        compiler_params=_params(("parallel", "parallel", "arbitrary")),
        name="prompt_attention",
    )(lam_rows, subln_g[None], qb, kb, vb)


def _sattn_kernel(pt_ref, lam_ref, g_ref, x_ref, wq_ref, kn_ref, vn_ref, *rest, lam_init, dec_seq):
    npg = SAMPLE_SEQS_PER_STEP * SAMPLE_PAGES_PER_STEP
    k_refs = rest[:npg]
    v_refs = rest[npg:2 * npg]
    o_ref = rest[2 * npg]
    qbd_ref, m_ref, l_ref, acc_ref = rest[2 * npg + 1:]
    c = pl.program_id(1)
    n_rows = SAMPLE_SEQS_PER_STEP * dec_seq
    n_cols = qbd_ref.shape[0]
    d = x_ref.shape[1]
    vd = d // N_HEADS

    @pl.when(c == 0)
    def _():
        q = _dot(x_ref[...].astype(BF16), wq_ref[...]) * (HEAD_DIM ** -0.5)
        rep = jnp.concatenate([q] * (n_cols // n_rows), axis=0)
        rr = lax.broadcasted_iota(jnp.int32, rep.shape, 0)
        cc = lax.broadcasted_iota(jnp.int32, rep.shape, 1)
        qbd_ref[...] = jnp.where((cc // HEAD_DIM) == (rr // n_rows), rep, 0.0).astype(BF16)
        m_ref[...] = jnp.full_like(m_ref, NEG)
        l_ref[...] = jnp.zeros_like(l_ref)
        acc_ref[...] = jnp.zeros_like(acc_ref)

    qbd = qbd_ref[...]
    keys_per_seq = SAMPLE_PAGES_PER_STEP * PAGE_SIZE
    kall = jnp.concatenate([r[0] for r in k_refs], axis=0).astype(BF16)
    vall = jnp.concatenate([r[0] for r in v_refs], axis=0).astype(BF16)
    s = _dot_nt(qbd, kall)
    rr = lax.broadcasted_iota(jnp.int32, s.shape, 0)
    kk = lax.broadcasted_iota(jnp.int32, s.shape, 1)
    row_seq = (rr % n_rows) // dec_seq
    s = jnp.where(row_seq == kk // keys_per_seq, s, NEG)
    _softmax_step(s, vall, m_ref, l_ref, acc_ref)

    @pl.when(c == pl.num_programs(1) - 1)
    def _():
        pad = jnp.zeros((LANES - n_rows, d), F32)
        kn = jnp.concatenate([kn_ref[...], pad], axis=0).astype(BF16)
        vn = jnp.concatenate([vn_ref[...], pad], axis=0).astype(BF16)
        sn = _dot_nt(qbd, kn)
        r2 = lax.broadcasted_iota(jnp.int32, sn.shape, 0) % n_rows
        k2 = lax.broadcasted_iota(jnp.int32, sn.shape, 1)
        ok = (k2 < n_rows) & (k2 // dec_seq == r2 // dec_seq) & (k2 % dec_seq <= r2 % dec_seq)
        _softmax_step(jnp.where(ok, sn, NEG), vn, m_ref, l_ref, acc_ref)
        lam = _diff_lambda(lam_ref, lam_init)
        o = acc_ref[...] / l_ref[...]
        outs = []
        for h in range(N_HEADS):
            r0 = 2 * h * n_rows
            o1 = o[r0:r0 + n_rows, h * vd:(h + 1) * vd]
            o2 = o[r0 + n_rows:r0 + 2 * n_rows, h * vd:(h + 1) * vd]
            outs.append(_head_norm(o1 - lam * o2, g_ref[...], lam_init))
        o_ref[...] = jnp.concatenate(outs, axis=1)


def _sample_attention(x, w_q, k_f32, v_f32, cache_k, cache_v, page_table, lam_rows, subln_g, lam_init,
                      *, n_prompt, dec_seq):
    n, d = x.shape
    n_seq, n_pg = page_table.shape
    n_rows = SAMPLE_SEQS_PER_STEP * dec_seq
    n_cols = 2 * N_HEADS * n_rows
    pps = SAMPLE_PAGES_PER_STEP
    ck = cache_k.reshape(cache_k.shape[0], PAGE_SIZE, d)
    cv = cache_v.reshape(cache_v.shape[0], PAGE_SIZE, d)
    pt = page_table.reshape(-1)
    base = n_prompt // n_rows
    new_rows = pl.BlockSpec((n_rows, d), lambda b, c, pt: (base + b, 0))

    def page_spec(s, j):
        return pl.BlockSpec((1, PAGE_SIZE, d),
                            lambda b, c, pt: (pt[(b * SAMPLE_SEQS_PER_STEP + s) * n_pg + c * pps + j], 0, 0))

    page_specs = [page_spec(s, j) for s in range(SAMPLE_SEQS_PER_STEP) for j in range(pps)]
    kern = functools.partial(_sattn_kernel, lam_init=lam_init, dec_seq=dec_seq)
    grid_spec = pltpu.PrefetchScalarGridSpec(
        num_scalar_prefetch=1,
        grid=(n_seq // SAMPLE_SEQS_PER_STEP, n_pg // pps),
        in_specs=[
            pl.BlockSpec((4, HEAD_DIM), lambda b, c, pt: (0, 0)),
            pl.BlockSpec((1, d // N_HEADS), lambda b, c, pt: (0, 0)),
            new_rows,
            pl.BlockSpec((d, d), lambda b, c, pt: (0, 0)),
            new_rows,
            new_rows,
        ] + page_specs + page_specs,
        out_specs=pl.BlockSpec((n_rows, d), lambda b, c, pt: (b, 0)),
        scratch_shapes=[pltpu.VMEM((n_cols, d), BF16), pltpu.VMEM((n_cols, 1), F32),
                        pltpu.VMEM((n_cols, 1), F32), pltpu.VMEM((n_cols, d), F32)],
    )
    n_pages_step = SAMPLE_SEQS_PER_STEP * pps
    return pl.pallas_call(
        kern,
        grid_spec=grid_spec,
        out_shape=jax.ShapeDtypeStruct((n - n_prompt, d), F32),
        compiler_params=_params(("parallel", "arbitrary")),
        name="sample_attention",
    )(pt, lam_rows, subln_g[None], x, w_q.astype(BF16), k_f32, v_f32,
      *([ck] * n_pages_step), *([cv] * n_pages_step))


def _oproj_kernel(ap_ref, as_ref, w_ref, x_ref, g_ref, b_ref, o_ref, *, n_prompt_tiles):
    i = pl.program_id(0)
    a = jnp.where(i >= n_prompt_tiles, as_ref[...].astype(BF16), ap_ref[...])
    m = _dot(a, w_ref[...])
    o_ref[...] = _layer_norm(ALPHA * x_ref[...] + m, g_ref[...], b_ref[...])


def _oproj_layer(a_prompt, a_sample, w_o, x, ln_g, ln_b):
    n, d = x.shape
    tm = TOKEN_TILE
    n_prompt_tiles = a_prompt.shape[0] // tm
    row = pl.BlockSpec((tm, d), lambda i: (i, 0))
    const2 = lambda i: (0, 0)
    kern = functools.partial(_oproj_kernel, n_prompt_tiles=n_prompt_tiles)
    return pl.pallas_call(
        kern,
        grid=(n // tm,),
        in_specs=[
            pl.BlockSpec((tm, d), lambda i: (jnp.minimum(i, n_prompt_tiles - 1), 0)),
            pl.BlockSpec((tm, d), lambda i: (jnp.maximum(i - n_prompt_tiles, 0), 0)),
            pl.BlockSpec((d, d), const2),
            row,
            pl.BlockSpec((1, d), const2),
            pl.BlockSpec((1, d), const2),
        ],
        out_specs=row,
        out_shape=jax.ShapeDtypeStruct((n, d), F32),
        compiler_params=_params(("parallel",)),
        name="attn_out_proj",
    )(a_prompt, a_sample, w_o.astype(BF16), x, ln_g[None], ln_b[None])


def kernel(x_prompt, x_sample, cache_k, cache_v, page_table, ln1_g, ln1_b, ln2_g, ln2_b, a_w_in, a_b_in, a_lnv_g, a_lnv_b, a_w_s, a_b_s, a_w_out, w_k, w_v, b_w_q, b_lam_q1, b_lam_k1, b_lam_q2, b_lam_k2, b_subln_g, b_w_o, router_w, router_bias, moe_w_up, moe_w_down):
    batch, seq, d = x_prompt.shape
    dec_batch, dec_seq, _ = x_sample.shape
    n_prompt = batch * seq
    n_sample = dec_batch * dec_seq
    assert seq % CHUNK == 0 and CHUNK % dec_seq == 0 and dec_seq <= CHUNK
    assert n_prompt % TOKEN_TILE == 0 and n_sample % TOKEN_TILE == 0
    assert seq % ATTN_Q_TILE == 0 and ATTN_Q_TILE % ATTN_K_TILE == 0
    assert a_w_s.shape[-1] == CHUNK and cache_k.shape[1] == PAGE_SIZE
    assert dec_batch % SAMPLE_SEQS_PER_STEP == 0 and page_table.shape[1] % SAMPLE_PAGES_PER_STEP == 0

    x = jnp.concatenate([x_prompt.reshape(n_prompt, d), x_sample.reshape(n_sample, d)], axis=0)
    rwt = router_w.T.astype(BF16)
    rbias = router_bias.astype(F32)[:, None]
    chunk_v = []
    k_f32 = v_f32 = kb = vb = qb = None
    for l in range(DEPTH):
        if l < N_A_LAYERS:
            x, v_rows = _gmlp_layer(x, a_w_in[l], a_b_in[l], a_lnv_g[l], a_lnv_b[l], a_w_s[l], a_b_s[l],
                                    a_w_out[l], ln1_g[l], ln1_b[l], n_prompt=n_prompt, dec_seq=dec_seq)
            chunk_v.append(v_rows.reshape(dec_batch, dec_seq, -1))
        else:
            j = l - N_A_LAYERS
            lam_init = 0.8 - 0.6 * math.exp(-0.3 * l)
            if j == 0:
                k_f32, v_f32, kb, vb, qb = _kvq_proj(x, w_k, w_v, b_w_q[j])
            else:
                qb = _q_proj(x, b_w_q[j], n_prompt)
            lam_rows = jnp.stack([b_lam_q1[j], b_lam_k1[j], b_lam_q2[j], b_lam_k2[j]]).astype(F32)
            a_p = _prompt_attention(qb, kb, vb, lam_rows, b_subln_g[j], lam_init, batch=batch, seq=seq)
            a_s = _sample_attention(x, b_w_q[j], k_f32, v_f32, cache_k, cache_v, page_table, lam_rows,
                                    b_subln_g[j], lam_init, n_prompt=n_prompt, dec_seq=dec_seq)
            x = _oproj_layer(a_p, a_s, b_w_o[j], x, ln1_g[l], ln1_b[l])
        x = _moe_layer(x, rwt, rbias, moe_w_up[l], moe_w_down[l], ln2_g[l], ln2_b[l])

    kh = N_HEADS
    y_prompt = x[:n_prompt].reshape(batch, seq, d)
    y_sample = x[n_prompt:].reshape(dec_batch, dec_seq, d)
    k_prompt = k_f32[:n_prompt].reshape(batch, seq, kh, d // kh)
    v_prompt = v_f32[:n_prompt].reshape(batch, seq, kh, d // kh)
    k_sample = k_f32[n_prompt:].reshape(dec_batch, dec_seq, kh, d // kh)
    v_sample = v_f32[n_prompt:].reshape(dec_batch, dec_seq, kh, d // kh)
    return (y_prompt, y_sample, k_prompt, v_prompt, k_sample, v_sample, jnp.stack(chunk_v))
```

```python
import functools
import math

import jax
import jax.numpy as jnp
from jax import lax
from jax.experimental import pallas as pl
from jax.experimental.pallas import tpu as pltpu

F32 = jnp.float32
BF16 = jnp.bfloat16

DEPTH = 4
N_A_LAYERS = DEPTH // 2
CHUNK = 128
A_GROUPS = 8
N_HEADS = 8
HEAD_DIM = 64
N_EXPERTS = 16
N_EXPERT_GROUPS = 4
EXPERTS_PER_GROUP = N_EXPERTS // N_EXPERT_GROUPS
ALPHA = (2.0 * DEPTH) ** 0.25
LN_EPS = 1e-5
NEG = -1e30
PAGE_SIZE = 128
Q_SCALE = HEAD_DIM ** -0.5 * math.log2(math.e)

LANES = 128
SUBLANES = 8
TOKEN_TILE = 512
ATTN_Q_TILE = 512
ATTN_K_TILE = 512
SAMPLE_SEQS_PER_STEP = 2
SAMPLE_PAGES_PER_STEP = 4
VMEM_LIMIT = 56 * 1024 * 1024


def _params(semantics):
    return pltpu.CompilerParams(dimension_semantics=semantics, vmem_limit_bytes=VMEM_LIMIT)


def _layer_norm(x, g, b):
    mu = jnp.mean(x, axis=-1, keepdims=True)
    xc = x - mu
    var = jnp.mean(xc * xc, axis=-1, keepdims=True)
    return xc * lax.rsqrt(var + LN_EPS) * g + b


def _dot(a, b):
    return jnp.dot(a, b, preferred_element_type=F32)


def _dot_nt(a, b):
    return lax.dot_general(a, b, (((1,), (1,)), ((), ())), preferred_element_type=F32)


def _lo_index(n_prompt_tiles):
    return lambda i: (jnp.minimum(i, n_prompt_tiles - 1), 0)


def _hi_index(n_prompt_tiles):
    return lambda i: (jnp.maximum(i - n_prompt_tiles, 0), 0)


def _gmlp_kernel(xp_ref, xs_ref, win_ref, bin_ref, lvg_ref, lvb_ref, ws_ref, sb_ref,
                 wout_ref, g1_ref, b1_ref, xo_ref, v_ref, *, n_prompt_tiles, dec_seq):
    i = pl.program_id(0)
    is_sample = i >= n_prompt_tiles
    a_inner = wout_ref.shape[0]
    gd = a_inner // A_GROUPS
    tm = xp_ref.shape[0]

    x = jnp.where(is_sample, xs_ref[...], xp_ref[...])
    z = _dot(x.astype(BF16), win_ref[...]) + bin_ref[...]
    z = jax.nn.gelu(z)
    u = z[:, :a_inner]
    v = _layer_norm(z[:, a_inner:], lvg_ref[...], lvb_ref[...])

    @pl.when(is_sample)
    def _():
        v_ref[...] = v

    row = lax.broadcasted_iota(jnp.int32, (CHUNK, CHUNK), 0)
    col = lax.broadcasted_iota(jnp.int32, (CHUNK, CHUNK), 1)
    mask = jnp.logical_and(col <= row, jnp.logical_or(jnp.logical_not(is_sample),
                                                      (row // dec_seq) == (col // dec_seq)))
    vb = v.astype(BF16)
    kind = is_sample.astype(jnp.int32)
    bias = sb_ref[kind]
    s_rows = []
    mixes = [jnp.where(mask, ws_ref[kind, g], 0.0).astype(BF16) for g in range(A_GROUPS)]
    for c in range(tm // CHUNK):
        cols = []
        for g in range(A_GROUPS):
            blk = vb[c * CHUNK:(c + 1) * CHUNK, g * gd:(g + 1) * gd]
            cols.append(_dot(mixes[g], blk))
        s_rows.append(jnp.concatenate(cols, axis=1) + bias)
    s = jnp.concatenate(s_rows, axis=0)
    m = _dot((u * s).astype(BF16), wout_ref[...])
    xo_ref[...] = _layer_norm(ALPHA * x + m, g1_ref[...], b1_ref[...])


def _gmlp_layer(x_prompt, x_sample, prompt_index, sample_index, w_in, b_in, lnv_g, lnv_b, w_s, b_s, w_out,
                ln_g, ln_b, *, n_prompt, n_sample, dec_seq):
    d = x_prompt.shape[1]
    a_inner = w_out.shape[0]
    tm = TOKEN_TILE
    n = n_prompt + n_sample
    n_prompt_tiles = n_prompt // tm
    reps = CHUNK // dec_seq
    ws = jnp.stack([w_s, jnp.tile(w_s[:, :dec_seq, :dec_seq], (1, reps, reps))])
    gd = a_inner // A_GROUPS
    sb_p = jnp.repeat(b_s.T, gd, axis=1)
    sb_s = jnp.repeat(jnp.tile(b_s[:, :dec_seq].T, (reps, 1)), gd, axis=1)
    sb = jnp.stack([sb_p, sb_s])
    const2 = lambda i: (0, 0)
    const3 = lambda i: (0, 0, 0)
    const4 = lambda i: (0, 0, 0, 0)
    kern = functools.partial(_gmlp_kernel, n_prompt_tiles=n_prompt_tiles, dec_seq=dec_seq)
    return pl.pallas_call(
        kern,
        grid=(n // tm,),
        in_specs=[
            pl.BlockSpec((tm, d), prompt_index),
            pl.BlockSpec((tm, d), sample_index),
            pl.BlockSpec((d, 2 * a_inner), const2),
            pl.BlockSpec((1, 2 * a_inner), const2),
            pl.BlockSpec((1, a_inner), const2),
            pl.BlockSpec((1, a_inner), const2),
            pl.BlockSpec((2, A_GROUPS, CHUNK, CHUNK), const4),
            pl.BlockSpec((2, CHUNK, a_inner), const3),
            pl.BlockSpec((a_inner, d), const2),
            pl.BlockSpec((1, d), const2),
            pl.BlockSpec((1, d), const2),
        ],
        out_specs=[
            pl.BlockSpec((tm, d), lambda i: (i, 0)),
            pl.BlockSpec((tm, a_inner), _hi_index(n_prompt_tiles)),
        ],
        out_shape=[jax.ShapeDtypeStruct((n, d), F32), jax.ShapeDtypeStruct((n_sample, a_inner), F32)],
        compiler_params=_params(("arbitrary",)),
        name="gmlp_layer",
    )(x_prompt, x_sample, w_in.astype(BF16), b_in[None], lnv_g[None], lnv_b[None], ws, sb,
      w_out.astype(BF16), ln_g[None], ln_b[None])


def _route(x_bf, rwt_ref, rbias_ref):
    logits = _dot_nt(rwt_ref[...], x_bf)
    mx = jnp.max(logits, axis=0, keepdims=True)
    ex = jnp.exp(logits - mx)
    probs = ex / jnp.sum(ex, axis=0, keepdims=True)
    sel = probs + rbias_ref[...]
    p = [probs[e:e + 1, :] for e in range(N_EXPERTS)]
    s = [sel[e:e + 1, :] for e in range(N_EXPERTS)]
    gscore = []
    for g in range(N_EXPERT_GROUPS):
        a, b, c, d = s[4 * g:4 * g + 4]
        hi1, lo1 = jnp.maximum(a, b), jnp.minimum(a, b)
        hi2, lo2 = jnp.maximum(c, d), jnp.minimum(c, d)
        top1 = jnp.maximum(hi1, hi2)
        top2 = jnp.maximum(jnp.minimum(hi1, hi2), jnp.maximum(lo1, lo2))
        gscore.append(top1 + top2)
    best = gscore[0]
    gbest = jnp.zeros_like(best, dtype=jnp.int32)
    for g in range(1, N_EXPERT_GROUPS):
        better = gscore[g] > best
        best = jnp.where(better, gscore[g], best)
        gbest = jnp.where(better, g, gbest)
    gates = []
    for e in range(N_EXPERTS):
        g = e // EXPERTS_PER_GROUP
        rank = jnp.zeros_like(gbest)
        for f in range(EXPERTS_PER_GROUP * g, EXPERTS_PER_GROUP * (g + 1)):
            if f == e:
                continue
            beats = (s[f] >= s[e]) if f < e else (s[f] > s[e])
            rank = rank + beats.astype(jnp.int32)
        chosen = (gbest == g) & (rank < 2)
        gates.append(jnp.where(chosen, p[e], 0.0))
    denom = gates[0]
    for e in range(1, N_EXPERTS):
        denom = denom + gates[e]
    return [gt / denom for gt in gates]


def _moe_kernel(x_ref, rwt_ref, rbias_ref, wup_ref, wdn_ref, g_ref, b_ref, *rest, n_prompt_tiles, split):
    if split:
        op_ref, os_ref, acc_ref, comb_ref = rest
    else:
        o_ref, acc_ref, comb_ref = rest
    i = pl.program_id(0)
    e = pl.program_id(1)
    d_exp = wdn_ref.shape[1]
    x = x_ref[...]
    xb = x.astype(BF16)

    @pl.when(e == 0)
    def _():
        comb = _route(xb, rwt_ref, rbias_ref)
        tm = x.shape[0]
        rows = jnp.concatenate(comb + [jnp.zeros((LANES - N_EXPERTS, tm), F32)], axis=0)
        comb_ref[...] = rows.T
        acc_ref[...] = jnp.zeros_like(acc_ref)

    lane = lax.broadcasted_iota(jnp.int32, comb_ref.shape, 1)
    c_e = jnp.sum(jnp.where(lane == e, comb_ref[...], 0.0), axis=1, keepdims=True)
    h = _dot(xb, wup_ref[0])
    act = jax.nn.silu(h[:, :d_exp]) * h[:, d_exp:]
    acc_ref[...] += _dot((act * c_e).astype(BF16), wdn_ref[0])

    last = e == N_EXPERTS - 1
    if split:
        @pl.when(last & (i < n_prompt_tiles))
        def _():
            op_ref[...] = _layer_norm(ALPHA * x + acc_ref[...], g_ref[...], b_ref[...])

        @pl.when(last & (i >= n_prompt_tiles))
        def _():
            os_ref[...] = _layer_norm(ALPHA * x + acc_ref[...], g_ref[...], b_ref[...])
    else:
        @pl.when(last)
        def _():
            o_ref[...] = _layer_norm(ALPHA * x + acc_ref[...], g_ref[...], b_ref[...])


def _moe_layer(x, rwt, rbias, w_up, w_down, ln_g, ln_b, *, n_prompt, split):
    n, d = x.shape
    tm = TOKEN_TILE
    d_exp = w_down.shape[1]
    n_prompt_tiles = n_prompt // tm
    const2 = lambda i, e: (0, 0)
    if split:
        lo, hi = _lo_index(n_prompt_tiles), _hi_index(n_prompt_tiles)
        out_specs = [pl.BlockSpec((tm, d), lambda i, e: lo(i)), pl.BlockSpec((tm, d), lambda i, e: hi(i))]
        out_shape = [jax.ShapeDtypeStruct((n_prompt, d), F32), jax.ShapeDtypeStruct((n - n_prompt, d), F32)]
    else:
        out_specs = pl.BlockSpec((tm, d), lambda i, e: (i, 0))
        out_shape = jax.ShapeDtypeStruct((n, d), F32)
    return pl.pallas_call(
        functools.partial(_moe_kernel, n_prompt_tiles=n_prompt_tiles, split=split),
        grid=(n // tm, N_EXPERTS),
        in_specs=[
            pl.BlockSpec((tm, d), lambda i, e: (i, 0)),
            pl.BlockSpec((N_EXPERTS, d), const2),
            pl.BlockSpec((N_EXPERTS, 1), const2),
            pl.BlockSpec((1, d, 2 * d_exp), lambda i, e: (e, 0, 0)),
            pl.BlockSpec((1, d_exp, d), lambda i, e: (e, 0, 0)),
            pl.BlockSpec((1, d), const2),
            pl.BlockSpec((1, d), const2),
        ],
        out_specs=out_specs,
        out_shape=out_shape,
        scratch_shapes=[pltpu.VMEM((tm, d), F32), pltpu.VMEM((tm, LANES), F32)],
        compiler_params=_params(("arbitrary", "arbitrary")),
        name="moe_layer",
    )(x, rwt, rbias, w_up.astype(BF16), w_down.astype(BF16), ln_g[None], ln_b[None])


def _kvq_kernel(x_ref, wk_ref, wv_ref, wq_ref, kp_ref, vp_ref, ks_ref, vs_ref, kb_ref, vb_ref, qb_ref,
                *, n_prompt_tiles):
    i = pl.program_id(0)
    xb = x_ref[...].astype(BF16)
    k = _dot(xb, wk_ref[...])
    v = _dot(xb, wv_ref[...])
    tm = k.shape[0]
    vd = k.shape[1] // N_HEADS

    def store_heads(ref, val):
        for h in range(N_HEADS):
            ref[pl.ds(h, tm, stride=N_HEADS), :] = val[:, h * vd:(h + 1) * vd]

    @pl.when(i < n_prompt_tiles)
    def _():
        store_heads(kp_ref, k)
        store_heads(vp_ref, v)
        kb_ref[...] = k.astype(BF16)
        vb_ref[...] = v.astype(BF16)
        qb_ref[...] = (_dot(xb, wq_ref[...]) * Q_SCALE).astype(BF16)

    @pl.when(i >= n_prompt_tiles)
    def _():
        store_heads(ks_ref, k)
        store_heads(vs_ref, v)


def _kvq_proj(x, w_k, w_v, w_q, *, n_prompt):
    n, d = x.shape
    tm = TOKEN_TILE
    n_prompt_tiles = n_prompt // tm
    n_sample = n - n_prompt
    vd = d // N_HEADS
    lo = pl.BlockSpec((tm, d), _lo_index(n_prompt_tiles))
    lo_heads = pl.BlockSpec((tm * N_HEADS, vd), _lo_index(n_prompt_tiles))
    hi_heads = pl.BlockSpec((tm * N_HEADS, vd), _hi_index(n_prompt_tiles))
    wspec = pl.BlockSpec((d, d), lambda i: (0, 0))
    return pl.pallas_call(
        functools.partial(_kvq_kernel, n_prompt_tiles=n_prompt_tiles),
        grid=(n // tm,),
        in_specs=[pl.BlockSpec((tm, d), lambda i: (i, 0)), wspec, wspec, wspec],
        out_specs=[lo_heads, lo_heads, hi_heads, hi_heads, lo, lo, lo],
        out_shape=[jax.ShapeDtypeStruct((n_prompt * N_HEADS, vd), F32),
                   jax.ShapeDtypeStruct((n_prompt * N_HEADS, vd), F32),
                   jax.ShapeDtypeStruct((n_sample * N_HEADS, vd), F32),
                   jax.ShapeDtypeStruct((n_sample * N_HEADS, vd), F32),
                   jax.ShapeDtypeStruct((n_prompt, d), BF16), jax.ShapeDtypeStruct((n_prompt, d), BF16),
                   jax.ShapeDtypeStruct((n_prompt, d), BF16)],
        compiler_params=_params(("arbitrary",)),
        name="kvq_proj",
    )(x, w_k.astype(BF16), w_v.astype(BF16), w_q.astype(BF16))


def _q_kernel(x_ref, wq_ref, qb_ref):
    qb_ref[...] = (_dot(x_ref[...].astype(BF16), wq_ref[...]) * Q_SCALE).astype(BF16)


def _q_proj(x, w_q, n_rows):
    d = x.shape[1]
    tm = TOKEN_TILE
    row = pl.BlockSpec((tm, d), lambda i: (i, 0))
    return pl.pallas_call(
        _q_kernel,
        grid=(n_rows // tm,),
        in_specs=[row, pl.BlockSpec((d, d), lambda i: (0, 0))],
        out_specs=row,
        out_shape=jax.ShapeDtypeStruct((n_rows, d), BF16),
        compiler_params=_params(("parallel",)),
        name="q_proj",
    )(x, w_q.astype(BF16))


def _diff_lambda(lam_ref, lam_init):
    lv = lam_ref[...]
    e1 = jnp.exp(jnp.sum(lv[0:1, :] * lv[1:2, :], axis=1, keepdims=True))
    e2 = jnp.exp(jnp.sum(lv[2:3, :] * lv[3:4, :], axis=1, keepdims=True))
    return e1 - e2 + lam_init


def _lane_tile(stat, n):
    return stat if n == LANES else jnp.concatenate([stat] * (n // LANES), axis=1)


def _softmax_step(s, vc, m_ref, l_ref, acc_ref, rows=None):
    sl = slice(None) if rows is None else rows
    m_old = m_ref[sl, :]
    m_new = jnp.maximum(m_old, jnp.max(s, axis=1, keepdims=True))
    alpha = jnp.exp2(m_old - m_new)
    p = jnp.exp2(s - _lane_tile(m_new, s.shape[1]))
    l_ref[sl, :] = alpha * l_ref[sl, :] + jnp.sum(p, axis=1, keepdims=True)
    acc_ref[sl, :] = alpha * acc_ref[sl, :] + _dot(p.astype(BF16), vc)
    m_ref[sl, :] = m_new


def _head_norm(o, g, lam_init):
    return o * lax.rsqrt(jnp.mean(o * o, axis=-1, keepdims=True) + LN_EPS) * g * (1.0 - lam_init)


def _pattn_kernel(lam_ref, g_ref, q_ref, k_ref, v_ref, o_ref,
                  m1_ref, l1_ref, a1_ref, m2_ref, l2_ref, a2_ref, *, lam_init):
    qi = pl.program_id(2)
    tq = q_ref.shape[0]
    tk = ATTN_K_TILE
    per_tile = tq // tk
    hd = HEAD_DIM
    q = q_ref[...]
    q1, q2 = q[:, :hd], q[:, hd:]
    for m_ref, l_ref, a_ref in ((m1_ref, l1_ref, a1_ref), (m2_ref, l2_ref, a2_ref)):
        m_ref[...] = jnp.full_like(m_ref, NEG)
        l_ref[...] = jnp.zeros_like(l_ref)
        a_ref[...] = jnp.zeros_like(a_ref)

    def chunk(start, mask):
        kc = k_ref[pl.ds(start, tk), :]
        vc = v_ref[pl.ds(start, tk), :]
        s1 = _dot_nt(q1, kc[:, :hd])
        s2 = _dot_nt(q2, kc[:, hd:])
        if mask is not None:
            s1 = jnp.where(mask, s1, NEG)
            s2 = jnp.where(mask, s2, NEG)
        _softmax_step(s1, vc, m1_ref, l1_ref, a1_ref)
        _softmax_step(s2, vc, m2_ref, l2_ref, a2_ref)

    n_full = qi * per_tile

    def body(j, carry):
        chunk(pl.multiple_of(2 * j * tk, tk), None)
        chunk(pl.multiple_of((2 * j + 1) * tk, tk), None)
        return carry

    lax.fori_loop(0, n_full // 2, body, 0)

    @pl.when(n_full % 2 == 1)
    def _():
        chunk(pl.multiple_of((n_full - 1) * tk, tk), None)

    row = lax.broadcasted_iota(jnp.int32, (tq, tk), 0)
    col = lax.broadcasted_iota(jnp.int32, (tq, tk), 1)
    for jj in range(per_tile):
        chunk(pl.multiple_of(qi * tq + jj * tk, tk), row >= col + jj * tk)

    lam = _diff_lambda(lam_ref, lam_init)
    o = a1_ref[...] / l1_ref[...] - lam * (a2_ref[...] / l2_ref[...])
    o_ref[...] = _head_norm(o, g_ref[...], lam_init).astype(o_ref.dtype)


def _prompt_attention(qb, kb, vb, lam_rows, subln_g, lam_init, *, batch, seq):
    d = qb.shape[1]
    vd = d // N_HEADS
    tq = ATTN_Q_TILE
    nq = seq // tq
    kern = functools.partial(_pattn_kernel, lam_init=lam_init)
    stat = pltpu.VMEM((tq, LANES), F32)
    return pl.pallas_call(
        kern,
        grid=(batch, N_HEADS, nq),
        in_specs=[
            pl.BlockSpec((4, HEAD_DIM), lambda b, h, i: (0, 0)),
            pl.BlockSpec((1, vd), lambda b, h, i: (0, 0)),
            pl.BlockSpec((tq, vd), lambda b, h, i: (b * nq + i, h)),
            pl.BlockSpec((seq, vd), lambda b, h, i: (b, h)),
            pl.BlockSpec((seq, vd), lambda b, h, i: (b, h)),
        ],
        out_specs=pl.BlockSpec((tq, vd), lambda b, h, i: (b * nq + i, h)),
        out_shape=jax.ShapeDtypeStruct((batch * seq, d), BF16),
        scratch_shapes=[stat, stat, pltpu.VMEM((tq, vd), F32), stat, stat, pltpu.VMEM((tq, vd), F32)],
        compiler_params=_params(("parallel", "parallel", "arbitrary")),
        name="prompt_attention",
    )(lam_rows, subln_g[None], qb, kb, vb)


def _sattn_kernel(pt_ref, lam_ref, g_ref, x_ref, wq_ref, kn_ref, vn_ref, *rest, lam_init, dec_seq):
    npg = SAMPLE_SEQS_PER_STEP * SAMPLE_PAGES_PER_STEP
    k_refs = rest[:npg]
    v_refs = rest[npg:2 * npg]
    o_ref = rest[2 * npg]
    qbd_ref, m_ref, l_ref, acc_ref = rest[2 * npg + 1:]
    c = pl.program_id(1)
    n_rows = SAMPLE_SEQS_PER_STEP * dec_seq
    vd = qbd_ref.shape[1]
    hrows = 2 * n_rows

    @pl.when(c == 0)
    def _():
        q = _dot(x_ref[...].astype(BF16), wq_ref[...]) * Q_SCALE
        lane = lax.broadcasted_iota(jnp.int32, (n_rows, vd), 1)
        parts = []
        for h in range(N_HEADS):
            qh = q[:, h * vd:(h + 1) * vd]
            parts += [jnp.where(lane < HEAD_DIM, qh, 0.0), jnp.where(lane >= HEAD_DIM, qh, 0.0)]
        qbd_ref[...] = jnp.concatenate(parts, axis=0).astype(BF16)
        m_ref[...] = jnp.full_like(m_ref, NEG)
        l_ref[...] = jnp.zeros_like(l_ref)
        acc_ref[...] = jnp.zeros_like(acc_ref)

    keys_per_seq = SAMPLE_PAGES_PER_STEP * PAGE_SIZE
    n_keys = SAMPLE_SEQS_PER_STEP * keys_per_seq
    n_cols = N_HEADS * hrows
    rr = lax.broadcasted_iota(jnp.int32, (n_cols, n_keys), 0)
    kk = lax.broadcasted_iota(jnp.int32, (n_cols, n_keys), 1)
    same_seq = (rr % n_rows) // dec_seq == kk // keys_per_seq

    def head_rows(refs, h):
        head = pl.ds(h, PAGE_SIZE, stride=N_HEADS)
        return jnp.concatenate([r.at[0][head, :] for r in refs], axis=0).astype(BF16)

    s = jnp.concatenate([_dot_nt(qbd_ref[h * hrows:(h + 1) * hrows, :], head_rows(k_refs, h))
                         for h in range(N_HEADS)], axis=0)
    s = jnp.where(same_seq, s, NEG)
    m_old = m_ref[...]
    m_new = jnp.maximum(m_old, jnp.max(s, axis=1, keepdims=True))
    alpha = jnp.exp2(m_old - m_new)
    p = jnp.exp2(s - _lane_tile(m_new, n_keys))
    l_ref[...] = alpha * l_ref[...] + jnp.sum(p, axis=1, keepdims=True)
    pb = p.astype(BF16)
    pv = jnp.concatenate([_dot(pb[h * hrows:(h + 1) * hrows, :], head_rows(v_refs, h))
                          for h in range(N_HEADS)], axis=0)
    acc_ref[...] = alpha * acc_ref[...] + pv
    m_ref[...] = m_new

    @pl.when(c == pl.num_programs(1) - 1)
    def _():
        lam = _diff_lambda(lam_ref, lam_init)
        r2 = lax.broadcasted_iota(jnp.int32, (hrows, LANES), 0) % n_rows
        k2 = lax.broadcasted_iota(jnp.int32, (hrows, LANES), 1)
        ok = (k2 < n_rows) & (k2 // dec_seq == r2 // dec_seq) & (k2 % dec_seq <= r2 % dec_seq)
        pad = jnp.zeros((LANES - n_rows, vd), F32)
        outs = []
        for h in range(N_HEADS):
            new_head = pl.ds(h, n_rows, stride=N_HEADS)
            kn = jnp.concatenate([kn_ref[new_head, :], pad], axis=0).astype(BF16)
            vn = jnp.concatenate([vn_ref[new_head, :], pad], axis=0).astype(BF16)
            rows = slice(h * hrows, (h + 1) * hrows)
            sn = jnp.where(ok, _dot_nt(qbd_ref[rows, :], kn), NEG)
            _softmax_step(sn, vn, m_ref, l_ref, acc_ref, rows)
            o = acc_ref[rows, :] / l_ref[rows, :]
            outs.append(_head_norm(o[:n_rows] - lam * o[n_rows:], g_ref[...], lam_init))
        o_ref[...] = jnp.concatenate(outs, axis=1)


def _sample_attention(x, w_q, k_new, v_new, cache_k, cache_v, page_table, lam_rows, subln_g, lam_init,
                      *, n_prompt, dec_seq):
    n, d = x.shape
    vd = d // N_HEADS
    n_seq, n_pg = page_table.shape
    n_rows = SAMPLE_SEQS_PER_STEP * dec_seq
    n_cols = 2 * N_HEADS * n_rows
    pps = SAMPLE_PAGES_PER_STEP
    slab = PAGE_SIZE * N_HEADS
    ck = cache_k.reshape(cache_k.shape[0], slab, vd)
    cv = cache_v.reshape(cache_v.shape[0], slab, vd)
    pt = page_table.reshape(-1)
    base = n_prompt // n_rows

    def page_spec(s, j):
        return pl.BlockSpec((1, slab, vd),
                            lambda b, c, pt: (pt[(b * SAMPLE_SEQS_PER_STEP + s) * n_pg + c * pps + j], 0, 0))

    page_specs = [page_spec(s, j) for s in range(SAMPLE_SEQS_PER_STEP) for j in range(pps)]
    new_rows = pl.BlockSpec((n_rows * N_HEADS, vd), lambda b, c, pt: (b, 0))
    kern = functools.partial(_sattn_kernel, lam_init=lam_init, dec_seq=dec_seq)
    stat = pltpu.VMEM((n_cols, LANES), F32)
    grid_spec = pltpu.PrefetchScalarGridSpec(
        num_scalar_prefetch=1,
        grid=(n_seq // SAMPLE_SEQS_PER_STEP, n_pg // pps),
        in_specs=[
            pl.BlockSpec((4, HEAD_DIM), lambda b, c, pt: (0, 0)),
            pl.BlockSpec((1, vd), lambda b, c, pt: (0, 0)),
            pl.BlockSpec((n_rows, d), lambda b, c, pt: (base + b, 0)),
            pl.BlockSpec((d, d), lambda b, c, pt: (0, 0)),
            new_rows,
            new_rows,
        ] + page_specs + page_specs,
        out_specs=pl.BlockSpec((n_rows, d), lambda b, c, pt: (b, 0)),
        scratch_shapes=[pltpu.VMEM((n_cols, vd), BF16), stat, stat, pltpu.VMEM((n_cols, vd), F32)],
    )
    n_pages_step = SAMPLE_SEQS_PER_STEP * pps
    return pl.pallas_call(
        kern,
        grid_spec=grid_spec,
        out_shape=jax.ShapeDtypeStruct((n - n_prompt, d), F32),
        compiler_params=_params(("parallel", "arbitrary")),
        name="sample_attention",
    )(pt, lam_rows, subln_g[None], x, w_q.astype(BF16), k_new, v_new,
      *([ck] * n_pages_step), *([cv] * n_pages_step))


def _oproj_kernel(ap_ref, as_ref, w_ref, x_ref, g_ref, b_ref, o_ref, *, n_prompt_tiles):
    i = pl.program_id(0)
    a = jnp.where(i >= n_prompt_tiles, as_ref[...].astype(BF16), ap_ref[...])
    m = _dot(a, w_ref[...])
    o_ref[...] = _layer_norm(ALPHA * x_ref[...] + m, g_ref[...], b_ref[...])


def _oproj_layer(a_prompt, a_sample, w_o, x, ln_g, ln_b):
    n, d = x.shape
    tm = TOKEN_TILE
    n_prompt_tiles = a_prompt.shape[0] // tm
    row = pl.BlockSpec((tm, d), lambda i: (i, 0))
    const2 = lambda i: (0, 0)
    kern = functools.partial(_oproj_kernel, n_prompt_tiles=n_prompt_tiles)
    return pl.pallas_call(
        kern,
        grid=(n // tm,),
        in_specs=[
            pl.BlockSpec((tm, d), _lo_index(n_prompt_tiles)),
            pl.BlockSpec((tm, d), _hi_index(n_prompt_tiles)),
            pl.BlockSpec((d, d), const2),
            row,
            pl.BlockSpec((1, d), const2),
            pl.BlockSpec((1, d), const2),
        ],
        out_specs=row,
        out_shape=jax.ShapeDtypeStruct((n, d), F32),
        compiler_params=_params(("parallel",)),
        name="attn_out_proj",
    )(a_prompt, a_sample, w_o.astype(BF16), x, ln_g[None], ln_b[None])


def kernel(x_prompt, x_sample, cache_k, cache_v, page_table, ln1_g, ln1_b, ln2_g, ln2_b, a_w_in, a_b_in, a_lnv_g, a_lnv_b, a_w_s, a_b_s, a_w_out, w_k, w_v, b_w_q, b_lam_q1, b_lam_k1, b_lam_q2, b_lam_k2, b_subln_g, b_w_o, router_w, router_bias, moe_w_up, moe_w_down):
    batch, seq, d = x_prompt.shape
    dec_batch, dec_seq, _ = x_sample.shape
    n_prompt = batch * seq
    n_sample = dec_batch * dec_seq
    tm = TOKEN_TILE
    assert seq % CHUNK == 0 and CHUNK % dec_seq == 0 and dec_seq <= CHUNK
    assert n_prompt % tm == 0 and n_sample % tm == 0
    assert seq % ATTN_Q_TILE == 0 and ATTN_Q_TILE % ATTN_K_TILE == 0
    assert a_w_s.shape[-1] == CHUNK and cache_k.shape[1] == PAGE_SIZE and cache_k.shape[2] == N_HEADS
    assert dec_batch % SAMPLE_SEQS_PER_STEP == 0 and page_table.shape[1] % SAMPLE_PAGES_PER_STEP == 0
    assert SAMPLE_SEQS_PER_STEP * dec_seq == SUBLANES

    n_prompt_tiles = n_prompt // tm
    rwt = router_w.T.astype(BF16)
    rbias = router_bias.astype(F32)[:, None]
    chunk_v = []
    x = None
    for l in range(DEPTH):
        if l < N_A_LAYERS:
            if l == 0:
                xa, xb = x_prompt.reshape(n_prompt, d), x_sample.reshape(n_sample, d)
                ia, ib = _lo_index(n_prompt_tiles), _hi_index(n_prompt_tiles)
            else:
                xa, xb = x, x
                ia, ib = _lo_index(n_prompt_tiles), (lambda i: (jnp.maximum(i, n_prompt_tiles), 0))
            x, v_rows = _gmlp_layer(xa, xb, ia, ib, a_w_in[l], a_b_in[l], a_lnv_g[l], a_lnv_b[l], a_w_s[l],
                                    a_b_s[l], a_w_out[l], ln1_g[l], ln1_b[l],
                                    n_prompt=n_prompt, n_sample=n_sample, dec_seq=dec_seq)
            chunk_v.append(v_rows.reshape(dec_batch, dec_seq, -1))
        else:
            j = l - N_A_LAYERS
            lam_init = 0.8 - 0.6 * math.exp(-0.3 * l)
            if j == 0:
                kp, vp, ks, vs, kb, vb, qb = _kvq_proj(x, w_k, w_v, b_w_q[j], n_prompt=n_prompt)
            else:
                qb = _q_proj(x, b_w_q[j], n_prompt)
            lam_rows = jnp.stack([b_lam_q1[j], b_lam_k1[j], b_lam_q2[j], b_lam_k2[j]]).astype(F32)
            a_p = _prompt_attention(qb, kb, vb, lam_rows, b_subln_g[j], lam_init, batch=batch, seq=seq)
            a_s = _sample_attention(x, b_w_q[j], ks, vs, cache_k, cache_v, page_table, lam_rows,
                                    b_subln_g[j], lam_init, n_prompt=n_prompt, dec_seq=dec_seq)
            x = _oproj_layer(a_p, a_s, b_w_o[j], x, ln1_g[l], ln1_b[l])
        x = _moe_layer(x, rwt, rbias, moe_w_up[l], moe_w_down[l], ln2_g[l], ln2_b[l],
                       n_prompt=n_prompt, split=(l == DEPTH - 1))

    yp, ys = x
    vd = d // N_HEADS
    return (yp.reshape(batch, seq, d), ys.reshape(dec_batch, dec_seq, d),
            kp.reshape(batch, seq, N_HEADS, vd), vp.reshape(batch, seq, N_HEADS, vd),
            ks.reshape(dec_batch, dec_seq, N_HEADS, vd), vs.reshape(dec_batch, dec_seq, N_HEADS, vd),
            jnp.stack(chunk_v))
```

```python
import functools
import math

import jax
import jax.numpy as jnp
from jax import lax
from jax.experimental import pallas as pl
from jax.experimental.pallas import tpu as pltpu

F32 = jnp.float32
BF16 = jnp.bfloat16

DEPTH = 4
N_A_LAYERS = DEPTH // 2
CHUNK = 128
A_GROUPS = 8
N_HEADS = 8
HEAD_DIM = 64
N_EXPERTS = 16
N_EXPERT_GROUPS = 4
EXPERTS_PER_GROUP = N_EXPERTS // N_EXPERT_GROUPS
ALPHA = (2.0 * DEPTH) ** 0.25
LN_EPS = 1e-5
NEG = -1e30
PAGE_SIZE = 128
Q_SCALE = HEAD_DIM ** -0.5 * math.log2(math.e)

LANES = 128
SUBLANES = 8
TOKEN_TILE = 512
ATTN_Q_TILE = 512
ATTN_K_TILE = 512
SAMPLE_SEQS_PER_STEP = 2
SAMPLE_PAGES_PER_STEP = 4
VMEM_LIMIT = 56 * 1024 * 1024


def _params(semantics):
    return pltpu.CompilerParams(dimension_semantics=semantics, vmem_limit_bytes=VMEM_LIMIT)


def _layer_norm(x, g, b):
    mu = jnp.mean(x, axis=-1, keepdims=True)
    xc = x - mu
    var = jnp.mean(xc * xc, axis=-1, keepdims=True)
    return xc * lax.rsqrt(var + LN_EPS) * g + b


def _dot(a, b):
    return jnp.dot(a, b, preferred_element_type=F32)


def _dot_nt(a, b):
    return lax.dot_general(a, b, (((1,), (1,)), ((), ())), preferred_element_type=F32)


def _lo_index(n_prompt_tiles):
    return lambda i: (jnp.minimum(i, n_prompt_tiles - 1), 0)


def _hi_index(n_prompt_tiles):
    return lambda i: (jnp.maximum(i - n_prompt_tiles, 0), 0)


def _gmlp_kernel(xp_ref, xs_ref, win_ref, bin_ref, lvg_ref, lvb_ref, ws_ref, sb_ref,
                 wout_ref, g1_ref, b1_ref, xo_ref, v_ref, *, n_prompt_tiles, dec_seq):
    i = pl.program_id(0)
    is_sample = i >= n_prompt_tiles
    a_inner = wout_ref.shape[0]
    gd = a_inner // A_GROUPS
    tm = xp_ref.shape[0]

    x = jnp.where(is_sample, xs_ref[...], xp_ref[...])
    z = _dot(x.astype(BF16), win_ref[...]) + bin_ref[...]
    z = jax.nn.gelu(z)
    u = z[:, :a_inner]
    v = _layer_norm(z[:, a_inner:], lvg_ref[...], lvb_ref[...])

    @pl.when(is_sample)
    def _():
        v_ref[...] = v

    row = lax.broadcasted_iota(jnp.int32, (CHUNK, CHUNK), 0)
    col = lax.broadcasted_iota(jnp.int32, (CHUNK, CHUNK), 1)
    mask = jnp.logical_and(col <= row, jnp.logical_or(jnp.logical_not(is_sample),
                                                      (row // dec_seq) == (col // dec_seq)))
    vb = v.astype(BF16)
    kind = is_sample.astype(jnp.int32)
    bias = sb_ref[kind]
    s_rows = []
    mixes = [jnp.where(mask, ws_ref[kind, g], 0.0).astype(BF16) for g in range(A_GROUPS)]
    for c in range(tm // CHUNK):
        cols = []
        for g in range(A_GROUPS):
            blk = vb[c * CHUNK:(c + 1) * CHUNK, g * gd:(g + 1) * gd]
            cols.append(_dot(mixes[g], blk))
        s_rows.append(jnp.concatenate(cols, axis=1) + bias)
    s = jnp.concatenate(s_rows, axis=0)
    m = _dot((u * s).astype(BF16), wout_ref[...])
    xo_ref[...] = _layer_norm(ALPHA * x + m, g1_ref[...], b1_ref[...])


def _gmlp_layer(x_prompt, x_sample, prompt_index, sample_index, w_in, b_in, lnv_g, lnv_b, w_s, b_s, w_out,
                ln_g, ln_b, *, n_prompt, n_sample, dec_seq):
    d = x_prompt.shape[1]
    a_inner = w_out.shape[0]
    tm = TOKEN_TILE
    n = n_prompt + n_sample
    n_prompt_tiles = n_prompt // tm
    reps = CHUNK // dec_seq
    ws = jnp.stack([w_s, jnp.tile(w_s[:, :dec_seq, :dec_seq], (1, reps, reps))])
    gd = a_inner // A_GROUPS
    sb_p = jnp.repeat(b_s.T, gd, axis=1)
    sb_s = jnp.repeat(jnp.tile(b_s[:, :dec_seq].T, (reps, 1)), gd, axis=1)
    sb = jnp.stack([sb_p, sb_s])
    const2 = lambda i: (0, 0)
    const3 = lambda i: (0, 0, 0)
    const4 = lambda i: (0, 0, 0, 0)
    kern = functools.partial(_gmlp_kernel, n_prompt_tiles=n_prompt_tiles, dec_seq=dec_seq)
    return pl.pallas_call(
        kern,
        grid=(n // tm,),
        in_specs=[
            pl.BlockSpec((tm, d), prompt_index),
            pl.BlockSpec((tm, d), sample_index),
            pl.BlockSpec((d, 2 * a_inner), const2),
            pl.BlockSpec((1, 2 * a_inner), const2),
            pl.BlockSpec((1, a_inner), const2),
            pl.BlockSpec((1, a_inner), const2),
            pl.BlockSpec((2, A_GROUPS, CHUNK, CHUNK), const4),
            pl.BlockSpec((2, CHUNK, a_inner), const3),
            pl.BlockSpec((a_inner, d), const2),
            pl.BlockSpec((1, d), const2),
            pl.BlockSpec((1, d), const2),
        ],
        out_specs=[
            pl.BlockSpec((tm, d), lambda i: (i, 0)),
            pl.BlockSpec((tm, a_inner), _hi_index(n_prompt_tiles)),
        ],
        out_shape=[jax.ShapeDtypeStruct((n, d), F32), jax.ShapeDtypeStruct((n_sample, a_inner), F32)],
        compiler_params=_params(("arbitrary",)),
        name="gmlp_layer",
    )(x_prompt, x_sample, w_in.astype(BF16), b_in[None], lnv_g[None], lnv_b[None], ws, sb,
      w_out.astype(BF16), ln_g[None], ln_b[None])


def _route(x_bf, rwt_ref, rbias_ref):
    logits = _dot_nt(rwt_ref[...], x_bf)
    mx = jnp.max(logits, axis=0, keepdims=True)
    ex = jnp.exp(logits - mx)
    probs = ex / jnp.sum(ex, axis=0, keepdims=True)
    sel = probs + rbias_ref[...]
    p = [probs[e:e + 1, :] for e in range(N_EXPERTS)]
    s = [sel[e:e + 1, :] for e in range(N_EXPERTS)]
    gscore = []
    for g in range(N_EXPERT_GROUPS):
        a, b, c, d = s[4 * g:4 * g + 4]
        hi1, lo1 = jnp.maximum(a, b), jnp.minimum(a, b)
        hi2, lo2 = jnp.maximum(c, d), jnp.minimum(c, d)
        top1 = jnp.maximum(hi1, hi2)
        top2 = jnp.maximum(jnp.minimum(hi1, hi2), jnp.maximum(lo1, lo2))
        gscore.append(top1 + top2)
    best = gscore[0]
    gbest = jnp.zeros_like(best, dtype=jnp.int32)
    for g in range(1, N_EXPERT_GROUPS):
        better = gscore[g] > best
        best = jnp.where(better, gscore[g], best)
        gbest = jnp.where(better, g, gbest)
    gates = []
    for e in range(N_EXPERTS):
        g = e // EXPERTS_PER_GROUP
        rank = jnp.zeros_like(gbest)
        for f in range(EXPERTS_PER_GROUP * g, EXPERTS_PER_GROUP * (g + 1)):
            if f == e:
                continue
            beats = (s[f] >= s[e]) if f < e else (s[f] > s[e])
            rank = rank + beats.astype(jnp.int32)
        chosen = (gbest == g) & (rank < 2)
        gates.append(jnp.where(chosen, p[e], 0.0))
    denom = gates[0]
    for e in range(1, N_EXPERTS):
        denom = denom + gates[e]
    return [gt / denom for gt in gates]


MOE_ALIGN = 16
MOE_ROW_TILE = 512


def _moe_local_rows(tm):
    return 2 * tm + N_EXPERTS * MOE_ALIGN


def _route_kernel(x_ref, rwt_ref, rbias_ref, gt_ref, cnt_ref):
    gates = jnp.concatenate(_route(x_ref[...].astype(BF16), rwt_ref, rbias_ref), axis=0)
    gt_ref[...] = gates
    cnt = jnp.sum((gates > 0.0).astype(F32), axis=1, keepdims=True)
    cnt_ref[...] = jnp.broadcast_to(cnt, cnt_ref.shape).astype(jnp.int32)


def _smem_column(ref, base, n):
    row = lax.broadcasted_iota(jnp.int32, (n, 1), 0)
    col = jnp.zeros((n, 1), jnp.int32)
    for e in range(n):
        col = jnp.where(row == e, ref[base + e], col)
    return col.astype(F32)


def _local_positions(gates, seg_start):
    n_e, t = gates.shape
    chosen = gates > 0.0
    ch_bf = chosen.astype(BF16)
    ta = lax.broadcasted_iota(jnp.int32, (t, t), 0)
    tb = lax.broadcasted_iota(jnp.int32, (t, t), 1)
    rank = _dot(ch_bf, (ta < tb).astype(BF16))
    ch = chosen.astype(F32)
    seen = jnp.zeros((1, t), F32)
    firsts = []
    for e in range(n_e):
        row = ch[e:e + 1, :]
        firsts.append(jnp.where(seen == 0.0, row, 0.0))
        seen = seen + row
    first = jnp.concatenate(firsts, axis=0)
    second = ch - first
    pos = seg_start + rank

    def pick(w):
        has = jnp.sum(w, axis=0, keepdims=True) > 0.0
        return (jnp.where(has, jnp.sum(w * pos, axis=0, keepdims=True), -1.0),
                jnp.sum(w * gates, axis=0, keepdims=True))

    pos1, g1 = pick(first)
    pos2, g2 = pick(second)
    return pos1, pos2, g1, g2


def _segment_copies(lstart_ref, far_ref, ngrp_ref, base, make_copy):
    total = jnp.int32(0)
    for e in range(N_EXPERTS):
        n = ngrp_ref[base + e]
        near = lstart_ref[base + e]
        far = far_ref[base + e]

        def body(j, carry, near=near, far=far):
            make_copy(pl.multiple_of(near + j * MOE_ALIGN, MOE_ALIGN),
                      pl.multiple_of(far + j * MOE_ALIGN, MOE_ALIGN)).start()
            return carry

        lax.fori_loop(0, n, body, 0)
        total = total + n
    return total


def _wait_copies(count, make_copy):
    def body(j, carry):
        make_copy(0, 0).wait()
        return carry

    lax.fori_loop(0, count, body, 0)


def _dispatch_kernel(lstart_ref, dst_ref, ngrp_ref, tail_ref, tailn_ref, nused_ref, x_ref, gt_ref, xs_ref,
                     loc_ref, zero_ref, pending_ref, sems):
    t = pl.program_id(0)
    slot = t % 2
    base = t * N_EXPERTS
    tm = x_ref.shape[0]
    rl = loc_ref.shape[1]
    pos1, pos2, _, _ = _local_positions(gt_ref[...], _smem_column(lstart_ref, base, N_EXPERTS))
    r = lax.broadcasted_iota(jnp.int32, (rl, tm), 0)
    perm = jnp.logical_or(r == pos1.astype(jnp.int32), r == pos2.astype(jnp.int32)).astype(BF16)
    loc_ref[slot] = _dot(perm, x_ref[...].astype(BF16)).astype(BF16)

    def to_sorted(buf):
        def make(src_row, dst_row):
            return pltpu.make_async_copy(loc_ref.at[buf, pl.ds(src_row, MOE_ALIGN), :],
                                         xs_ref.at[pl.ds(dst_row, MOE_ALIGN), :], sems.at[buf])
        return make

    @pl.when(t > 0)
    def _():
        _wait_copies(pending_ref[0], to_sorted(1 - slot))

    n_copies = _segment_copies(lstart_ref, dst_ref, ngrp_ref, base, to_sorted(slot))
    pending_ref[0] = n_copies
    sem = sems.at[slot]

    @pl.when(t == pl.num_programs(0) - 1)
    def _():
        _wait_copies(n_copies, to_sorted(slot))
        zero_ref[...] = jnp.zeros_like(zero_ref)

        def zero_fill(src_row, dst_row):
            return pltpu.make_async_copy(zero_ref.at[pl.ds(0, MOE_ALIGN), :],
                                         xs_ref.at[pl.ds(dst_row, MOE_ALIGN), :], sem)

        total = jnp.int32(0)
        for e in range(N_EXPERTS):
            n = tailn_ref[e]
            start = tail_ref[e]

            def body(j, carry, start=start):
                zero_fill(0, pl.multiple_of(start + j * MOE_ALIGN, MOE_ALIGN)).start()
                return carry

            lax.fori_loop(0, n, body, 0)
            total = total + n
        _wait_copies(total, zero_fill)

        rt = zero_ref.shape[0]

        def zero_tile(src_row, dst_row):
            return pltpu.make_async_copy(zero_ref, xs_ref.at[pl.ds(dst_row, rt), :], sem)

        n_unused = xs_ref.shape[0] // rt - nused_ref[0]

        def tile_body(j, carry):
            zero_tile(0, pl.multiple_of((nused_ref[0] + j) * rt, rt)).start()
            return carry

        lax.fori_loop(0, n_unused, tile_body, 0)
        _wait_copies(n_unused, zero_tile)


def _experts_kernel(te_ref, nused_ref, xs_ref, wup_ref, wdn_ref, ys_ref):
    d_exp = wdn_ref.shape[1]
    used = pl.program_id(0) < nused_ref[0]

    @pl.when(used)
    def _():
        h = _dot(xs_ref[...], wup_ref[0].astype(BF16))
        act = jax.nn.silu(h[:, :d_exp]) * h[:, d_exp:]
        ys_ref[...] = _dot(act.astype(BF16), wdn_ref[0].astype(BF16)).astype(ys_ref.dtype)

    @pl.when(jnp.logical_not(used))
    def _():
        ys_ref[...] = jnp.zeros_like(ys_ref)


def _combine_kernel(lstart_ref, dst_ref, ngrp_ref, x_ref, gt_ref, ys_ref, g_ref, b_ref, *rest,
                    n_prompt_tiles, split):
    if split:
        op_ref, os_ref, loc_ref, pending_ref, sems = rest
    else:
        o_ref, loc_ref, pending_ref, sems = rest
    t = pl.program_id(0)
    slot = t % 2
    base = t * N_EXPERTS
    tm = x_ref.shape[0]
    rl = loc_ref.shape[1]

    def from_sorted(buf):
        def make(dst_row, src_row):
            return pltpu.make_async_copy(ys_ref.at[pl.ds(src_row, MOE_ALIGN), :],
                                         loc_ref.at[buf, pl.ds(dst_row, MOE_ALIGN), :], sems.at[buf])
        return make

    @pl.when(t == 0)
    def _():
        loc_ref[...] = jnp.zeros_like(loc_ref)
        pending_ref[0] = _segment_copies(lstart_ref, dst_ref, ngrp_ref, base, from_sorted(0))

    @pl.when(t + 1 < pl.num_programs(0))
    def _():
        pending_ref[1 - slot] = _segment_copies(lstart_ref, dst_ref, ngrp_ref, base + N_EXPERTS,
                                                from_sorted(1 - slot))

    pos1, pos2, g1, g2 = _local_positions(gt_ref[...], _smem_column(lstart_ref, base, N_EXPERTS))
    rows = jnp.concatenate([pos1, pos2, g1, g2, jnp.zeros((LANES - 4, tm), F32)], axis=0)
    cols = rows.T
    lane = lax.broadcasted_iota(jnp.int32, (tm, rl), 1)
    sel = (jnp.where(lane == cols[:, 0:1].astype(jnp.int32), cols[:, 2:3], 0.0)
           + jnp.where(lane == cols[:, 1:2].astype(jnp.int32), cols[:, 3:4], 0.0)).astype(BF16)
    _wait_copies(pending_ref[slot], from_sorted(slot))
    x = x_ref[...]
    y = _layer_norm(ALPHA * x + _dot(sel, loc_ref[slot]), g_ref[...], b_ref[...])
    if split:
        @pl.when(t < n_prompt_tiles)
        def _():
            op_ref[...] = y

        @pl.when(t >= n_prompt_tiles)
        def _():
            os_ref[...] = y
    else:
        o_ref[...] = y


def _moe_layer(x, rwt, rbias, w_up, w_down, ln_g, ln_b, *, n_prompt, split):
    n, d = x.shape
    tm = TOKEN_TILE
    n_tiles = n // tm
    d_exp = w_down.shape[1]
    n_prompt_tiles = n_prompt // tm
    n_e, al, rt = N_EXPERTS, MOE_ALIGN, MOE_ROW_TILE
    rl = _moe_local_rows(tm)
    n_row_tiles = -(-(2 * n + n_tiles * n_e * (al - 1) + n_e * (rt - al)) // rt)

    gates_t, cnt_raw = pl.pallas_call(
        _route_kernel,
        grid=(n_tiles,),
        in_specs=[pl.BlockSpec((tm, d), lambda i: (i, 0)),
                  pl.BlockSpec((n_e, d), lambda i: (0, 0)),
                  pl.BlockSpec((n_e, 1), lambda i: (0, 0))],
        out_specs=[pl.BlockSpec((n_e, tm), lambda i: (0, i)), pl.BlockSpec((n_e, LANES), lambda i: (i, 0))],
        out_shape=[jax.ShapeDtypeStruct((n_e, n), F32), jax.ShapeDtypeStruct((n_tiles * n_e, LANES), jnp.int32)],
        compiler_params=_params(("parallel",)),
        name="moe_route",
    )(x, rwt, rbias)

    cnt = cnt_raw[:, 0].reshape(n_tiles, n_e)
    pad = (cnt + (al - 1)) // al * al
    lstart = jnp.cumsum(pad, axis=1) - pad
    tot = jnp.sum(pad, axis=0)
    cap = (tot + (rt - 1)) // rt * rt
    cend = jnp.cumsum(cap)
    dst = (cend - cap)[None, :] + jnp.cumsum(pad, axis=0) - pad
    n_used = (cend[-1] // rt).astype(jnp.int32)
    tile_id = jnp.arange(n_row_tiles, dtype=jnp.int32)
    tile_expert = jnp.sum((tile_id[:, None] * rt >= cend[None, :]).astype(jnp.int32), axis=1)
    tile_expert = jnp.take(tile_expert, jnp.minimum(tile_id, n_used - 1))
    i32 = lambda a: a.reshape(-1).astype(jnp.int32)
    lstart, dst, ngrp = i32(lstart), i32(dst), i32(pad // al)
    tail, tailn = i32(cend - cap + tot), i32((cap - tot) // al)

    xs = pl.pallas_call(
        _dispatch_kernel,
        grid_spec=pltpu.PrefetchScalarGridSpec(
            num_scalar_prefetch=6,
            grid=(n_tiles,),
            in_specs=[pl.BlockSpec((tm, d), lambda i, *_: (i, 0)),
                      pl.BlockSpec((n_e, tm), lambda i, *_: (0, i))],
            out_specs=pl.BlockSpec(memory_space=pl.ANY),
            scratch_shapes=[pltpu.VMEM((2, rl, d), BF16), pltpu.VMEM((rt, d), BF16),
                            pltpu.SMEM((1,), jnp.int32), pltpu.SemaphoreType.DMA((2,))],
        ),
        out_shape=jax.ShapeDtypeStruct((n_row_tiles * rt, d), BF16),
        compiler_params=_params(("arbitrary",)),
        name="moe_dispatch",
    )(lstart, dst, ngrp, tail, tailn, n_used.reshape(1), x, gates_t)

    last_used = lambda i, te, nu: (jnp.minimum(i, nu[0] - 1), 0)
    ys = pl.pallas_call(
        _experts_kernel,
        grid_spec=pltpu.PrefetchScalarGridSpec(
            num_scalar_prefetch=2,
            grid=(n_row_tiles,),
            in_specs=[pl.BlockSpec((rt, d), last_used),
                      pl.BlockSpec((1, d, 2 * d_exp), lambda i, te, nu: (te[i], 0, 0)),
                      pl.BlockSpec((1, d_exp, d), lambda i, te, nu: (te[i], 0, 0))],
            out_specs=pl.BlockSpec((rt, d), lambda i, te, nu: (i, 0)),
        ),
        out_shape=jax.ShapeDtypeStruct((n_row_tiles * rt, d), BF16),
        compiler_params=_params(("arbitrary",)),
        name="moe_experts",
    )(tile_expert, n_used.reshape(1), xs, w_up, w_down)

    if split:
        lo, hi = _lo_index(n_prompt_tiles), _hi_index(n_prompt_tiles)
        out_specs = [pl.BlockSpec((tm, d), lambda i, *_: lo(i)), pl.BlockSpec((tm, d), lambda i, *_: hi(i))]
        out_shape = [jax.ShapeDtypeStruct((n_prompt, d), F32), jax.ShapeDtypeStruct((n - n_prompt, d), F32)]
    else:
        out_specs = pl.BlockSpec((tm, d), lambda i, *_: (i, 0))
        out_shape = jax.ShapeDtypeStruct((n, d), F32)
    return pl.pallas_call(
        functools.partial(_combine_kernel, n_prompt_tiles=n_prompt_tiles, split=split),
        grid_spec=pltpu.PrefetchScalarGridSpec(
            num_scalar_prefetch=3,
            grid=(n_tiles,),
            in_specs=[pl.BlockSpec((tm, d), lambda i, *_: (i, 0)),
                      pl.BlockSpec((n_e, tm), lambda i, *_: (0, i)),
                      pl.BlockSpec(memory_space=pl.ANY),
                      pl.BlockSpec((1, d), lambda i, *_: (0, 0)),
                      pl.BlockSpec((1, d), lambda i, *_: (0, 0))],
            out_specs=out_specs,
            scratch_shapes=[pltpu.VMEM((2, rl, d), BF16), pltpu.SMEM((2,), jnp.int32),
                            pltpu.SemaphoreType.DMA((2,))],
        ),
        out_shape=out_shape,
        compiler_params=_params(("arbitrary",)),
        name="moe_combine",
    )(lstart, dst, ngrp, x, gates_t, ys, ln_g[None], ln_b[None])


def _kvq_kernel(x_ref, wk_ref, wv_ref, wq_ref, kp_ref, vp_ref, ks_ref, vs_ref, kb_ref, vb_ref, qb_ref,
                *, n_prompt_tiles):
    i = pl.program_id(0)
    xb = x_ref[...].astype(BF16)
    k = _dot(xb, wk_ref[...])
    v = _dot(xb, wv_ref[...])
    tm = k.shape[0]
    vd = k.shape[1] // N_HEADS

    def store_heads(ref, val):
        for h in range(N_HEADS):
            ref[pl.ds(h, tm, stride=N_HEADS), :] = val[:, h * vd:(h + 1) * vd]

    @pl.when(i < n_prompt_tiles)
    def _():
        store_heads(kp_ref, k)
        store_heads(vp_ref, v)
        kb_ref[...] = k.astype(BF16)
        vb_ref[...] = v.astype(BF16)
        qb_ref[...] = (_dot(xb, wq_ref[...]) * Q_SCALE).astype(BF16)

    @pl.when(i >= n_prompt_tiles)
    def _():
        store_heads(ks_ref, k)
        store_heads(vs_ref, v)


def _kvq_proj(x, w_k, w_v, w_q, *, n_prompt):
    n, d = x.shape
    tm = TOKEN_TILE
    n_prompt_tiles = n_prompt // tm
    n_sample = n - n_prompt
    vd = d // N_HEADS
    lo = pl.BlockSpec((tm, d), _lo_index(n_prompt_tiles))
    lo_heads = pl.BlockSpec((tm * N_HEADS, vd), _lo_index(n_prompt_tiles))
    hi_heads = pl.BlockSpec((tm * N_HEADS, vd), _hi_index(n_prompt_tiles))
    wspec = pl.BlockSpec((d, d), lambda i: (0, 0))
    return pl.pallas_call(
        functools.partial(_kvq_kernel, n_prompt_tiles=n_prompt_tiles),
        grid=(n // tm,),
        in_specs=[pl.BlockSpec((tm, d), lambda i: (i, 0)), wspec, wspec, wspec],
        out_specs=[lo_heads, lo_heads, hi_heads, hi_heads, lo, lo, lo],
        out_shape=[jax.ShapeDtypeStruct((n_prompt * N_HEADS, vd), F32),
                   jax.ShapeDtypeStruct((n_prompt * N_HEADS, vd), F32),
                   jax.ShapeDtypeStruct((n_sample * N_HEADS, vd), F32),
                   jax.ShapeDtypeStruct((n_sample * N_HEADS, vd), F32),
                   jax.ShapeDtypeStruct((n_prompt, d), BF16), jax.ShapeDtypeStruct((n_prompt, d), BF16),
                   jax.ShapeDtypeStruct((n_prompt, d), BF16)],
        compiler_params=_params(("arbitrary",)),
        name="kvq_proj",
    )(x, w_k.astype(BF16), w_v.astype(BF16), w_q.astype(BF16))


def _q_kernel(x_ref, wq_ref, qb_ref):
    qb_ref[...] = (_dot(x_ref[...].astype(BF16), wq_ref[...]) * Q_SCALE).astype(BF16)


def _q_proj(x, w_q, n_rows):
    d = x.shape[1]
    tm = TOKEN_TILE
    row = pl.BlockSpec((tm, d), lambda i: (i, 0))
    return pl.pallas_call(
        _q_kernel,
        grid=(n_rows // tm,),
        in_specs=[row, pl.BlockSpec((d, d), lambda i: (0, 0))],
        out_specs=row,
        out_shape=jax.ShapeDtypeStruct((n_rows, d), BF16),
        compiler_params=_params(("parallel",)),
        name="q_proj",
    )(x, w_q.astype(BF16))


def _diff_lambda(lam_ref, lam_init):
    lv = lam_ref[...]
    e1 = jnp.exp(jnp.sum(lv[0:1, :] * lv[1:2, :], axis=1, keepdims=True))
    e2 = jnp.exp(jnp.sum(lv[2:3, :] * lv[3:4, :], axis=1, keepdims=True))
    return e1 - e2 + lam_init


def _lane_tile(stat, n):
    return stat if n == LANES else jnp.concatenate([stat] * (n // LANES), axis=1)


def _softmax_step(s, vc, m_ref, l_ref, acc_ref, rows=None):
    sl = slice(None) if rows is None else rows
    m_old = m_ref[sl, :]
    m_new = jnp.maximum(m_old, jnp.max(s, axis=1, keepdims=True))
    alpha = jnp.exp2(m_old - m_new)
    p = jnp.exp2(s - _lane_tile(m_new, s.shape[1]))
    l_ref[sl, :] = alpha * l_ref[sl, :] + jnp.sum(p, axis=1, keepdims=True)
    acc_ref[sl, :] = alpha * acc_ref[sl, :] + _dot(p.astype(BF16), vc)
    m_ref[sl, :] = m_new


def _head_norm(o, g, lam_init):
    return o * lax.rsqrt(jnp.mean(o * o, axis=-1, keepdims=True) + LN_EPS) * g * (1.0 - lam_init)


def _pattn_kernel(lam_ref, g_ref, q_ref, k_ref, v_ref, o_ref,
                  m1_ref, l1_ref, a1_ref, m2_ref, l2_ref, a2_ref, *, lam_init):
    qi = pl.program_id(2)
    tq = q_ref.shape[0]
    tk = ATTN_K_TILE
    per_tile = tq // tk
    hd = HEAD_DIM
    q = q_ref[...]
    q1, q2 = q[:, :hd], q[:, hd:]
    for m_ref, l_ref, a_ref in ((m1_ref, l1_ref, a1_ref), (m2_ref, l2_ref, a2_ref)):
        m_ref[...] = jnp.full_like(m_ref, NEG)
        l_ref[...] = jnp.zeros_like(l_ref)
        a_ref[...] = jnp.zeros_like(a_ref)

    def chunk(start, mask):
        kc = k_ref[pl.ds(start, tk), :]
        vc = v_ref[pl.ds(start, tk), :]
        s1 = _dot_nt(q1, kc[:, :hd])
        s2 = _dot_nt(q2, kc[:, hd:])
        if mask is not None:
            s1 = jnp.where(mask, s1, NEG)
            s2 = jnp.where(mask, s2, NEG)
        _softmax_step(s1, vc, m1_ref, l1_ref, a1_ref)
        _softmax_step(s2, vc, m2_ref, l2_ref, a2_ref)

    n_full = qi * per_tile

    def body(j, carry):
        chunk(pl.multiple_of(2 * j * tk, tk), None)
        chunk(pl.multiple_of((2 * j + 1) * tk, tk), None)
        return carry

    lax.fori_loop(0, n_full // 2, body, 0)

    @pl.when(n_full % 2 == 1)
    def _():
        chunk(pl.multiple_of((n_full - 1) * tk, tk), None)

    row = lax.broadcasted_iota(jnp.int32, (tq, tk), 0)
    col = lax.broadcasted_iota(jnp.int32, (tq, tk), 1)
    for jj in range(per_tile):
        chunk(pl.multiple_of(qi * tq + jj * tk, tk), row >= col + jj * tk)

    lam = _diff_lambda(lam_ref, lam_init)
    o = a1_ref[...] / l1_ref[...] - lam * (a2_ref[...] / l2_ref[...])
    o_ref[...] = _head_norm(o, g_ref[...], lam_init).astype(o_ref.dtype)


def _prompt_attention(qb, kb, vb, lam_rows, subln_g, lam_init, *, batch, seq):
    d = qb.shape[1]
    vd = d // N_HEADS
    tq = ATTN_Q_TILE
    nq = seq // tq
    kern = functools.partial(_pattn_kernel, lam_init=lam_init)
    stat = pltpu.VMEM((tq, LANES), F32)
    return pl.pallas_call(
        kern,
        grid=(batch, N_HEADS, nq),
        in_specs=[
            pl.BlockSpec((4, HEAD_DIM), lambda b, h, i: (0, 0)),
            pl.BlockSpec((1, vd), lambda b, h, i: (0, 0)),
            pl.BlockSpec((tq, vd), lambda b, h, i: (b * nq + i, h)),
            pl.BlockSpec((seq, vd), lambda b, h, i: (b, h)),
            pl.BlockSpec((seq, vd), lambda b, h, i: (b, h)),
        ],
        out_specs=pl.BlockSpec((tq, vd), lambda b, h, i: (b * nq + i, h)),
        out_shape=jax.ShapeDtypeStruct((batch * seq, d), BF16),
        scratch_shapes=[stat, stat, pltpu.VMEM((tq, vd), F32), stat, stat, pltpu.VMEM((tq, vd), F32)],
        compiler_params=_params(("parallel", "parallel", "arbitrary")),
        name="prompt_attention",
    )(lam_rows, subln_g[None], qb, kb, vb)


def _sattn_kernel(pt_ref, lam_ref, g_ref, x_ref, wq_ref, kn_ref, vn_ref, *rest, lam_init, dec_seq):
    npg = SAMPLE_SEQS_PER_STEP * SAMPLE_PAGES_PER_STEP
    k_refs = rest[:npg]
    v_refs = rest[npg:2 * npg]
    o_ref = rest[2 * npg]
    qbd_ref, m_ref, l_ref, acc_ref = rest[2 * npg + 1:]
    c = pl.program_id(1)
    n_rows = SAMPLE_SEQS_PER_STEP * dec_seq
    vd = qbd_ref.shape[1]
    hrows = 2 * n_rows

    @pl.when(c == 0)
    def _():
        q = _dot(x_ref[...].astype(BF16), wq_ref[...]) * Q_SCALE
        lane = lax.broadcasted_iota(jnp.int32, (n_rows, vd), 1)
        parts = []
        for h in range(N_HEADS):
            qh = q[:, h * vd:(h + 1) * vd]
            parts += [jnp.where(lane < HEAD_DIM, qh, 0.0), jnp.where(lane >= HEAD_DIM, qh, 0.0)]
        qbd_ref[...] = jnp.concatenate(parts, axis=0).astype(BF16)
        m_ref[...] = jnp.full_like(m_ref, NEG)
        l_ref[...] = jnp.zeros_like(l_ref)
        acc_ref[...] = jnp.zeros_like(acc_ref)

    keys_per_seq = SAMPLE_PAGES_PER_STEP * PAGE_SIZE
    n_keys = SAMPLE_SEQS_PER_STEP * keys_per_seq
    n_cols = N_HEADS * hrows
    rr = lax.broadcasted_iota(jnp.int32, (n_cols, n_keys), 0)
    kk = lax.broadcasted_iota(jnp.int32, (n_cols, n_keys), 1)
    same_seq = (rr % n_rows) // dec_seq == kk // keys_per_seq

    def head_rows(refs, h):
        head = pl.ds(h, PAGE_SIZE, stride=N_HEADS)
        return jnp.concatenate([r.at[0][head, :] for r in refs], axis=0).astype(BF16)

    s = jnp.concatenate([_dot_nt(qbd_ref[h * hrows:(h + 1) * hrows, :], head_rows(k_refs, h))
                         for h in range(N_HEADS)], axis=0)
    s = jnp.where(same_seq, s, NEG)
    m_old = m_ref[...]
    m_new = jnp.maximum(m_old, jnp.max(s, axis=1, keepdims=True))
    alpha = jnp.exp2(m_old - m_new)
    p = jnp.exp2(s - _lane_tile(m_new, n_keys))
    l_ref[...] = alpha * l_ref[...] + jnp.sum(p, axis=1, keepdims=True)
    pb = p.astype(BF16)
    pv = jnp.concatenate([_dot(pb[h * hrows:(h + 1) * hrows, :], head_rows(v_refs, h))
                          for h in range(N_HEADS)], axis=0)
    acc_ref[...] = alpha * acc_ref[...] + pv
    m_ref[...] = m_new

    @pl.when(c == pl.num_programs(1) - 1)
    def _():
        lam = _diff_lambda(lam_ref, lam_init)
        r2 = lax.broadcasted_iota(jnp.int32, (hrows, LANES), 0) % n_rows
        k2 = lax.broadcasted_iota(jnp.int32, (hrows, LANES), 1)
        ok = (k2 < n_rows) & (k2 // dec_seq == r2 // dec_seq) & (k2 % dec_seq <= r2 % dec_seq)
        pad = jnp.zeros((LANES - n_rows, vd), F32)
        outs = []
        for h in range(N_HEADS):
            new_head = pl.ds(h, n_rows, stride=N_HEADS)
            kn = jnp.concatenate([kn_ref[new_head, :], pad], axis=0).astype(BF16)
            vn = jnp.concatenate([vn_ref[new_head, :], pad], axis=0).astype(BF16)
            rows = slice(h * hrows, (h + 1) * hrows)
            sn = jnp.where(ok, _dot_nt(qbd_ref[rows, :], kn), NEG)
            _softmax_step(sn, vn, m_ref, l_ref, acc_ref, rows)
            o = acc_ref[rows, :] / l_ref[rows, :]
            outs.append(_head_norm(o[:n_rows] - lam * o[n_rows:], g_ref[...], lam_init))
        o_ref[...] = jnp.concatenate(outs, axis=1)


def _sample_attention(x, w_q, k_new, v_new, cache_k, cache_v, page_table, lam_rows, subln_g, lam_init,
                      *, n_prompt, dec_seq):
    n, d = x.shape
    vd = d // N_HEADS
    n_seq, n_pg = page_table.shape
    n_rows = SAMPLE_SEQS_PER_STEP * dec_seq
    n_cols = 2 * N_HEADS * n_rows
    pps = SAMPLE_PAGES_PER_STEP
    slab = PAGE_SIZE * N_HEADS
    ck = cache_k.reshape(cache_k.shape[0], slab, vd)
    cv = cache_v.reshape(cache_v.shape[0], slab, vd)
    pt = page_table.reshape(-1)
    base = n_prompt // n_rows

    def page_spec(s, j):
        return pl.BlockSpec((1, slab, vd),
                            lambda b, c, pt: (pt[(b * SAMPLE_SEQS_PER_STEP + s) * n_pg + c * pps + j], 0, 0))

    page_specs = [page_spec(s, j) for s in range(SAMPLE_SEQS_PER_STEP) for j in range(pps)]
    new_rows = pl.BlockSpec((n_rows * N_HEADS, vd), lambda b, c, pt: (b, 0))
    kern = functools.partial(_sattn_kernel, lam_init=lam_init, dec_seq=dec_seq)
    stat = pltpu.VMEM((n_cols, LANES), F32)
    grid_spec = pltpu.PrefetchScalarGridSpec(
        num_scalar_prefetch=1,
        grid=(n_seq // SAMPLE_SEQS_PER_STEP, n_pg // pps),
        in_specs=[
            pl.BlockSpec((4, HEAD_DIM), lambda b, c, pt: (0, 0)),
            pl.BlockSpec((1, vd), lambda b, c, pt: (0, 0)),
            pl.BlockSpec((n_rows, d), lambda b, c, pt: (base + b, 0)),
            pl.BlockSpec((d, d), lambda b, c, pt: (0, 0)),
            new_rows,
            new_rows,
        ] + page_specs + page_specs,
        out_specs=pl.BlockSpec((n_rows, d), lambda b, c, pt: (b, 0)),
        scratch_shapes=[pltpu.VMEM((n_cols, vd), BF16), stat, stat, pltpu.VMEM((n_cols, vd), F32)],
    )
    n_pages_step = SAMPLE_SEQS_PER_STEP * pps
    return pl.pallas_call(
        kern,
        grid_spec=grid_spec,
        out_shape=jax.ShapeDtypeStruct((n - n_prompt, d), F32),
        compiler_params=_params(("parallel", "arbitrary")),
        name="sample_attention",
    )(pt, lam_rows, subln_g[None], x, w_q.astype(BF16), k_new, v_new,
      *([ck] * n_pages_step), *([cv] * n_pages_step))


def _oproj_kernel(ap_ref, as_ref, w_ref, x_ref, g_ref, b_ref, o_ref, *, n_prompt_tiles):
    i = pl.program_id(0)
    a = jnp.where(i >= n_prompt_tiles, as_ref[...].astype(BF16), ap_ref[...])
    m = _dot(a, w_ref[...])
    o_ref[...] = _layer_norm(ALPHA * x_ref[...] + m, g_ref[...], b_ref[...])


def _oproj_layer(a_prompt, a_sample, w_o, x, ln_g, ln_b):
    n, d = x.shape
    tm = TOKEN_TILE
    n_prompt_tiles = a_prompt.shape[0] // tm
    row = pl.BlockSpec((tm, d), lambda i: (i, 0))
    const2 = lambda i: (0, 0)
    kern = functools.partial(_oproj_kernel, n_prompt_tiles=n_prompt_tiles)
    return pl.pallas_call(
        kern,
        grid=(n // tm,),
        in_specs=[
            pl.BlockSpec((tm, d), _lo_index(n_prompt_tiles)),
            pl.BlockSpec((tm, d), _hi_index(n_prompt_tiles)),
            pl.BlockSpec((d, d), const2),
            row,
            pl.BlockSpec((1, d), const2),
            pl.BlockSpec((1, d), const2),
        ],
        out_specs=row,
        out_shape=jax.ShapeDtypeStruct((n, d), F32),
        compiler_params=_params(("parallel",)),
        name="attn_out_proj",
    )(a_prompt, a_sample, w_o.astype(BF16), x, ln_g[None], ln_b[None])


def kernel(x_prompt, x_sample, cache_k, cache_v, page_table, ln1_g, ln1_b, ln2_g, ln2_b, a_w_in, a_b_in, a_lnv_g, a_lnv_b, a_w_s, a_b_s, a_w_out, w_k, w_v, b_w_q, b_lam_q1, b_lam_k1, b_lam_q2, b_lam_k2, b_subln_g, b_w_o, router_w, router_bias, moe_w_up, moe_w_down):
    batch, seq, d = x_prompt.shape
    dec_batch, dec_seq, _ = x_sample.shape
    n_prompt = batch * seq
    n_sample = dec_batch * dec_seq
    tm = TOKEN_TILE
    assert seq % CHUNK == 0 and CHUNK % dec_seq == 0 and dec_seq <= CHUNK
    assert n_prompt % tm == 0 and n_sample % tm == 0
    assert seq % ATTN_Q_TILE == 0 and ATTN_Q_TILE % ATTN_K_TILE == 0
    assert a_w_s.shape[-1] == CHUNK and cache_k.shape[1] == PAGE_SIZE and cache_k.shape[2] == N_HEADS
    assert dec_batch % SAMPLE_SEQS_PER_STEP == 0 and page_table.shape[1] % SAMPLE_PAGES_PER_STEP == 0
    assert SAMPLE_SEQS_PER_STEP * dec_seq == SUBLANES

    n_prompt_tiles = n_prompt // tm
    rwt = router_w.T.astype(BF16)
    rbias = router_bias.astype(F32)[:, None]
    chunk_v = []
    x = None
    for l in range(DEPTH):
        if l < N_A_LAYERS:
            if l == 0:
                xa, xb = x_prompt.reshape(n_prompt, d), x_sample.reshape(n_sample, d)
                ia, ib = _lo_index(n_prompt_tiles), _hi_index(n_prompt_tiles)
            else:
                xa, xb = x, x
                ia, ib = _lo_index(n_prompt_tiles), (lambda i: (jnp.maximum(i, n_prompt_tiles), 0))
            x, v_rows = _gmlp_layer(xa, xb, ia, ib, a_w_in[l], a_b_in[l], a_lnv_g[l], a_lnv_b[l], a_w_s[l],
                                    a_b_s[l], a_w_out[l], ln1_g[l], ln1_b[l],
                                    n_prompt=n_prompt, n_sample=n_sample, dec_seq=dec_seq)
            chunk_v.append(v_rows.reshape(dec_batch, dec_seq, -1))
        else:
            j = l - N_A_LAYERS
            lam_init = 0.8 - 0.6 * math.exp(-0.3 * l)
            if j == 0:
                kp, vp, ks, vs, kb, vb, qb = _kvq_proj(x, w_k, w_v, b_w_q[j], n_prompt=n_prompt)
            else:
                qb = _q_proj(x, b_w_q[j], n_prompt)
            lam_rows = jnp.stack([b_lam_q1[j], b_lam_k1[j], b_lam_q2[j], b_lam_k2[j]]).astype(F32)
            a_p = _prompt_attention(qb, kb, vb, lam_rows, b_subln_g[j], lam_init, batch=batch, seq=seq)
            a_s = _sample_attention(x, b_w_q[j], ks, vs, cache_k, cache_v, page_table, lam_rows,
                                    b_subln_g[j], lam_init, n_prompt=n_prompt, dec_seq=dec_seq)
            x = _oproj_layer(a_p, a_s, b_w_o[j], x, ln1_g[l], ln1_b[l])
        x = _moe_layer(x, rwt, rbias, moe_w_up[l], moe_w_down[l], ln2_g[l], ln2_b[l],
                       n_prompt=n_prompt, split=(l == DEPTH - 1))

    yp, ys = x
    vd = d // N_HEADS
    return (yp.reshape(batch, seq, d), ys.reshape(dec_batch, dec_seq, d),
            kp.reshape(batch, seq, N_HEADS, vd), vp.reshape(batch, seq, N_HEADS, vd),
            ks.reshape(dec_batch, dec_seq, N_HEADS, vd), vs.reshape(dec_batch, dec_seq, N_HEADS, vd),
            jnp.stack(chunk_v))
```

```python
import functools
import math

import jax
import jax.numpy as jnp
from jax import lax
from jax.experimental import pallas as pl
from jax.experimental.pallas import tpu as pltpu

F32 = jnp.float32
BF16 = jnp.bfloat16

DEPTH = 4
N_A_LAYERS = DEPTH // 2
CHUNK = 128
A_GROUPS = 8
N_HEADS = 8
HEAD_DIM = 64
N_EXPERTS = 16
N_EXPERT_GROUPS = 4
EXPERTS_PER_GROUP = N_EXPERTS // N_EXPERT_GROUPS
ALPHA = (2.0 * DEPTH) ** 0.25
LN_EPS = 1e-5
NEG = -1e30
PAGE_SIZE = 128
Q_SCALE = HEAD_DIM ** -0.5 * math.log2(math.e)

LANES = 128
SUBLANES = 8
TOKEN_TILE = 512
ATTN_Q_TILE = 1024
ATTN_K_TILE = 512
SAMPLE_SEQS_PER_STEP = 2
SAMPLE_PAGES_PER_STEP = 4
VMEM_LIMIT = 56 * 1024 * 1024


def _params(semantics):
    return pltpu.CompilerParams(dimension_semantics=semantics, vmem_limit_bytes=VMEM_LIMIT)


def _layer_norm(x, g, b):
    mu = jnp.mean(x, axis=-1, keepdims=True)
    xc = x - mu
    var = jnp.mean(xc * xc, axis=-1, keepdims=True)
    return xc * lax.rsqrt(var + LN_EPS) * g + b


def _dot(a, b):
    return jnp.dot(a, b, preferred_element_type=F32)


def _dot_nt(a, b):
    return lax.dot_general(a, b, (((1,), (1,)), ((), ())), preferred_element_type=F32)


def _lo_index(n_prompt_tiles):
    return lambda i: (jnp.minimum(i, n_prompt_tiles - 1), 0)


def _hi_index(n_prompt_tiles):
    return lambda i: (jnp.maximum(i - n_prompt_tiles, 0), 0)


def _gmlp_kernel(xp_ref, xs_ref, win_ref, bin_ref, lvg_ref, lvb_ref, ws_ref, sb_ref,
                 wout_ref, g1_ref, b1_ref, xo_ref, v_ref, *, n_prompt_tiles, dec_seq):
    i = pl.program_id(0)
    is_sample = i >= n_prompt_tiles
    a_inner = wout_ref.shape[0]
    gd = a_inner // A_GROUPS
    tm = xp_ref.shape[0]

    x = jnp.where(is_sample, xs_ref[...], xp_ref[...])
    z = _dot(x.astype(BF16), win_ref[...]) + bin_ref[...]
    z = jax.nn.gelu(z)
    u = z[:, :a_inner]
    v = _layer_norm(z[:, a_inner:], lvg_ref[...], lvb_ref[...])

    @pl.when(is_sample)
    def _():
        v_ref[...] = v

    row = lax.broadcasted_iota(jnp.int32, (CHUNK, CHUNK), 0)
    col = lax.broadcasted_iota(jnp.int32, (CHUNK, CHUNK), 1)
    mask = jnp.logical_and(col <= row, jnp.logical_or(jnp.logical_not(is_sample),
                                                      (row // dec_seq) == (col // dec_seq)))
    vb = v.astype(BF16)
    kind = is_sample.astype(jnp.int32)
    bias = sb_ref[kind]
    s_rows = []
    mixes = [jnp.where(mask, ws_ref[kind, g], 0.0).astype(BF16) for g in range(A_GROUPS)]
    for c in range(tm // CHUNK):
        cols = []
        for g in range(A_GROUPS):
            blk = vb[c * CHUNK:(c + 1) * CHUNK, g * gd:(g + 1) * gd]
            cols.append(_dot(mixes[g], blk))
        s_rows.append(jnp.concatenate(cols, axis=1) + bias)
    s = jnp.concatenate(s_rows, axis=0)
    m = _dot((u * s).astype(BF16), wout_ref[...])
    xo_ref[...] = _layer_norm(ALPHA * x + m, g1_ref[...], b1_ref[...])


def _gmlp_layer(x_prompt, x_sample, prompt_index, sample_index, w_in, b_in, lnv_g, lnv_b, w_s, b_s, w_out,
                ln_g, ln_b, *, n_prompt, n_sample, dec_seq):
    d = x_prompt.shape[1]
    a_inner = w_out.shape[0]
    tm = TOKEN_TILE
    n = n_prompt + n_sample
    n_prompt_tiles = n_prompt // tm
    reps = CHUNK // dec_seq
    ws = jnp.stack([w_s, jnp.tile(w_s[:, :dec_seq, :dec_seq], (1, reps, reps))])
    gd = a_inner // A_GROUPS
    sb_p = jnp.repeat(b_s.T, gd, axis=1)
    sb_s = jnp.repeat(jnp.tile(b_s[:, :dec_seq].T, (reps, 1)), gd, axis=1)
    sb = jnp.stack([sb_p, sb_s])
    const2 = lambda i: (0, 0)
    const3 = lambda i: (0, 0, 0)
    const4 = lambda i: (0, 0, 0, 0)
    kern = functools.partial(_gmlp_kernel, n_prompt_tiles=n_prompt_tiles, dec_seq=dec_seq)
    return pl.pallas_call(
        kern,
        grid=(n // tm,),
        in_specs=[
            pl.BlockSpec((tm, d), prompt_index),
            pl.BlockSpec((tm, d), sample_index),
            pl.BlockSpec((d, 2 * a_inner), const2),
            pl.BlockSpec((1, 2 * a_inner), const2),
            pl.BlockSpec((1, a_inner), const2),
            pl.BlockSpec((1, a_inner), const2),
            pl.BlockSpec((2, A_GROUPS, CHUNK, CHUNK), const4),
            pl.BlockSpec((2, CHUNK, a_inner), const3),
            pl.BlockSpec((a_inner, d), const2),
            pl.BlockSpec((1, d), const2),
            pl.BlockSpec((1, d), const2),
        ],
        out_specs=[
            pl.BlockSpec((tm, d), lambda i: (i, 0)),
            pl.BlockSpec((tm, a_inner), _hi_index(n_prompt_tiles)),
        ],
        out_shape=[jax.ShapeDtypeStruct((n, d), F32), jax.ShapeDtypeStruct((n_sample, a_inner), F32)],
        compiler_params=_params(("arbitrary",)),
        name="gmlp_layer",
    )(x_prompt, x_sample, w_in.astype(BF16), b_in[None], lnv_g[None], lnv_b[None], ws, sb,
      w_out.astype(BF16), ln_g[None], ln_b[None])


def _route(x_bf, rwt_ref, rbias_ref):
    logits = _dot_nt(rwt_ref[...], x_bf)
    mx = jnp.max(logits, axis=0, keepdims=True)
    ex = jnp.exp(logits - mx)
    probs = ex / jnp.sum(ex, axis=0, keepdims=True)
    sel = probs + rbias_ref[...]
    p = [probs[e:e + 1, :] for e in range(N_EXPERTS)]
    s = [sel[e:e + 1, :] for e in range(N_EXPERTS)]
    gscore = []
    for g in range(N_EXPERT_GROUPS):
        a, b, c, d = s[4 * g:4 * g + 4]
        hi1, lo1 = jnp.maximum(a, b), jnp.minimum(a, b)
        hi2, lo2 = jnp.maximum(c, d), jnp.minimum(c, d)
        top1 = jnp.maximum(hi1, hi2)
        top2 = jnp.maximum(jnp.minimum(hi1, hi2), jnp.maximum(lo1, lo2))
        gscore.append(top1 + top2)
    best = gscore[0]
    gbest = jnp.zeros_like(best, dtype=jnp.int32)
    for g in range(1, N_EXPERT_GROUPS):
        better = gscore[g] > best
        best = jnp.where(better, gscore[g], best)
        gbest = jnp.where(better, g, gbest)
    gates = []
    for e in range(N_EXPERTS):
        g = e // EXPERTS_PER_GROUP
        rank = jnp.zeros_like(gbest)
        for f in range(EXPERTS_PER_GROUP * g, EXPERTS_PER_GROUP * (g + 1)):
            if f == e:
                continue
            beats = (s[f] >= s[e]) if f < e else (s[f] > s[e])
            rank = rank + beats.astype(jnp.int32)
        chosen = (gbest == g) & (rank < 2)
        gates.append(jnp.where(chosen, p[e], 0.0))
    denom = gates[0]
    for e in range(1, N_EXPERTS):
        denom = denom + gates[e]
    return [gt / denom for gt in gates]


MOE_ALIGN = 16
MOE_ROW_TILE = 512


def _moe_local_rows(tm):
    return 2 * tm + N_EXPERTS * MOE_ALIGN


def _route_kernel(x_ref, rwt_ref, rbias_ref, prow_ref, pcol_ref, cnt_ref):
    gates = jnp.concatenate(_route(x_ref[...].astype(BF16), rwt_ref, rbias_ref), axis=0)
    tm = gates.shape[1]
    cnt = jnp.sum((gates > 0.0).astype(F32), axis=1, keepdims=True)
    cnt_i = cnt.astype(jnp.int32)
    cnt_ref[...] = jnp.broadcast_to(cnt_i, cnt_ref.shape)
    shift = MOE_ALIGN.bit_length() - 1
    padded = lax.shift_left(lax.shift_right_logical(cnt_i + (MOE_ALIGN - 1), shift), shift).astype(F32)
    starts, run = [], jnp.zeros((1, 1), F32)
    for e in range(N_EXPERTS):
        starts.append(run)
        run = run + padded[e:e + 1, :]
    pos1, pos2, g1, g2 = _local_positions(gates, jnp.concatenate(starts, axis=0))
    prow_ref[...] = jnp.concatenate([pos1, pos2, g1, g2, jnp.zeros((SUBLANES - 4, tm), F32)], axis=0)
    rows = jnp.concatenate([pos1, pos2, g1, g2, jnp.zeros((LANES - 4, tm), F32)], axis=0)
    pcol_ref[...] = rows.T


def _local_positions(gates, seg_start):
    n_e, t = gates.shape
    chosen = gates > 0.0
    ch_bf = chosen.astype(BF16)
    ta = lax.broadcasted_iota(jnp.int32, (t, t), 0)
    tb = lax.broadcasted_iota(jnp.int32, (t, t), 1)
    rank = _dot(ch_bf, (ta < tb).astype(BF16))
    ch = chosen.astype(F32)
    seen = jnp.zeros((1, t), F32)
    firsts = []
    for e in range(n_e):
        row = ch[e:e + 1, :]
        firsts.append(jnp.where(seen == 0.0, row, 0.0))
        seen = seen + row
    first = jnp.concatenate(firsts, axis=0)
    second = ch - first
    pos = seg_start + rank

    def pick(w):
        has = jnp.sum(w, axis=0, keepdims=True) > 0.0
        return (jnp.where(has, jnp.sum(w * pos, axis=0, keepdims=True), -1.0),
                jnp.sum(w * gates, axis=0, keepdims=True))

    pos1, g1 = pick(first)
    pos2, g2 = pick(second)
    return pos1, pos2, g1, g2


def _segment_copies(lstart_ref, far_ref, ngrp_ref, base, make_copy):
    total = jnp.int32(0)
    for e in range(N_EXPERTS):
        n = ngrp_ref[base + e]
        near = lstart_ref[base + e]
        far = far_ref[base + e]

        def body(j, carry, near=near, far=far):
            make_copy(pl.multiple_of(near + j * MOE_ALIGN, MOE_ALIGN),
                      pl.multiple_of(far + j * MOE_ALIGN, MOE_ALIGN)).start()
            return carry

        lax.fori_loop(0, n, body, 0)
        total = total + n
    return total


def _wait_copies(count, make_copy):
    def body(j, carry):
        make_copy(0, 0).wait()
        return carry

    lax.fori_loop(0, count, body, 0)


def _dispatch_kernel(lstart_ref, dst_ref, ngrp_ref, tail_ref, tailn_ref, nused_ref, x_ref, prow_ref, xs_ref,
                     loc_ref, zero_ref, pending_ref, sems):
    t = pl.program_id(0)
    slot = t % 2
    base = t * N_EXPERTS
    tm = x_ref.shape[0]
    rl = loc_ref.shape[1]
    pos = prow_ref[...].astype(jnp.int32)
    r = lax.broadcasted_iota(jnp.int32, (rl, tm), 0)
    perm = jnp.logical_or(r == pos[0:1, :], r == pos[1:2, :]).astype(BF16)
    loc_ref[slot] = _dot(perm, x_ref[...].astype(BF16)).astype(BF16)

    def to_sorted(buf):
        def make(src_row, dst_row):
            return pltpu.make_async_copy(loc_ref.at[buf, pl.ds(src_row, MOE_ALIGN), :],
                                         xs_ref.at[pl.ds(dst_row, MOE_ALIGN), :], sems.at[buf])
        return make

    @pl.when(t > 0)
    def _():
        _wait_copies(pending_ref[0], to_sorted(1 - slot))

    n_copies = _segment_copies(lstart_ref, dst_ref, ngrp_ref, base, to_sorted(slot))
    pending_ref[0] = n_copies
    sem = sems.at[slot]

    @pl.when(t == pl.num_programs(0) - 1)
    def _():
        _wait_copies(n_copies, to_sorted(slot))
        zero_ref[...] = jnp.zeros_like(zero_ref)

        def zero_fill(src_row, dst_row):
            return pltpu.make_async_copy(zero_ref.at[pl.ds(0, MOE_ALIGN), :],
                                         xs_ref.at[pl.ds(dst_row, MOE_ALIGN), :], sem)

        total = jnp.int32(0)
        for e in range(N_EXPERTS):
            n = tailn_ref[e]
            start = tail_ref[e]

            def body(j, carry, start=start):
                zero_fill(0, pl.multiple_of(start + j * MOE_ALIGN, MOE_ALIGN)).start()
                return carry

            lax.fori_loop(0, n, body, 0)
            total = total + n
        _wait_copies(total, zero_fill)

        rt = zero_ref.shape[0]

        def zero_tile(src_row, dst_row):
            return pltpu.make_async_copy(zero_ref, xs_ref.at[pl.ds(dst_row, rt), :], sem)

        n_unused = xs_ref.shape[0] // rt - nused_ref[0]

        def tile_body(j, carry):
            zero_tile(0, pl.multiple_of((nused_ref[0] + j) * rt, rt)).start()
            return carry

        lax.fori_loop(0, n_unused, tile_body, 0)
        _wait_copies(n_unused, zero_tile)


def _experts_kernel(te_ref, nused_ref, xs_ref, wup_ref, wdn_ref, ys_ref):
    d_exp = wdn_ref.shape[2]
    used = pl.program_id(0) < nused_ref[0]

    @pl.when(used)
    def _():
        h = _dot(xs_ref[...], wup_ref[0, 0].astype(BF16))
        act = jax.nn.silu(h[:, :d_exp]) * h[:, d_exp:]
        ys_ref[...] = _dot(act.astype(BF16), wdn_ref[0, 0].astype(BF16)).astype(ys_ref.dtype)

    @pl.when(jnp.logical_not(used))
    def _():
        ys_ref[...] = jnp.zeros_like(ys_ref)


def _combine_kernel(lstart_ref, dst_ref, ngrp_ref, x_ref, pcol_ref, ys_ref, g_ref, b_ref, *rest,
                    n_prompt_tiles, split):
    if split:
        op_ref, os_ref, loc_ref, pending_ref, sems = rest
    else:
        o_ref, loc_ref, pending_ref, sems = rest
    t = pl.program_id(0)
    slot = t % 2
    base = t * N_EXPERTS
    tm = x_ref.shape[0]
    rl = loc_ref.shape[1]

    def from_sorted(buf):
        def make(dst_row, src_row):
            return pltpu.make_async_copy(ys_ref.at[pl.ds(src_row, MOE_ALIGN), :],
                                         loc_ref.at[buf, pl.ds(dst_row, MOE_ALIGN), :], sems.at[buf])
        return make

    @pl.when(t == 0)
    def _():
        loc_ref[...] = jnp.zeros_like(loc_ref)
        pending_ref[0] = _segment_copies(lstart_ref, dst_ref, ngrp_ref, base, from_sorted(0))

    @pl.when(t + 1 < pl.num_programs(0))
    def _():
        pending_ref[1 - slot] = _segment_copies(lstart_ref, dst_ref, ngrp_ref, base + N_EXPERTS,
                                                from_sorted(1 - slot))

    cols = pcol_ref[...]
    lane = lax.broadcasted_iota(jnp.int32, (tm, rl), 1)
    sel = (jnp.where(lane == cols[:, 0:1].astype(jnp.int32), cols[:, 2:3], 0.0)
           + jnp.where(lane == cols[:, 1:2].astype(jnp.int32), cols[:, 3:4], 0.0)).astype(BF16)
    _wait_copies(pending_ref[slot], from_sorted(slot))
    x = x_ref[...]
    y = _layer_norm(ALPHA * x + _dot(sel, loc_ref[slot]), g_ref[...], b_ref[...])
    if split:
        @pl.when(t < n_prompt_tiles)
        def _():
            op_ref[...] = y

        @pl.when(t >= n_prompt_tiles)
        def _():
            os_ref[...] = y
    else:
        o_ref[...] = y


def _moe_layer(x, rwt, rbias, w_up, w_down, ln_g, ln_b, *, layer, n_prompt, split):
    n, d = x.shape
    tm = TOKEN_TILE
    n_tiles = n // tm
    d_exp = w_down.shape[2]
    n_prompt_tiles = n_prompt // tm
    n_e, al, rt = N_EXPERTS, MOE_ALIGN, MOE_ROW_TILE
    rl = _moe_local_rows(tm)
    n_row_tiles = -(-(2 * n + n_tiles * n_e * (al - 1) + n_e * (rt - al)) // rt)

    pos_rows, pos_cols, cnt_raw = pl.pallas_call(
        _route_kernel,
        grid=(n_tiles,),
        in_specs=[pl.BlockSpec((tm, d), lambda i: (i, 0)),
                  pl.BlockSpec((n_e, d), lambda i: (0, 0)),
                  pl.BlockSpec((n_e, 1), lambda i: (0, 0))],
        out_specs=[pl.BlockSpec((SUBLANES, tm), lambda i: (0, i)),
                   pl.BlockSpec((tm, LANES), lambda i: (i, 0)),
                   pl.BlockSpec((n_e, LANES), lambda i: (i, 0))],
        out_shape=[jax.ShapeDtypeStruct((SUBLANES, n), F32), jax.ShapeDtypeStruct((n, LANES), F32),
                   jax.ShapeDtypeStruct((n_tiles * n_e, LANES), jnp.int32)],
        compiler_params=_params(("parallel",)),
        name="moe_route",
    )(x, rwt, rbias)

    cnt = cnt_raw[:, 0].reshape(n_tiles, n_e)
    pad = (cnt + (al - 1)) // al * al
    lstart = jnp.cumsum(pad, axis=1) - pad
    tot = jnp.sum(pad, axis=0)
    cap = (tot + (rt - 1)) // rt * rt
    cend = jnp.cumsum(cap)
    dst = (cend - cap)[None, :] + jnp.cumsum(pad, axis=0) - pad
    n_used = (cend[-1] // rt).astype(jnp.int32)
    tile_id = jnp.arange(n_row_tiles, dtype=jnp.int32)
    tile_expert = jnp.sum((tile_id[:, None] * rt >= cend[None, :]).astype(jnp.int32), axis=1)
    tile_expert = jnp.take(tile_expert, jnp.minimum(tile_id, n_used - 1))
    i32 = lambda a: a.reshape(-1).astype(jnp.int32)
    lstart, dst, ngrp = i32(lstart), i32(dst), i32(pad // al)
    tail, tailn = i32(cend - cap + tot), i32((cap - tot) // al)

    xs = pl.pallas_call(
        _dispatch_kernel,
        grid_spec=pltpu.PrefetchScalarGridSpec(
            num_scalar_prefetch=6,
            grid=(n_tiles,),
            in_specs=[pl.BlockSpec((tm, d), lambda i, *_: (i, 0)),
                      pl.BlockSpec((SUBLANES, tm), lambda i, *_: (0, i))],
            out_specs=pl.BlockSpec(memory_space=pl.ANY),
            scratch_shapes=[pltpu.VMEM((2, rl, d), BF16), pltpu.VMEM((rt, d), BF16),
                            pltpu.SMEM((1,), jnp.int32), pltpu.SemaphoreType.DMA((2,))],
        ),
        out_shape=jax.ShapeDtypeStruct((n_row_tiles * rt, d), BF16),
        compiler_params=_params(("arbitrary",)),
        name="moe_dispatch",
    )(lstart, dst, ngrp, tail, tailn, n_used.reshape(1), x, pos_rows)

    last_used = lambda i, te, nu: (jnp.minimum(i, nu[0] - 1), 0)
    ys = pl.pallas_call(
        _experts_kernel,
        grid_spec=pltpu.PrefetchScalarGridSpec(
            num_scalar_prefetch=2,
            grid=(n_row_tiles,),
            in_specs=[pl.BlockSpec((rt, d), last_used),
                      pl.BlockSpec((1, 1, d, 2 * d_exp), lambda i, te, nu: (layer, te[i], 0, 0)),
                      pl.BlockSpec((1, 1, d_exp, d), lambda i, te, nu: (layer, te[i], 0, 0))],
            out_specs=pl.BlockSpec((rt, d), lambda i, te, nu: (i, 0)),
        ),
        out_shape=jax.ShapeDtypeStruct((n_row_tiles * rt, d), BF16),
        compiler_params=_params(("arbitrary",)),
        name="moe_experts",
    )(tile_expert, n_used.reshape(1), xs, w_up, w_down)

    if split:
        lo, hi = _lo_index(n_prompt_tiles), _hi_index(n_prompt_tiles)
        out_specs = [pl.BlockSpec((tm, d), lambda i, *_: lo(i)), pl.BlockSpec((tm, d), lambda i, *_: hi(i))]
        out_shape = [jax.ShapeDtypeStruct((n_prompt, d), F32), jax.ShapeDtypeStruct((n - n_prompt, d), F32)]
    else:
        out_specs = pl.BlockSpec((tm, d), lambda i, *_: (i, 0))
        out_shape = jax.ShapeDtypeStruct((n, d), F32)
    return pl.pallas_call(
        functools.partial(_combine_kernel, n_prompt_tiles=n_prompt_tiles, split=split),
        grid_spec=pltpu.PrefetchScalarGridSpec(
            num_scalar_prefetch=3,
            grid=(n_tiles,),
            in_specs=[pl.BlockSpec((tm, d), lambda i, *_: (i, 0)),
                      pl.BlockSpec((tm, LANES), lambda i, *_: (i, 0)),
                      pl.BlockSpec(memory_space=pl.ANY),
                      pl.BlockSpec((1, d), lambda i, *_: (0, 0)),
                      pl.BlockSpec((1, d), lambda i, *_: (0, 0))],
            out_specs=out_specs,
            scratch_shapes=[pltpu.VMEM((2, rl, d), BF16), pltpu.SMEM((2,), jnp.int32),
                            pltpu.SemaphoreType.DMA((2,))],
        ),
        out_shape=out_shape,
        compiler_params=_params(("arbitrary",)),
        name="moe_combine",
    )(lstart, dst, ngrp, x, pos_cols, ys, ln_g[None], ln_b[None])


def _kvq_kernel(x_ref, wk_ref, wv_ref, wq_ref, kp_ref, vp_ref, ks_ref, vs_ref, kb_ref, vb_ref, qb_ref,
                *, n_prompt_tiles):
    i = pl.program_id(0)
    xb = x_ref[...].astype(BF16)
    k = _dot(xb, wk_ref[...])
    v = _dot(xb, wv_ref[...])
    tm = k.shape[0]
    vd = k.shape[1] // N_HEADS

    def store_heads(ref, val):
        for h in range(N_HEADS):
            ref[pl.ds(h, tm, stride=N_HEADS), :] = val[:, h * vd:(h + 1) * vd]

    @pl.when(i < n_prompt_tiles)
    def _():
        store_heads(kp_ref, k)
        store_heads(vp_ref, v)
        kb_ref[...] = k.astype(BF16)
        vb_ref[...] = v.astype(BF16)
        qb_ref[...] = (_dot(xb, wq_ref[...]) * Q_SCALE).astype(BF16)

    @pl.when(i >= n_prompt_tiles)
    def _():
        store_heads(ks_ref, k)
        store_heads(vs_ref, v)


def _kvq_proj(x, w_k, w_v, w_q, *, n_prompt):
    n, d = x.shape
    tm = TOKEN_TILE
    n_prompt_tiles = n_prompt // tm
    n_sample = n - n_prompt
    vd = d // N_HEADS
    lo = pl.BlockSpec((tm, d), _lo_index(n_prompt_tiles))
    lo_heads = pl.BlockSpec((tm * N_HEADS, vd), _lo_index(n_prompt_tiles))
    hi_heads = pl.BlockSpec((tm * N_HEADS, vd), _hi_index(n_prompt_tiles))
    wspec = pl.BlockSpec((d, d), lambda i: (0, 0))
    return pl.pallas_call(
        functools.partial(_kvq_kernel, n_prompt_tiles=n_prompt_tiles),
        grid=(n // tm,),
        in_specs=[pl.BlockSpec((tm, d), lambda i: (i, 0)), wspec, wspec, wspec],
        out_specs=[lo_heads, lo_heads, hi_heads, hi_heads, lo, lo, lo],
        out_shape=[jax.ShapeDtypeStruct((n_prompt * N_HEADS, vd), F32),
                   jax.ShapeDtypeStruct((n_prompt * N_HEADS, vd), F32),
                   jax.ShapeDtypeStruct((n_sample * N_HEADS, vd), F32),
                   jax.ShapeDtypeStruct((n_sample * N_HEADS, vd), F32),
                   jax.ShapeDtypeStruct((n_prompt, d), BF16), jax.ShapeDtypeStruct((n_prompt, d), BF16),
                   jax.ShapeDtypeStruct((n_prompt, d), BF16)],
        compiler_params=_params(("arbitrary",)),
        name="kvq_proj",
    )(x, w_k.astype(BF16), w_v.astype(BF16), w_q.astype(BF16))


def _q_kernel(x_ref, wq_ref, qb_ref):
    qb_ref[...] = (_dot(x_ref[...].astype(BF16), wq_ref[...]) * Q_SCALE).astype(BF16)


def _q_proj(x, w_q, n_rows):
    d = x.shape[1]
    tm = TOKEN_TILE
    row = pl.BlockSpec((tm, d), lambda i: (i, 0))
    return pl.pallas_call(
        _q_kernel,
        grid=(n_rows // tm,),
        in_specs=[row, pl.BlockSpec((d, d), lambda i: (0, 0))],
        out_specs=row,
        out_shape=jax.ShapeDtypeStruct((n_rows, d), BF16),
        compiler_params=_params(("parallel",)),
        name="q_proj",
    )(x, w_q.astype(BF16))


def _diff_lambda(lam_ref, lam_init):
    lv = lam_ref[...]
    e1 = jnp.exp(jnp.sum(lv[0:1, :] * lv[1:2, :], axis=1, keepdims=True))
    e2 = jnp.exp(jnp.sum(lv[2:3, :] * lv[3:4, :], axis=1, keepdims=True))
    return e1 - e2 + lam_init


def _lane_tile(stat, n):
    return stat if n == LANES else jnp.concatenate([stat] * (n // LANES), axis=1)


def _softmax_step(s, vc, m_ref, l_ref, acc_ref, rows=None):
    sl = slice(None) if rows is None else rows
    m_old = m_ref[sl, :]
    m_new = jnp.maximum(m_old, jnp.max(s, axis=1, keepdims=True))
    alpha = jnp.exp2(m_old - m_new)
    p = jnp.exp2(s - _lane_tile(m_new, s.shape[1]))
    l_ref[sl, :] = alpha * l_ref[sl, :] + jnp.sum(p, axis=1, keepdims=True)
    acc_ref[sl, :] = alpha * acc_ref[sl, :] + _dot(p.astype(BF16), vc)
    m_ref[sl, :] = m_new


def _head_norm(o, g, lam_init):
    return o * lax.rsqrt(jnp.mean(o * o, axis=-1, keepdims=True) + LN_EPS) * g * (1.0 - lam_init)


def _pattn_kernel(lam_ref, g_ref, q_ref, k_ref, v_ref, o_ref,
                  m1_ref, l1_ref, a1_ref, m2_ref, l2_ref, a2_ref, *, lam_init):
    qi = pl.program_id(2)
    tq = q_ref.shape[0]
    tk = ATTN_K_TILE
    per_tile = tq // tk
    hd = HEAD_DIM
    q = q_ref[...]
    q1, q2 = q[:, :hd], q[:, hd:]
    for m_ref, l_ref, a_ref in ((m1_ref, l1_ref, a1_ref), (m2_ref, l2_ref, a2_ref)):
        m_ref[...] = jnp.full_like(m_ref, NEG)
        l_ref[...] = jnp.zeros_like(l_ref)
        a_ref[...] = jnp.zeros_like(a_ref)

    def chunk(start, first_row=None):
        kc = k_ref[pl.ds(start, tk), :]
        vc = v_ref[pl.ds(start, tk), :]
        rows = slice(first_row or 0, tq)
        s1 = _dot_nt(q1[rows], kc[:, :hd])
        s2 = _dot_nt(q2[rows], kc[:, hd:])
        if first_row is not None:
            causal = (lax.broadcasted_iota(jnp.int32, s1.shape, 0) >= lax.broadcasted_iota(jnp.int32, s1.shape, 1))
            s1 = jnp.where(causal, s1, NEG)
            s2 = jnp.where(causal, s2, NEG)
        _softmax_step(s1, vc, m1_ref, l1_ref, a1_ref, rows)
        _softmax_step(s2, vc, m2_ref, l2_ref, a2_ref, rows)

    def body(j, carry):
        chunk(pl.multiple_of(j * tk, tk))
        return carry

    lax.fori_loop(0, qi * per_tile, body, 0)
    for jj in range(per_tile):
        chunk(pl.multiple_of(qi * tq + jj * tk, tk), first_row=jj * tk)

    lam = _diff_lambda(lam_ref, lam_init)
    o = a1_ref[...] / l1_ref[...] - lam * (a2_ref[...] / l2_ref[...])
    o_ref[...] = _head_norm(o, g_ref[...], lam_init).astype(o_ref.dtype)


def _prompt_attention(qb, kb, vb, lam_rows, subln_g, lam_init, *, batch, seq):
    d = qb.shape[1]
    vd = d // N_HEADS
    tq = ATTN_Q_TILE
    nq = seq // tq
    kern = functools.partial(_pattn_kernel, lam_init=lam_init)
    stat = pltpu.VMEM((tq, LANES), F32)
    return pl.pallas_call(
        kern,
        grid=(batch, N_HEADS, nq),
        in_specs=[
            pl.BlockSpec((4, HEAD_DIM), lambda b, h, i: (0, 0)),
            pl.BlockSpec((1, vd), lambda b, h, i: (0, 0)),
            pl.BlockSpec((tq, vd), lambda b, h, i: (b * nq + i, h)),
            pl.BlockSpec((seq, vd), lambda b, h, i: (b, h)),
            pl.BlockSpec((seq, vd), lambda b, h, i: (b, h)),
        ],
        out_specs=pl.BlockSpec((tq, vd), lambda b, h, i: (b * nq + i, h)),
        out_shape=jax.ShapeDtypeStruct((batch * seq, d), BF16),
        scratch_shapes=[stat, stat, pltpu.VMEM((tq, vd), F32), stat, stat, pltpu.VMEM((tq, vd), F32)],
        compiler_params=_params(("parallel", "parallel", "arbitrary")),
        name="prompt_attention",
    )(lam_rows, subln_g[None], qb, kb, vb)


def _sattn_kernel(pt_ref, lam_ref, g_ref, x_ref, wq_ref, kn_ref, vn_ref, *rest, lam_init, dec_seq):
    npg = SAMPLE_SEQS_PER_STEP * SAMPLE_PAGES_PER_STEP
    k_refs = rest[:npg]
    v_refs = rest[npg:2 * npg]
    o_ref = rest[2 * npg]
    qbd_ref, m_ref, l_ref, acc_ref = rest[2 * npg + 1:]
    c = pl.program_id(1)
    n_rows = SAMPLE_SEQS_PER_STEP * dec_seq
    vd = qbd_ref.shape[1]
    hrows = 2 * n_rows
    half = N_HEADS // 2
    prow = 2 * hrows

    def first_row(h):
        return (h % half) * prow + (h // half) * hrows

    @pl.when(c == 0)
    def _():
        q = _dot(x_ref[...].astype(BF16), wq_ref[...]) * Q_SCALE
        lane = lax.broadcasted_iota(jnp.int32, (n_rows, vd), 1)
        parts = []
        for h in sorted(range(N_HEADS), key=first_row):
            qh = q[:, h * vd:(h + 1) * vd]
            parts += [jnp.where(lane < HEAD_DIM, qh, 0.0), jnp.where(lane >= HEAD_DIM, qh, 0.0)]
        qbd_ref[...] = jnp.concatenate(parts, axis=0).astype(BF16)
        m_ref[...] = jnp.full_like(m_ref, NEG)
        l_ref[...] = jnp.zeros_like(l_ref)
        acc_ref[...] = jnp.zeros_like(acc_ref)

    keys_per_seq = SAMPLE_PAGES_PER_STEP * PAGE_SIZE
    n_cols = 2 * SAMPLE_SEQS_PER_STEP * keys_per_seq
    n_score_rows = N_HEADS * hrows
    rr = lax.broadcasted_iota(jnp.int32, (n_score_rows, n_cols), 0)
    cc = lax.broadcasted_iota(jnp.int32, (n_score_rows, n_cols), 1)
    valid = jnp.logical_and((rr // hrows) % 2 == cc % 2,
                            (rr % n_rows) // dec_seq == cc // (2 * keys_per_seq))

    def pair_rows(refs, j):
        both = pl.ds(j, 2 * PAGE_SIZE, stride=half)
        return jnp.concatenate([r.at[0][both, :] for r in refs], axis=0).astype(BF16)

    s = jnp.concatenate([_dot_nt(qbd_ref[j * prow:(j + 1) * prow, :], pair_rows(k_refs, j))
                         for j in range(half)], axis=0)
    s = jnp.where(valid, s, NEG)
    m_old = m_ref[...]
    m_new = jnp.maximum(m_old, jnp.max(s, axis=1, keepdims=True))
    alpha = jnp.exp2(m_old - m_new)
    p = jnp.exp2(s - _lane_tile(m_new, n_cols))
    l_ref[...] = alpha * l_ref[...] + jnp.sum(p, axis=1, keepdims=True)
    pb = p.astype(BF16)
    pv = jnp.concatenate([_dot(pb[j * prow:(j + 1) * prow, :], pair_rows(v_refs, j))
                          for j in range(half)], axis=0)
    acc_ref[...] = alpha * acc_ref[...] + pv
    m_ref[...] = m_new

    @pl.when(c == pl.num_programs(1) - 1)
    def _():
        lam = _diff_lambda(lam_ref, lam_init)
        r2 = lax.broadcasted_iota(jnp.int32, (hrows, LANES), 0) % n_rows
        k2 = lax.broadcasted_iota(jnp.int32, (hrows, LANES), 1)
        ok = (k2 < n_rows) & (k2 // dec_seq == r2 // dec_seq) & (k2 % dec_seq <= r2 % dec_seq)
        pad = jnp.zeros((LANES - n_rows, vd), F32)
        outs = []
        for h in range(N_HEADS):
            new_head = pl.ds(h, n_rows, stride=N_HEADS)
            kn = jnp.concatenate([kn_ref[new_head, :], pad], axis=0).astype(BF16)
            vn = jnp.concatenate([vn_ref[new_head, :], pad], axis=0).astype(BF16)
            rows = slice(first_row(h), first_row(h) + hrows)
            sn = jnp.where(ok, _dot_nt(qbd_ref[rows, :], kn), NEG)
            _softmax_step(sn, vn, m_ref, l_ref, acc_ref, rows)
            o = acc_ref[rows, :] / l_ref[rows, :]
            outs.append(_head_norm(o[:n_rows] - lam * o[n_rows:], g_ref[...], lam_init))
        o_ref[...] = jnp.concatenate(outs, axis=1)


def _sample_attention(x, w_q, k_new, v_new, cache_k, cache_v, page_table, lam_rows, subln_g, lam_init,
                      *, n_prompt, dec_seq):
    n, d = x.shape
    vd = d // N_HEADS
    n_seq, n_pg = page_table.shape
    n_rows = SAMPLE_SEQS_PER_STEP * dec_seq
    n_cols = 2 * N_HEADS * n_rows
    pps = SAMPLE_PAGES_PER_STEP
    slab = PAGE_SIZE * N_HEADS
    ck = cache_k.reshape(cache_k.shape[0], slab, vd)
    cv = cache_v.reshape(cache_v.shape[0], slab, vd)
    pt = page_table.reshape(-1)
    base = n_prompt // n_rows

    def page_spec(s, j):
        return pl.BlockSpec((1, slab, vd),
                            lambda b, c, pt: (pt[(b * SAMPLE_SEQS_PER_STEP + s) * n_pg + c * pps + j], 0, 0))

    page_specs = [page_spec(s, j) for s in range(SAMPLE_SEQS_PER_STEP) for j in range(pps)]
    new_rows = pl.BlockSpec((n_rows * N_HEADS, vd), lambda b, c, pt: (b, 0))
    kern = functools.partial(_sattn_kernel, lam_init=lam_init, dec_seq=dec_seq)
    stat = pltpu.VMEM((n_cols, LANES), F32)
    grid_spec = pltpu.PrefetchScalarGridSpec(
        num_scalar_prefetch=1,
        grid=(n_seq // SAMPLE_SEQS_PER_STEP, n_pg // pps),
        in_specs=[
            pl.BlockSpec((4, HEAD_DIM), lambda b, c, pt: (0, 0)),
            pl.BlockSpec((1, vd), lambda b, c, pt: (0, 0)),
            pl.BlockSpec((n_rows, d), lambda b, c, pt: (base + b, 0)),
            pl.BlockSpec((d, d), lambda b, c, pt: (0, 0)),
            new_rows,
            new_rows,
        ] + page_specs + page_specs,
        out_specs=pl.BlockSpec((n_rows, d), lambda b, c, pt: (b, 0)),
        scratch_shapes=[pltpu.VMEM((n_cols, vd), BF16), stat, stat, pltpu.VMEM((n_cols, vd), F32)],
    )
    n_pages_step = SAMPLE_SEQS_PER_STEP * pps
    return pl.pallas_call(
        kern,
        grid_spec=grid_spec,
        out_shape=jax.ShapeDtypeStruct((n - n_prompt, d), F32),
        compiler_params=_params(("parallel", "arbitrary")),
        name="sample_attention",
    )(pt, lam_rows, subln_g[None], x, w_q.astype(BF16), k_new, v_new,
      *([ck] * n_pages_step), *([cv] * n_pages_step))


def _oproj_kernel(ap_ref, as_ref, w_ref, x_ref, g_ref, b_ref, o_ref, *, n_prompt_tiles):
    i = pl.program_id(0)
    a = jnp.where(i >= n_prompt_tiles, as_ref[...].astype(BF16), ap_ref[...])
    m = _dot(a, w_ref[...])
    o_ref[...] = _layer_norm(ALPHA * x_ref[...] + m, g_ref[...], b_ref[...])


def _oproj_layer(a_prompt, a_sample, w_o, x, ln_g, ln_b):
    n, d = x.shape
    tm = TOKEN_TILE
    n_prompt_tiles = a_prompt.shape[0] // tm
    row = pl.BlockSpec((tm, d), lambda i: (i, 0))
    const2 = lambda i: (0, 0)
    kern = functools.partial(_oproj_kernel, n_prompt_tiles=n_prompt_tiles)
    return pl.pallas_call(
        kern,
        grid=(n // tm,),
        in_specs=[
            pl.BlockSpec((tm, d), _lo_index(n_prompt_tiles)),
            pl.BlockSpec((tm, d), _hi_index(n_prompt_tiles)),
            pl.BlockSpec((d, d), const2),
            row,
            pl.BlockSpec((1, d), const2),
            pl.BlockSpec((1, d), const2),
        ],
        out_specs=row,
        out_shape=jax.ShapeDtypeStruct((n, d), F32),
        compiler_params=_params(("parallel",)),
        name="attn_out_proj",
    )(a_prompt, a_sample, w_o.astype(BF16), x, ln_g[None], ln_b[None])


def kernel(x_prompt, x_sample, cache_k, cache_v, page_table, ln1_g, ln1_b, ln2_g, ln2_b, a_w_in, a_b_in, a_lnv_g, a_lnv_b, a_w_s, a_b_s, a_w_out, w_k, w_v, b_w_q, b_lam_q1, b_lam_k1, b_lam_q2, b_lam_k2, b_subln_g, b_w_o, router_w, router_bias, moe_w_up, moe_w_down):
    batch, seq, d = x_prompt.shape
    dec_batch, dec_seq, _ = x_sample.shape
    n_prompt = batch * seq
    n_sample = dec_batch * dec_seq
    tm = TOKEN_TILE
    assert seq % CHUNK == 0 and CHUNK % dec_seq == 0 and dec_seq <= CHUNK
    assert n_prompt % tm == 0 and n_sample % tm == 0
    assert seq % ATTN_Q_TILE == 0 and ATTN_Q_TILE % ATTN_K_TILE == 0
    assert a_w_s.shape[-1] == CHUNK and cache_k.shape[1] == PAGE_SIZE and cache_k.shape[2] == N_HEADS
    assert dec_batch % SAMPLE_SEQS_PER_STEP == 0 and page_table.shape[1] % SAMPLE_PAGES_PER_STEP == 0
    assert SAMPLE_SEQS_PER_STEP * dec_seq == SUBLANES

    n_prompt_tiles = n_prompt // tm
    rwt = router_w.T.astype(BF16)
    rbias = router_bias.astype(F32)[:, None]
    chunk_v = []
    x = None
    for l in range(DEPTH):
        if l < N_A_LAYERS:
            if l == 0:
                xa, xb = x_prompt.reshape(n_prompt, d), x_sample.reshape(n_sample, d)
                ia, ib = _lo_index(n_prompt_tiles), _hi_index(n_prompt_tiles)
            else:
                xa, xb = x, x
                ia, ib = _lo_index(n_prompt_tiles), (lambda i: (jnp.maximum(i, n_prompt_tiles), 0))
            x, v_rows = _gmlp_layer(xa, xb, ia, ib, a_w_in[l], a_b_in[l], a_lnv_g[l], a_lnv_b[l], a_w_s[l],
                                    a_b_s[l], a_w_out[l], ln1_g[l], ln1_b[l],
                                    n_prompt=n_prompt, n_sample=n_sample, dec_seq=dec_seq)
            chunk_v.append(v_rows.reshape(dec_batch, dec_seq, -1))
        else:
            j = l - N_A_LAYERS
            lam_init = 0.8 - 0.6 * math.exp(-0.3 * l)
            if j == 0:
                kp, vp, ks, vs, kb, vb, qb = _kvq_proj(x, w_k, w_v, b_w_q[j], n_prompt=n_prompt)
            else:
                qb = _q_proj(x, b_w_q[j], n_prompt)
            lam_rows = jnp.stack([b_lam_q1[j], b_lam_k1[j], b_lam_q2[j], b_lam_k2[j]]).astype(F32)
            a_p = _prompt_attention(qb, kb, vb, lam_rows, b_subln_g[j], lam_init, batch=batch, seq=seq)
            a_s = _sample_attention(x, b_w_q[j], ks, vs, cache_k, cache_v, page_table, lam_rows,
                                    b_subln_g[j], lam_init, n_prompt=n_prompt, dec_seq=dec_seq)
            x = _oproj_layer(a_p, a_s, b_w_o[j], x, ln1_g[l], ln1_b[l])
        x = _moe_layer(x, rwt, rbias, moe_w_up, moe_w_down, ln2_g[l], ln2_b[l],
                       layer=l, n_prompt=n_prompt, split=(l == DEPTH - 1))

    yp, ys = x
    vd = d // N_HEADS
    return (yp.reshape(batch, seq, d), ys.reshape(dec_batch, dec_seq, d),
            kp.reshape(batch, seq, N_HEADS, vd), vp.reshape(batch, seq, N_HEADS, vd),
            ks.reshape(dec_batch, dec_seq, N_HEADS, vd), vs.reshape(dec_batch, dec_seq, N_HEADS, vd),
            jnp.stack(chunk_v))
```

```python
import functools
import math

import jax
import jax.numpy as jnp
from jax import lax
from jax.experimental import pallas as pl
from jax.experimental.pallas import tpu as pltpu

F32 = jnp.float32
BF16 = jnp.bfloat16

DEPTH = 4
N_A_LAYERS = DEPTH // 2
CHUNK = 128
A_GROUPS = 8
N_HEADS = 8
HEAD_DIM = 64
N_EXPERTS = 16
N_EXPERT_GROUPS = 4
EXPERTS_PER_GROUP = N_EXPERTS // N_EXPERT_GROUPS
ALPHA = (2.0 * DEPTH) ** 0.25
LN_EPS = 1e-5
NEG = -1e30
PAGE_SIZE = 128
Q_SCALE = HEAD_DIM ** -0.5 * math.log2(math.e)

LANES = 128
SUBLANES = 8
TOKEN_TILE = 512
ATTN_Q_TILE = 512
ATTN_K_TILE = 512
ATTN_DIAG_TILE = 256
SAMPLE_SEQS_PER_STEP = 2
SAMPLE_PAGES_PER_STEP = 4
VMEM_LIMIT = 56 * 1024 * 1024


def _params(semantics):
    return pltpu.CompilerParams(dimension_semantics=semantics, vmem_limit_bytes=VMEM_LIMIT)


def _layer_norm(x, g, b):
    mu = jnp.mean(x, axis=-1, keepdims=True)
    xc = x - mu
    var = jnp.mean(xc * xc, axis=-1, keepdims=True)
    return xc * lax.rsqrt(var + LN_EPS) * g + b


def _dot(a, b):
    return jnp.dot(a, b, preferred_element_type=F32)


def _dot_nt(a, b):
    return lax.dot_general(a, b, (((1,), (1,)), ((), ())), preferred_element_type=F32)


def _lo_index(n_prompt_tiles):
    return lambda i: (jnp.minimum(i, n_prompt_tiles - 1), 0)


def _hi_index(n_prompt_tiles):
    return lambda i: (jnp.maximum(i - n_prompt_tiles, 0), 0)


def _gmlp_kernel(xp_ref, xs_ref, win_ref, bin_ref, lvg_ref, lvb_ref, ws_ref, sb_ref,
                 wout_ref, g1_ref, b1_ref, xo_ref, v_ref, *, n_prompt_tiles, dec_seq):
    i = pl.program_id(0)
    is_sample = i >= n_prompt_tiles
    a_inner = wout_ref.shape[0]
    gd = a_inner // A_GROUPS
    tm = xp_ref.shape[0]

    x = jnp.where(is_sample, xs_ref[...], xp_ref[...])
    z = _dot(x.astype(BF16), win_ref[...]) + bin_ref[...]
    z = jax.nn.gelu(z)
    u = z[:, :a_inner]
    v = _layer_norm(z[:, a_inner:], lvg_ref[...], lvb_ref[...])

    @pl.when(is_sample)
    def _():
        v_ref[...] = v

    row = lax.broadcasted_iota(jnp.int32, (CHUNK, CHUNK), 0)
    col = lax.broadcasted_iota(jnp.int32, (CHUNK, CHUNK), 1)
    mask = jnp.logical_and(col <= row, jnp.logical_or(jnp.logical_not(is_sample),
                                                      (row // dec_seq) == (col // dec_seq)))
    vb = v.astype(BF16)
    kind = is_sample.astype(jnp.int32)
    bias = sb_ref[kind]
    s_rows = []
    mixes = [jnp.where(mask, ws_ref[kind, g], 0.0).astype(BF16) for g in range(A_GROUPS)]
    for c in range(tm // CHUNK):
        cols = []
        for g in range(A_GROUPS):
            blk = vb[c * CHUNK:(c + 1) * CHUNK, g * gd:(g + 1) * gd]
            cols.append(_dot(mixes[g], blk))
        s_rows.append(jnp.concatenate(cols, axis=1) + bias)
    s = jnp.concatenate(s_rows, axis=0)
    m = _dot((u * s).astype(BF16), wout_ref[...])
    xo_ref[...] = _layer_norm(ALPHA * x + m, g1_ref[...], b1_ref[...])


def _gmlp_layer(x_prompt, x_sample, prompt_index, sample_index, w_in, b_in, lnv_g, lnv_b, w_s, b_s, w_out,
                ln_g, ln_b, *, n_prompt, n_sample, dec_seq):
    d = x_prompt.shape[1]
    a_inner = w_out.shape[0]
    tm = TOKEN_TILE
    n = n_prompt + n_sample
    n_prompt_tiles = n_prompt // tm
    reps = CHUNK // dec_seq
    ws = jnp.stack([w_s, jnp.tile(w_s[:, :dec_seq, :dec_seq], (1, reps, reps))])
    gd = a_inner // A_GROUPS
    sb_p = jnp.repeat(b_s.T, gd, axis=1)
    sb_s = jnp.repeat(jnp.tile(b_s[:, :dec_seq].T, (reps, 1)), gd, axis=1)
    sb = jnp.stack([sb_p, sb_s])
    const2 = lambda i: (0, 0)
    const3 = lambda i: (0, 0, 0)
    const4 = lambda i: (0, 0, 0, 0)
    kern = functools.partial(_gmlp_kernel, n_prompt_tiles=n_prompt_tiles, dec_seq=dec_seq)
    return pl.pallas_call(
        kern,
        grid=(n // tm,),
        in_specs=[
            pl.BlockSpec((tm, d), prompt_index),
            pl.BlockSpec((tm, d), sample_index),
            pl.BlockSpec((d, 2 * a_inner), const2),
            pl.BlockSpec((1, 2 * a_inner), const2),
            pl.BlockSpec((1, a_inner), const2),
            pl.BlockSpec((1, a_inner), const2),
            pl.BlockSpec((2, A_GROUPS, CHUNK, CHUNK), const4),
            pl.BlockSpec((2, CHUNK, a_inner), const3),
            pl.BlockSpec((a_inner, d), const2),
            pl.BlockSpec((1, d), const2),
            pl.BlockSpec((1, d), const2),
        ],
        out_specs=[
            pl.BlockSpec((tm, d), lambda i: (i, 0)),
            pl.BlockSpec((tm, a_inner), _hi_index(n_prompt_tiles)),
        ],
        out_shape=[jax.ShapeDtypeStruct((n, d), F32), jax.ShapeDtypeStruct((n_sample, a_inner), F32)],
        compiler_params=_params(("arbitrary",)),
        name="gmlp_layer",
    )(x_prompt, x_sample, w_in.astype(BF16), b_in[None], lnv_g[None], lnv_b[None], ws, sb,
      w_out.astype(BF16), ln_g[None], ln_b[None])


def _route(x_bf, rwt_ref, rbias_ref):
    logits = _dot_nt(rwt_ref[...], x_bf)
    mx = jnp.max(logits, axis=0, keepdims=True)
    ex = jnp.exp(logits - mx)
    probs = ex / jnp.sum(ex, axis=0, keepdims=True)
    sel = probs + rbias_ref[...]
    p = [probs[e:e + 1, :] for e in range(N_EXPERTS)]
    s = [sel[e:e + 1, :] for e in range(N_EXPERTS)]
    gscore = []
    for g in range(N_EXPERT_GROUPS):
        a, b, c, d = s[4 * g:4 * g + 4]
        hi1, lo1 = jnp.maximum(a, b), jnp.minimum(a, b)
        hi2, lo2 = jnp.maximum(c, d), jnp.minimum(c, d)
        top1 = jnp.maximum(hi1, hi2)
        top2 = jnp.maximum(jnp.minimum(hi1, hi2), jnp.maximum(lo1, lo2))
        gscore.append(top1 + top2)
    best = gscore[0]
    gbest = jnp.zeros_like(best, dtype=jnp.int32)
    for g in range(1, N_EXPERT_GROUPS):
        better = gscore[g] > best
        best = jnp.where(better, gscore[g], best)
        gbest = jnp.where(better, g, gbest)
    gates = []
    for e in range(N_EXPERTS):
        g = e // EXPERTS_PER_GROUP
        rank = jnp.zeros_like(gbest)
        for f in range(EXPERTS_PER_GROUP * g, EXPERTS_PER_GROUP * (g + 1)):
            if f == e:
                continue
            beats = (s[f] >= s[e]) if f < e else (s[f] > s[e])
            rank = rank + beats.astype(jnp.int32)
        chosen = (gbest == g) & (rank < 2)
        gates.append(jnp.where(chosen, p[e], 0.0))
    denom = gates[0]
    for e in range(1, N_EXPERTS):
        denom = denom + gates[e]
    return [gt / denom for gt in gates]


MOE_ALIGN = 16
MOE_ROW_TILE = 512


def _moe_local_rows(tm):
    return 2 * tm + N_EXPERTS * MOE_ALIGN


def _route_kernel(x_ref, rwt_ref, rbias_ref, prow_ref, pcol_ref, cnt_ref):
    gates = jnp.concatenate(_route(x_ref[...].astype(BF16), rwt_ref, rbias_ref), axis=0)
    tm = gates.shape[1]
    cnt = jnp.sum((gates > 0.0).astype(F32), axis=1, keepdims=True)
    cnt_i = cnt.astype(jnp.int32)
    cnt_ref[...] = jnp.broadcast_to(cnt_i, cnt_ref.shape)
    shift = MOE_ALIGN.bit_length() - 1
    padded = lax.shift_left(lax.shift_right_logical(cnt_i + (MOE_ALIGN - 1), shift), shift).astype(F32)
    starts, run = [], jnp.zeros((1, 1), F32)
    for e in range(N_EXPERTS):
        starts.append(run)
        run = run + padded[e:e + 1, :]
    pos1, pos2, g1, g2 = _local_positions(gates, jnp.concatenate(starts, axis=0))
    prow_ref[...] = jnp.concatenate([pos1, pos2, g1, g2, jnp.zeros((SUBLANES - 4, tm), F32)], axis=0)
    rows = jnp.concatenate([pos1, pos2, g1, g2, jnp.zeros((LANES - 4, tm), F32)], axis=0)
    pcol_ref[...] = rows.T


def _local_positions(gates, seg_start):
    n_e, t = gates.shape
    chosen = gates > 0.0
    ch_bf = chosen.astype(BF16)
    ta = lax.broadcasted_iota(jnp.int32, (t, t), 0)
    tb = lax.broadcasted_iota(jnp.int32, (t, t), 1)
    rank = _dot(ch_bf, (ta < tb).astype(BF16))
    ch = chosen.astype(F32)
    seen = jnp.zeros((1, t), F32)
    firsts = []
    for e in range(n_e):
        row = ch[e:e + 1, :]
        firsts.append(jnp.where(seen == 0.0, row, 0.0))
        seen = seen + row
    first = jnp.concatenate(firsts, axis=0)
    second = ch - first
    pos = seg_start + rank

    def pick(w):
        has = jnp.sum(w, axis=0, keepdims=True) > 0.0
        return (jnp.where(has, jnp.sum(w * pos, axis=0, keepdims=True), -1.0),
                jnp.sum(w * gates, axis=0, keepdims=True))

    pos1, g1 = pick(first)
    pos2, g2 = pick(second)
    return pos1, pos2, g1, g2


def _segment_copies(lstart_ref, far_ref, ngrp_ref, base, make_copy):
    total = jnp.int32(0)
    for e in range(N_EXPERTS):
        n = ngrp_ref[base + e]
        near = lstart_ref[base + e]
        far = far_ref[base + e]

        def body(j, carry, near=near, far=far):
            make_copy(pl.multiple_of(near + j * MOE_ALIGN, MOE_ALIGN),
                      pl.multiple_of(far + j * MOE_ALIGN, MOE_ALIGN)).start()
            return carry

        lax.fori_loop(0, n, body, 0)
        total = total + n
    return total


def _wait_copies(count, make_copy):
    def body(j, carry):
        make_copy(0, 0).wait()
        return carry

    lax.fori_loop(0, count, body, 0)


def _dispatch_kernel(lstart_ref, dst_ref, ngrp_ref, tail_ref, tailn_ref, nused_ref, x_ref, prow_ref, xs_ref,
                     loc_ref, zero_ref, pending_ref, sems):
    t = pl.program_id(0)
    slot = t % 2
    base = t * N_EXPERTS
    tm = x_ref.shape[0]
    rl = loc_ref.shape[1]
    pos = prow_ref[...].astype(jnp.int32)
    r = lax.broadcasted_iota(jnp.int32, (rl, tm), 0)
    perm = jnp.logical_or(r == pos[0:1, :], r == pos[1:2, :]).astype(BF16)
    loc_ref[slot] = _dot(perm, x_ref[...].astype(BF16)).astype(BF16)

    def to_sorted(buf):
        def make(src_row, dst_row):
            return pltpu.make_async_copy(loc_ref.at[buf, pl.ds(src_row, MOE_ALIGN), :],
                                         xs_ref.at[pl.ds(dst_row, MOE_ALIGN), :], sems.at[buf])
        return make

    @pl.when(t > 0)
    def _():
        _wait_copies(pending_ref[0], to_sorted(1 - slot))

    n_copies = _segment_copies(lstart_ref, dst_ref, ngrp_ref, base, to_sorted(slot))
    pending_ref[0] = n_copies
    sem = sems.at[slot]

    @pl.when(t == pl.num_programs(0) - 1)
    def _():
        _wait_copies(n_copies, to_sorted(slot))
        zero_ref[...] = jnp.zeros_like(zero_ref)

        def zero_fill(src_row, dst_row):
            return pltpu.make_async_copy(zero_ref.at[pl.ds(0, MOE_ALIGN), :],
                                         xs_ref.at[pl.ds(dst_row, MOE_ALIGN), :], sem)

        total = jnp.int32(0)
        for e in range(N_EXPERTS):
            n = tailn_ref[e]
            start = tail_ref[e]

            def body(j, carry, start=start):
                zero_fill(0, pl.multiple_of(start + j * MOE_ALIGN, MOE_ALIGN)).start()
                return carry

            lax.fori_loop(0, n, body, 0)
            total = total + n
        _wait_copies(total, zero_fill)

        rt = zero_ref.shape[0]

        def zero_tile(src_row, dst_row):
            return pltpu.make_async_copy(zero_ref, xs_ref.at[pl.ds(dst_row, rt), :], sem)

        n_unused = xs_ref.shape[0] // rt - nused_ref[0]

        def tile_body(j, carry):
            zero_tile(0, pl.multiple_of((nused_ref[0] + j) * rt, rt)).start()
            return carry

        lax.fori_loop(0, n_unused, tile_body, 0)
        _wait_copies(n_unused, zero_tile)


def _experts_kernel(te_ref, nused_ref, xs_ref, wup_ref, wdn_ref, ys_ref):
    d_exp = wdn_ref.shape[2]
    used = pl.program_id(0) < nused_ref[0]

    @pl.when(used)
    def _():
        h = _dot(xs_ref[...], wup_ref[0, 0].astype(BF16))
        act = jax.nn.silu(h[:, :d_exp]) * h[:, d_exp:]
        ys_ref[...] = _dot(act.astype(BF16), wdn_ref[0, 0].astype(BF16)).astype(ys_ref.dtype)

    @pl.when(jnp.logical_not(used))
    def _():
        ys_ref[...] = jnp.zeros_like(ys_ref)


def _combine_kernel(lstart_ref, dst_ref, ngrp_ref, x_ref, pcol_ref, ys_ref, g_ref, b_ref, *rest,
                    n_prompt_tiles, split):
    if split:
        op_ref, os_ref, loc_ref, pending_ref, sems = rest
    else:
        o_ref, loc_ref, pending_ref, sems = rest
    t = pl.program_id(0)
    slot = t % 2
    base = t * N_EXPERTS
    tm = x_ref.shape[0]
    rl = loc_ref.shape[1]

    def from_sorted(buf):
        def make(dst_row, src_row):
            return pltpu.make_async_copy(ys_ref.at[pl.ds(src_row, MOE_ALIGN), :],
                                         loc_ref.at[buf, pl.ds(dst_row, MOE_ALIGN), :], sems.at[buf])
        return make

    @pl.when(t == 0)
    def _():
        loc_ref[...] = jnp.zeros_like(loc_ref)
        pending_ref[0] = _segment_copies(lstart_ref, dst_ref, ngrp_ref, base, from_sorted(0))

    @pl.when(t + 1 < pl.num_programs(0))
    def _():
        pending_ref[1 - slot] = _segment_copies(lstart_ref, dst_ref, ngrp_ref, base + N_EXPERTS,
                                                from_sorted(1 - slot))

    cols = pcol_ref[...]
    lane = lax.broadcasted_iota(jnp.int32, (tm, rl), 1)
    sel = (jnp.where(lane == cols[:, 0:1].astype(jnp.int32), cols[:, 2:3], 0.0)
           + jnp.where(lane == cols[:, 1:2].astype(jnp.int32), cols[:, 3:4], 0.0)).astype(BF16)
    _wait_copies(pending_ref[slot], from_sorted(slot))
    x = x_ref[...]
    y = _layer_norm(ALPHA * x + _dot(sel, loc_ref[slot]), g_ref[...], b_ref[...])
    if split:
        @pl.when(t < n_prompt_tiles)
        def _():
            op_ref[...] = y

        @pl.when(t >= n_prompt_tiles)
        def _():
            os_ref[...] = y
    else:
        o_ref[...] = y


def _moe_layer(x, rwt, rbias, w_up, w_down, ln_g, ln_b, *, layer, n_prompt, split):
    n, d = x.shape
    tm = TOKEN_TILE
    n_tiles = n // tm
    d_exp = w_down.shape[2]
    n_prompt_tiles = n_prompt // tm
    n_e, al, rt = N_EXPERTS, MOE_ALIGN, MOE_ROW_TILE
    rl = _moe_local_rows(tm)
    n_row_tiles = -(-(2 * n + n_tiles * n_e * (al - 1) + n_e * (rt - al)) // rt)

    pos_rows, pos_cols, cnt_raw = pl.pallas_call(
        _route_kernel,
        grid=(n_tiles,),
        in_specs=[pl.BlockSpec((tm, d), lambda i: (i, 0)),
                  pl.BlockSpec((n_e, d), lambda i: (0, 0)),
                  pl.BlockSpec((n_e, 1), lambda i: (0, 0))],
        out_specs=[pl.BlockSpec((SUBLANES, tm), lambda i: (0, i)),
                   pl.BlockSpec((tm, LANES), lambda i: (i, 0)),
                   pl.BlockSpec((n_e, LANES), lambda i: (i, 0))],
        out_shape=[jax.ShapeDtypeStruct((SUBLANES, n), F32), jax.ShapeDtypeStruct((n, LANES), F32),
                   jax.ShapeDtypeStruct((n_tiles * n_e, LANES), jnp.int32)],
        compiler_params=_params(("parallel",)),
        name="moe_route",
    )(x, rwt, rbias)

    cnt = cnt_raw[:, 0].reshape(n_tiles, n_e)
    pad = (cnt + (al - 1)) // al * al
    lstart = jnp.cumsum(pad, axis=1) - pad
    tot = jnp.sum(pad, axis=0)
    cap = (tot + (rt - 1)) // rt * rt
    cend = jnp.cumsum(cap)
    dst = (cend - cap)[None, :] + jnp.cumsum(pad, axis=0) - pad
    n_used = (cend[-1] // rt).astype(jnp.int32)
    tile_id = jnp.arange(n_row_tiles, dtype=jnp.int32)
    tile_expert = jnp.sum((tile_id[:, None] * rt >= cend[None, :]).astype(jnp.int32), axis=1)
    tile_expert = jnp.take(tile_expert, jnp.minimum(tile_id, n_used - 1))
    i32 = lambda a: a.reshape(-1).astype(jnp.int32)
    lstart, dst, ngrp = i32(lstart), i32(dst), i32(pad // al)
    tail, tailn = i32(cend - cap + tot), i32((cap - tot) // al)

    xs = pl.pallas_call(
        _dispatch_kernel,
        grid_spec=pltpu.PrefetchScalarGridSpec(
            num_scalar_prefetch=6,
            grid=(n_tiles,),
            in_specs=[pl.BlockSpec((tm, d), lambda i, *_: (i, 0)),
                      pl.BlockSpec((SUBLANES, tm), lambda i, *_: (0, i))],
            out_specs=pl.BlockSpec(memory_space=pl.ANY),
            scratch_shapes=[pltpu.VMEM((2, rl, d), BF16), pltpu.VMEM((rt, d), BF16),
                            pltpu.SMEM((1,), jnp.int32), pltpu.SemaphoreType.DMA((2,))],
        ),
        out_shape=jax.ShapeDtypeStruct((n_row_tiles * rt, d), BF16),
        compiler_params=_params(("arbitrary",)),
        name="moe_dispatch",
    )(lstart, dst, ngrp, tail, tailn, n_used.reshape(1), x, pos_rows)

    last_used = lambda i, te, nu: (jnp.minimum(i, nu[0] - 1), 0)
    ys = pl.pallas_call(
        _experts_kernel,
        grid_spec=pltpu.PrefetchScalarGridSpec(
            num_scalar_prefetch=2,
            grid=(n_row_tiles,),
            in_specs=[pl.BlockSpec((rt, d), last_used),
                      pl.BlockSpec((1, 1, d, 2 * d_exp), lambda i, te, nu: (layer, te[i], 0, 0)),
                      pl.BlockSpec((1, 1, d_exp, d), lambda i, te, nu: (layer, te[i], 0, 0))],
            out_specs=pl.BlockSpec((rt, d), lambda i, te, nu: (i, 0)),
        ),
        out_shape=jax.ShapeDtypeStruct((n_row_tiles * rt, d), BF16),
        compiler_params=_params(("arbitrary",)),
        name="moe_experts",
    )(tile_expert, n_used.reshape(1), xs, w_up, w_down)

    if split:
        lo, hi = _lo_index(n_prompt_tiles), _hi_index(n_prompt_tiles)
        out_specs = [pl.BlockSpec((tm, d), lambda i, *_: lo(i)), pl.BlockSpec((tm, d), lambda i, *_: hi(i))]
        out_shape = [jax.ShapeDtypeStruct((n_prompt, d), F32), jax.ShapeDtypeStruct((n - n_prompt, d), F32)]
    else:
        out_specs = pl.BlockSpec((tm, d), lambda i, *_: (i, 0))
        out_shape = jax.ShapeDtypeStruct((n, d), F32)
    return pl.pallas_call(
        functools.partial(_combine_kernel, n_prompt_tiles=n_prompt_tiles, split=split),
        grid_spec=pltpu.PrefetchScalarGridSpec(
            num_scalar_prefetch=3,
            grid=(n_tiles,),
            in_specs=[pl.BlockSpec((tm, d), lambda i, *_: (i, 0)),
                      pl.BlockSpec((tm, LANES), lambda i, *_: (i, 0)),
                      pl.BlockSpec(memory_space=pl.ANY),
                      pl.BlockSpec((1, d), lambda i, *_: (0, 0)),
                      pl.BlockSpec((1, d), lambda i, *_: (0, 0))],
            out_specs=out_specs,
            scratch_shapes=[pltpu.VMEM((2, rl, d), BF16), pltpu.SMEM((2,), jnp.int32),
                            pltpu.SemaphoreType.DMA((2,))],
        ),
        out_shape=out_shape,
        compiler_params=_params(("arbitrary",)),
        name="moe_combine",
    )(lstart, dst, ngrp, x, pos_cols, ys, ln_g[None], ln_b[None])


def _kvq_kernel(x_ref, wk_ref, wv_ref, wq_ref, kp_ref, vp_ref, ks_ref, vs_ref, kb_ref, vb_ref, qb_ref,
                *, n_prompt_tiles):
    i = pl.program_id(0)
    xb = x_ref[...].astype(BF16)
    k = _dot(xb, wk_ref[...])
    v = _dot(xb, wv_ref[...])
    tm = k.shape[0]
    vd = k.shape[1] // N_HEADS

    def store_heads(ref, val):
        for h in range(N_HEADS):
            ref[pl.ds(h, tm, stride=N_HEADS), :] = val[:, h * vd:(h + 1) * vd]

    @pl.when(i < n_prompt_tiles)
    def _():
        store_heads(kp_ref, k)
        store_heads(vp_ref, v)
        kb_ref[...] = k.astype(BF16)
        vb_ref[...] = v.astype(BF16)
        qb_ref[...] = (_dot(xb, wq_ref[...]) * Q_SCALE).astype(BF16)

    @pl.when(i >= n_prompt_tiles)
    def _():
        store_heads(ks_ref, k)
        store_heads(vs_ref, v)


def _kvq_proj(x, w_k, w_v, w_q, *, n_prompt):
    n, d = x.shape
    tm = TOKEN_TILE
    n_prompt_tiles = n_prompt // tm
    n_sample = n - n_prompt
    vd = d // N_HEADS
    lo = pl.BlockSpec((tm, d), _lo_index(n_prompt_tiles))
    lo_heads = pl.BlockSpec((tm * N_HEADS, vd), _lo_index(n_prompt_tiles))
    hi_heads = pl.BlockSpec((tm * N_HEADS, vd), _hi_index(n_prompt_tiles))
    wspec = pl.BlockSpec((d, d), lambda i: (0, 0))
    return pl.pallas_call(
        functools.partial(_kvq_kernel, n_prompt_tiles=n_prompt_tiles),
        grid=(n // tm,),
        in_specs=[pl.BlockSpec((tm, d), lambda i: (i, 0)), wspec, wspec, wspec],
        out_specs=[lo_heads, lo_heads, hi_heads, hi_heads, lo, lo, lo],
        out_shape=[jax.ShapeDtypeStruct((n_prompt * N_HEADS, vd), F32),
                   jax.ShapeDtypeStruct((n_prompt * N_HEADS, vd), F32),
                   jax.ShapeDtypeStruct((n_sample * N_HEADS, vd), F32),
                   jax.ShapeDtypeStruct((n_sample * N_HEADS, vd), F32),
                   jax.ShapeDtypeStruct((n_prompt, d), BF16), jax.ShapeDtypeStruct((n_prompt, d), BF16),
                   jax.ShapeDtypeStruct((n_prompt, d), BF16)],
        compiler_params=_params(("arbitrary",)),
        name="kvq_proj",
    )(x, w_k.astype(BF16), w_v.astype(BF16), w_q.astype(BF16))


def _q_kernel(x_ref, wq_ref, qb_ref):
    qb_ref[...] = (_dot(x_ref[...].astype(BF16), wq_ref[...]) * Q_SCALE).astype(BF16)


def _q_proj(x, w_q, n_rows):
    d = x.shape[1]
    tm = TOKEN_TILE
    row = pl.BlockSpec((tm, d), lambda i: (i, 0))
    return pl.pallas_call(
        _q_kernel,
        grid=(n_rows // tm,),
        in_specs=[row, pl.BlockSpec((d, d), lambda i: (0, 0))],
        out_specs=row,
        out_shape=jax.ShapeDtypeStruct((n_rows, d), BF16),
        compiler_params=_params(("parallel",)),
        name="q_proj",
    )(x, w_q.astype(BF16))


def _diff_lambda(lam_ref, lam_init):
    lv = lam_ref[...]
    e1 = jnp.exp(jnp.sum(lv[0:1, :] * lv[1:2, :], axis=1, keepdims=True))
    e2 = jnp.exp(jnp.sum(lv[2:3, :] * lv[3:4, :], axis=1, keepdims=True))
    return e1 - e2 + lam_init


def _lane_tile(stat, n):
    return stat if n == LANES else jnp.concatenate([stat] * (n // LANES), axis=1)


def _softmax_step(s, vc, m_ref, l_ref, acc_ref, rows=None):
    sl = slice(None) if rows is None else rows
    m_old = m_ref[sl, :]
    m_new = jnp.maximum(m_old, jnp.max(s, axis=1, keepdims=True))
    alpha = jnp.exp2(m_old - m_new)
    p = jnp.exp2(s - _lane_tile(m_new, s.shape[1]))
    l_ref[sl, :] = alpha * l_ref[sl, :] + jnp.sum(p, axis=1, keepdims=True)
    acc_ref[sl, :] = alpha * acc_ref[sl, :] + _dot(p.astype(BF16), vc)
    m_ref[sl, :] = m_new


def _head_norm(o, g, lam_init):
    return o * lax.rsqrt(jnp.mean(o * o, axis=-1, keepdims=True) + LN_EPS) * g * (1.0 - lam_init)


def _pattn_body(qi, lam_ref, g_ref, q_ref, k_ref, v_ref, o_ref,
                m1_ref, l1_ref, a1_ref, m2_ref, l2_ref, a2_ref, *, lam_init):
    tq = q_ref.shape[0]
    tk = ATTN_K_TILE
    td = ATTN_DIAG_TILE
    hd = HEAD_DIM
    q = q_ref[...]
    q1, q2 = q[:, :hd], q[:, hd:]
    for m_ref, l_ref, a_ref in ((m1_ref, l1_ref, a1_ref), (m2_ref, l2_ref, a2_ref)):
        m_ref[...] = jnp.full_like(m_ref, NEG)
        l_ref[...] = jnp.zeros_like(l_ref)
        a_ref[...] = jnp.zeros_like(a_ref)

    def chunk(start, width, first_row=None):
        kc = k_ref[pl.ds(start, width), :]
        vc = v_ref[pl.ds(start, width), :]
        rows = slice(first_row or 0, tq)
        s1 = _dot_nt(q1[rows], kc[:, :hd])
        s2 = _dot_nt(q2[rows], kc[:, hd:])
        if first_row is not None:
            causal = (lax.broadcasted_iota(jnp.int32, s1.shape, 0) >= lax.broadcasted_iota(jnp.int32, s1.shape, 1))
            s1 = jnp.where(causal, s1, NEG)
            s2 = jnp.where(causal, s2, NEG)
        _softmax_step(s1, vc, m1_ref, l1_ref, a1_ref, rows)
        _softmax_step(s2, vc, m2_ref, l2_ref, a2_ref, rows)

    def body(j, carry):
        chunk(pl.multiple_of(j * tk, tk), tk)
        return carry

    lax.fori_loop(0, qi * (tq // tk), body, 0)
    for jj in range(tq // td):
        chunk(pl.multiple_of(qi * tq + jj * td, td), td, first_row=jj * td)

    lam = _diff_lambda(lam_ref, lam_init)
    o = a1_ref[...] / l1_ref[...] - lam * (a2_ref[...] / l2_ref[...])
    o_ref[...] = _head_norm(o, g_ref[...], lam_init).astype(o_ref.dtype)


def _sattn_body(c, n_chunks, lam_ref, g_ref, x_ref, wq_ref, kn_ref, vn_ref, k_refs, v_refs, o_ref,
                qbd_ref, m_ref, l_ref, acc_ref, *, lam_init, dec_seq):
    n_rows = SAMPLE_SEQS_PER_STEP * dec_seq
    vd = qbd_ref.shape[1]
    hrows = 2 * n_rows
    half = N_HEADS // 2
    prow = 2 * hrows

    def first_row(h):
        return (h % half) * prow + (h // half) * hrows

    @pl.when(c == 0)
    def _():
        q = _dot(x_ref[...].astype(BF16), wq_ref[...]) * Q_SCALE
        lane = lax.broadcasted_iota(jnp.int32, (n_rows, vd), 1)
        parts = []
        for h in sorted(range(N_HEADS), key=first_row):
            qh = q[:, h * vd:(h + 1) * vd]
            parts += [jnp.where(lane < HEAD_DIM, qh, 0.0), jnp.where(lane >= HEAD_DIM, qh, 0.0)]
        qbd_ref[...] = jnp.concatenate(parts, axis=0).astype(BF16)
        m_ref[...] = jnp.full_like(m_ref, NEG)
        l_ref[...] = jnp.zeros_like(l_ref)
        acc_ref[...] = jnp.zeros_like(acc_ref)

    keys_per_seq = SAMPLE_PAGES_PER_STEP * PAGE_SIZE
    n_cols = 2 * SAMPLE_SEQS_PER_STEP * keys_per_seq
    n_score_rows = N_HEADS * hrows
    rr = lax.broadcasted_iota(jnp.int32, (n_score_rows, n_cols), 0)
    cc = lax.broadcasted_iota(jnp.int32, (n_score_rows, n_cols), 1)
    valid = jnp.logical_and((rr // hrows) % 2 == cc % 2,
                            (rr % n_rows) // dec_seq == cc // (2 * keys_per_seq))

    def pair_rows(refs, j):
        both = pl.ds(j, 2 * PAGE_SIZE, stride=half)
        return jnp.concatenate([r.at[0][both, :] for r in refs], axis=0).astype(BF16)

    s = jnp.concatenate([_dot_nt(qbd_ref[j * prow:(j + 1) * prow, :], pair_rows(k_refs, j))
                         for j in range(half)], axis=0)
    s = jnp.where(valid, s, NEG)
    m_old = m_ref[...]
    m_new = jnp.maximum(m_old, jnp.max(s, axis=1, keepdims=True))
    alpha = jnp.exp2(m_old - m_new)
    p = jnp.exp2(s - _lane_tile(m_new, n_cols))
    l_ref[...] = alpha * l_ref[...] + jnp.sum(p, axis=1, keepdims=True)
    pb = p.astype(BF16)
    pv = jnp.concatenate([_dot(pb[j * prow:(j + 1) * prow, :], pair_rows(v_refs, j))
                          for j in range(half)], axis=0)
    acc_ref[...] = alpha * acc_ref[...] + pv
    m_ref[...] = m_new

    @pl.when(c == n_chunks - 1)
    def _():
        lam = _diff_lambda(lam_ref, lam_init)
        r2 = lax.broadcasted_iota(jnp.int32, (hrows, LANES), 0) % n_rows
        k2 = lax.broadcasted_iota(jnp.int32, (hrows, LANES), 1)
        ok = (k2 < n_rows) & (k2 // dec_seq == r2 // dec_seq) & (k2 % dec_seq <= r2 % dec_seq)
        pad = jnp.zeros((LANES - n_rows, vd), F32)
        outs = []
        for h in range(N_HEADS):
            new_head = pl.ds(h, n_rows, stride=N_HEADS)
            kn = jnp.concatenate([kn_ref[new_head, :], pad], axis=0).astype(BF16)
            vn = jnp.concatenate([vn_ref[new_head, :], pad], axis=0).astype(BF16)
            rows = slice(first_row(h), first_row(h) + hrows)
            sn = jnp.where(ok, _dot_nt(qbd_ref[rows, :], kn), NEG)
            _softmax_step(sn, vn, m_ref, l_ref, acc_ref, rows)
            o = acc_ref[rows, :] / l_ref[rows, :]
            outs.append(_head_norm(o[:n_rows] - lam * o[n_rows:], g_ref[...], lam_init))
        o_ref[...] = jnp.concatenate(outs, axis=1)


def _attn_kernel(pt_ref, lam_ref, g_ref, q_ref, k_ref, v_ref, x_ref, wq_ref, kn_ref, vn_ref, *rest,
                 nq, n_chunks, lam_init, dec_seq):
    npg = SAMPLE_SEQS_PER_STEP * SAMPLE_PAGES_PER_STEP
    k_refs = rest[:npg]
    v_refs = rest[npg:2 * npg]
    op_ref, os_ref = rest[2 * npg:2 * npg + 2]
    prompt_scratch = rest[2 * npg + 2:2 * npg + 8]
    sample_scratch = rest[2 * npg + 8:]
    step = pl.program_id(0)
    _pattn_body(step % nq, lam_ref, g_ref, q_ref, k_ref, v_ref, op_ref, *prompt_scratch, lam_init=lam_init)
    _sattn_body(step % n_chunks, n_chunks, lam_ref, g_ref, x_ref, wq_ref, kn_ref, vn_ref, k_refs, v_refs,
                os_ref, *sample_scratch, lam_init=lam_init, dec_seq=dec_seq)


def _attention(qb, kb, vb, x, w_q, k_new, v_new, cache_k, cache_v, page_table, lam_rows, subln_g, lam_init,
               *, batch, seq, dec_seq):
    n, d = x.shape
    n_prompt = batch * seq
    vd = d // N_HEADS
    tq = ATTN_Q_TILE
    nq = seq // tq
    n_seq, n_pg = page_table.shape
    n_rows = SAMPLE_SEQS_PER_STEP * dec_seq
    n_cols = 2 * N_HEADS * n_rows
    pps = SAMPLE_PAGES_PER_STEP
    n_chunks = n_pg // pps
    n_steps = batch * N_HEADS * nq
    assert n_steps == (n_seq // SAMPLE_SEQS_PER_STEP) * n_chunks
    slab = PAGE_SIZE * N_HEADS
    ck = cache_k.reshape(cache_k.shape[0], slab, vd)
    cv = cache_v.reshape(cache_v.shape[0], slab, vd)
    pt = page_table.reshape(-1)
    base = n_prompt // n_rows

    q_index = lambda g, pt: ((g // (N_HEADS * nq)) * nq + g % nq, (g // nq) % N_HEADS)
    kv_index = lambda g, pt: (g // (N_HEADS * nq), (g // nq) % N_HEADS)
    const = lambda g, pt: (0, 0)

    def page_spec(s, j):
        def index(g, pt):
            row = (g // n_chunks) * SAMPLE_SEQS_PER_STEP + s
            return (pt[row * n_pg + (g % n_chunks) * pps + j], 0, 0)
        return pl.BlockSpec((1, slab, vd), index)

    page_specs = [page_spec(s, j) for s in range(SAMPLE_SEQS_PER_STEP) for j in range(pps)]
    new_rows = pl.BlockSpec((n_rows * N_HEADS, vd), lambda g, pt: (g // n_chunks, 0))
    pstat = pltpu.VMEM((tq, LANES), F32)
    sstat = pltpu.VMEM((n_cols, LANES), F32)
    grid_spec = pltpu.PrefetchScalarGridSpec(
        num_scalar_prefetch=1,
        grid=(n_steps,),
        in_specs=[
            pl.BlockSpec((4, HEAD_DIM), const),
            pl.BlockSpec((1, vd), const),
            pl.BlockSpec((tq, vd), q_index),
            pl.BlockSpec((seq, vd), kv_index),
            pl.BlockSpec((seq, vd), kv_index),
            pl.BlockSpec((n_rows, d), lambda g, pt: (base + g // n_chunks, 0)),
            pl.BlockSpec((d, d), const),
            new_rows,
            new_rows,
        ] + page_specs + page_specs,
        out_specs=[pl.BlockSpec((tq, vd), q_index),
                   pl.BlockSpec((n_rows, d), lambda g, pt: (g // n_chunks, 0))],
        scratch_shapes=[pstat, pstat, pltpu.VMEM((tq, vd), F32), pstat, pstat, pltpu.VMEM((tq, vd), F32),
                        pltpu.VMEM((n_cols, vd), BF16), sstat, sstat, pltpu.VMEM((n_cols, vd), F32)],
    )
    n_pages_step = SAMPLE_SEQS_PER_STEP * pps
    kern = functools.partial(_attn_kernel, nq=nq, n_chunks=n_chunks, lam_init=lam_init, dec_seq=dec_seq)
    return pl.pallas_call(
        kern,
        grid_spec=grid_spec,
        out_shape=[jax.ShapeDtypeStruct((n_prompt, d), BF16), jax.ShapeDtypeStruct((n - n_prompt, d), F32)],
        compiler_params=_params(("arbitrary",)),
        name="attention",
    )(pt, lam_rows, subln_g[None], qb, kb, vb, x, w_q.astype(BF16), k_new, v_new,
      *([ck] * n_pages_step), *([cv] * n_pages_step))


def _oproj_kernel(ap_ref, as_ref, w_ref, x_ref, g_ref, b_ref, o_ref, *, n_prompt_tiles):
    i = pl.program_id(0)
    a = jnp.where(i >= n_prompt_tiles, as_ref[...].astype(BF16), ap_ref[...])
    m = _dot(a, w_ref[...])
    o_ref[...] = _layer_norm(ALPHA * x_ref[...] + m, g_ref[...], b_ref[...])


def _oproj_layer(a_prompt, a_sample, w_o, x, ln_g, ln_b):
    n, d = x.shape
    tm = TOKEN_TILE
    n_prompt_tiles = a_prompt.shape[0] // tm
    row = pl.BlockSpec((tm, d), lambda i: (i, 0))
    const2 = lambda i: (0, 0)
    kern = functools.partial(_oproj_kernel, n_prompt_tiles=n_prompt_tiles)
    return pl.pallas_call(
        kern,
        grid=(n // tm,),
        in_specs=[
            pl.BlockSpec((tm, d), _lo_index(n_prompt_tiles)),
            pl.BlockSpec((tm, d), _hi_index(n_prompt_tiles)),
            pl.BlockSpec((d, d), const2),
            row,
            pl.BlockSpec((1, d), const2),
            pl.BlockSpec((1, d), const2),
        ],
        out_specs=row,
        out_shape=jax.ShapeDtypeStruct((n, d), F32),
        compiler_params=_params(("parallel",)),
        name="attn_out_proj",
    )(a_prompt, a_sample, w_o.astype(BF16), x, ln_g[None], ln_b[None])


def kernel(x_prompt, x_sample, cache_k, cache_v, page_table, ln1_g, ln1_b, ln2_g, ln2_b, a_w_in, a_b_in, a_lnv_g, a_lnv_b, a_w_s, a_b_s, a_w_out, w_k, w_v, b_w_q, b_lam_q1, b_lam_k1, b_lam_q2, b_lam_k2, b_subln_g, b_w_o, router_w, router_bias, moe_w_up, moe_w_down):
    batch, seq, d = x_prompt.shape
    dec_batch, dec_seq, _ = x_sample.shape
    n_prompt = batch * seq
    n_sample = dec_batch * dec_seq
    tm = TOKEN_TILE
    assert seq % CHUNK == 0 and CHUNK % dec_seq == 0 and dec_seq <= CHUNK
    assert n_prompt % tm == 0 and n_sample % tm == 0
    assert seq % ATTN_Q_TILE == 0 and ATTN_Q_TILE % ATTN_K_TILE == 0 and ATTN_Q_TILE % ATTN_DIAG_TILE == 0
    assert a_w_s.shape[-1] == CHUNK and cache_k.shape[1] == PAGE_SIZE and cache_k.shape[2] == N_HEADS
    assert dec_batch % SAMPLE_SEQS_PER_STEP == 0 and page_table.shape[1] % SAMPLE_PAGES_PER_STEP == 0
    assert SAMPLE_SEQS_PER_STEP * dec_seq == SUBLANES

    n_prompt_tiles = n_prompt // tm
    rwt = router_w.T.astype(BF16)
    rbias = router_bias.astype(F32)[:, None]
    chunk_v = []
    x = None
    for l in range(DEPTH):
        if l < N_A_LAYERS:
            if l == 0:
                xa, xb = x_prompt.reshape(n_prompt, d), x_sample.reshape(n_sample, d)
                ia, ib = _lo_index(n_prompt_tiles), _hi_index(n_prompt_tiles)
            else:
                xa, xb = x, x
                ia, ib = _lo_index(n_prompt_tiles), (lambda i: (jnp.maximum(i, n_prompt_tiles), 0))
            x, v_rows = _gmlp_layer(xa, xb, ia, ib, a_w_in[l], a_b_in[l], a_lnv_g[l], a_lnv_b[l], a_w_s[l],
                                    a_b_s[l], a_w_out[l], ln1_g[l], ln1_b[l],
                                    n_prompt=n_prompt, n_sample=n_sample, dec_seq=dec_seq)
            chunk_v.append(v_rows.reshape(dec_batch, dec_seq, -1))
        else:
            j = l - N_A_LAYERS
            lam_init = 0.8 - 0.6 * math.exp(-0.3 * l)
            if j == 0:
                kp, vp, ks, vs, kb, vb, qb = _kvq_proj(x, w_k, w_v, b_w_q[j], n_prompt=n_prompt)
            else:
                qb = _q_proj(x, b_w_q[j], n_prompt)
            lam_rows = jnp.stack([b_lam_q1[j], b_lam_k1[j], b_lam_q2[j], b_lam_k2[j]]).astype(F32)
            a_p, a_s = _attention(qb, kb, vb, x, b_w_q[j], ks, vs, cache_k, cache_v, page_table, lam_rows,
                                  b_subln_g[j], lam_init, batch=batch, seq=seq, dec_seq=dec_seq)
            x = _oproj_layer(a_p, a_s, b_w_o[j], x, ln1_g[l], ln1_b[l])
        x = _moe_layer(x, rwt, rbias, moe_w_up, moe_w_down, ln2_g[l], ln2_b[l],
                       layer=l, n_prompt=n_prompt, split=(l == DEPTH - 1))

    yp, ys = x
    vd = d // N_HEADS
    return (yp.reshape(batch, seq, d), ys.reshape(dec_batch, dec_seq, d),
            kp.reshape(batch, seq, N_HEADS, vd), vp.reshape(batch, seq, N_HEADS, vd),
            ks.reshape(dec_batch, dec_seq, N_HEADS, vd), vs.reshape(dec_batch, dec_seq, N_HEADS, vd),
            jnp.stack(chunk_v))
```

```python
import functools
import math

import jax
import jax.numpy as jnp
from jax import lax
from jax.experimental import pallas as pl
from jax.experimental.pallas import tpu as pltpu

F32 = jnp.float32
BF16 = jnp.bfloat16

DEPTH = 4
N_A_LAYERS = DEPTH // 2
CHUNK = 128
A_GROUPS = 8
N_HEADS = 8
HEAD_DIM = 64
N_EXPERTS = 16
N_EXPERT_GROUPS = 4
EXPERTS_PER_GROUP = N_EXPERTS // N_EXPERT_GROUPS
ALPHA = (2.0 * DEPTH) ** 0.25
LN_EPS = 1e-5
NEG = -1e30
PAGE_SIZE = 128
Q_SCALE = HEAD_DIM ** -0.5 * math.log2(math.e)

LANES = 128
SUBLANES = 8
TOKEN_TILE = 512
ATTN_Q_TILE = 1024
ATTN_K_TILE = 512
SAMPLE_SEQS_PER_STEP = 2
SAMPLE_PAGES_PER_STEP = 4
VMEM_LIMIT = 56 * 1024 * 1024


def _params(semantics):
    return pltpu.CompilerParams(dimension_semantics=semantics, vmem_limit_bytes=VMEM_LIMIT)


def _layer_norm(x, g, b):
    mu = jnp.mean(x, axis=-1, keepdims=True)
    xc = x - mu
    var = jnp.mean(xc * xc, axis=-1, keepdims=True)
    return xc * lax.rsqrt(var + LN_EPS) * g + b


def _dot(a, b):
    return jnp.dot(a, b, preferred_element_type=F32)


def _dot_nt(a, b):
    return lax.dot_general(a, b, (((1,), (1,)), ((), ())), preferred_element_type=F32)


def _lo_index(n_prompt_tiles):
    return lambda i: (jnp.minimum(i, n_prompt_tiles - 1), 0)


def _hi_index(n_prompt_tiles):
    return lambda i: (jnp.maximum(i - n_prompt_tiles, 0), 0)


def _gmlp_kernel(xp_ref, xs_ref, win_ref, bin_ref, lvg_ref, lvb_ref, ws_ref, sb_ref,
                 wout_ref, g1_ref, b1_ref, xo_ref, v_ref, *, n_prompt_tiles, dec_seq):
    i = pl.program_id(0)
    is_sample = i >= n_prompt_tiles
    a_inner = wout_ref.shape[0]
    gd = a_inner // A_GROUPS
    tm = xp_ref.shape[0]

    x = jnp.where(is_sample, xs_ref[...], xp_ref[...])
    z = _dot(x.astype(BF16), win_ref[...]) + bin_ref[...]
    z = jax.nn.gelu(z)
    u = z[:, :a_inner]
    v = _layer_norm(z[:, a_inner:], lvg_ref[...], lvb_ref[...])

    @pl.when(is_sample)
    def _():
        v_ref[...] = v

    row = lax.broadcasted_iota(jnp.int32, (CHUNK, CHUNK), 0)
    col = lax.broadcasted_iota(jnp.int32, (CHUNK, CHUNK), 1)
    mask = jnp.logical_and(col <= row, jnp.logical_or(jnp.logical_not(is_sample),
                                                      (row // dec_seq) == (col // dec_seq)))
    vb = v.astype(BF16)
    kind = is_sample.astype(jnp.int32)
    bias = sb_ref[kind]
    s_rows = []
    mixes = [jnp.where(mask, ws_ref[kind, g], 0.0).astype(BF16) for g in range(A_GROUPS)]
    for c in range(tm // CHUNK):
        cols = []
        for g in range(A_GROUPS):
            blk = vb[c * CHUNK:(c + 1) * CHUNK, g * gd:(g + 1) * gd]
            cols.append(_dot(mixes[g], blk))
        s_rows.append(jnp.concatenate(cols, axis=1) + bias)
    s = jnp.concatenate(s_rows, axis=0)
    m = _dot((u * s).astype(BF16), wout_ref[...])
    xo_ref[...] = _layer_norm(ALPHA * x + m, g1_ref[...], b1_ref[...])


def _gmlp_layer(x_prompt, x_sample, prompt_index, sample_index, w_in, b_in, lnv_g, lnv_b, w_s, b_s, w_out,
                ln_g, ln_b, *, n_prompt, n_sample, dec_seq):
    d = x_prompt.shape[1]
    a_inner = w_out.shape[0]
    tm = TOKEN_TILE
    n = n_prompt + n_sample
    n_prompt_tiles = n_prompt // tm
    reps = CHUNK // dec_seq
    ws = jnp.stack([w_s, jnp.tile(w_s[:, :dec_seq, :dec_seq], (1, reps, reps))])
    gd = a_inner // A_GROUPS
    sb_p = jnp.repeat(b_s.T, gd, axis=1)
    sb_s = jnp.repeat(jnp.tile(b_s[:, :dec_seq].T, (reps, 1)), gd, axis=1)
    sb = jnp.stack([sb_p, sb_s])
    const2 = lambda i: (0, 0)
    const3 = lambda i: (0, 0, 0)
    const4 = lambda i: (0, 0, 0, 0)
    kern = functools.partial(_gmlp_kernel, n_prompt_tiles=n_prompt_tiles, dec_seq=dec_seq)
    return pl.pallas_call(
        kern,
        grid=(n // tm,),
        in_specs=[
            pl.BlockSpec((tm, d), prompt_index),
            pl.BlockSpec((tm, d), sample_index),
            pl.BlockSpec((d, 2 * a_inner), const2),
            pl.BlockSpec((1, 2 * a_inner), const2),
            pl.BlockSpec((1, a_inner), const2),
            pl.BlockSpec((1, a_inner), const2),
            pl.BlockSpec((2, A_GROUPS, CHUNK, CHUNK), const4),
            pl.BlockSpec((2, CHUNK, a_inner), const3),
            pl.BlockSpec((a_inner, d), const2),
            pl.BlockSpec((1, d), const2),
            pl.BlockSpec((1, d), const2),
        ],
        out_specs=[
            pl.BlockSpec((tm, d), lambda i: (i, 0)),
            pl.BlockSpec((tm, a_inner), _hi_index(n_prompt_tiles)),
        ],
        out_shape=[jax.ShapeDtypeStruct((n, d), F32), jax.ShapeDtypeStruct((n_sample, a_inner), F32)],
        compiler_params=_params(("arbitrary",)),
        name="gmlp_layer",
    )(x_prompt, x_sample, w_in.astype(BF16), b_in[None], lnv_g[None], lnv_b[None], ws, sb,
      w_out.astype(BF16), ln_g[None], ln_b[None])


def _route(x_bf, rwt_ref, rbias_ref):
    logits = _dot_nt(rwt_ref[...], x_bf)
    mx = jnp.max(logits, axis=0, keepdims=True)
    ex = jnp.exp(logits - mx)
    probs = ex / jnp.sum(ex, axis=0, keepdims=True)
    sel = probs + rbias_ref[...]
    p = [probs[e:e + 1, :] for e in range(N_EXPERTS)]
    s = [sel[e:e + 1, :] for e in range(N_EXPERTS)]
    gscore = []
    for g in range(N_EXPERT_GROUPS):
        a, b, c, d = s[4 * g:4 * g + 4]
        hi1, lo1 = jnp.maximum(a, b), jnp.minimum(a, b)
        hi2, lo2 = jnp.maximum(c, d), jnp.minimum(c, d)
        top1 = jnp.maximum(hi1, hi2)
        top2 = jnp.maximum(jnp.minimum(hi1, hi2), jnp.maximum(lo1, lo2))
        gscore.append(top1 + top2)
    best = gscore[0]
    gbest = jnp.zeros_like(best, dtype=jnp.int32)
    for g in range(1, N_EXPERT_GROUPS):
        better = gscore[g] > best
        best = jnp.where(better, gscore[g], best)
        gbest = jnp.where(better, g, gbest)
    gates = []
    for e in range(N_EXPERTS):
        g = e // EXPERTS_PER_GROUP
        rank = jnp.zeros_like(gbest)
        for f in range(EXPERTS_PER_GROUP * g, EXPERTS_PER_GROUP * (g + 1)):
            if f == e:
                continue
            beats = (s[f] >= s[e]) if f < e else (s[f] > s[e])
            rank = rank + beats.astype(jnp.int32)
        chosen = (gbest == g) & (rank < 2)
        gates.append(jnp.where(chosen, p[e], 0.0))
    denom = gates[0]
    for e in range(1, N_EXPERTS):
        denom = denom + gates[e]
    return [gt / denom for gt in gates]


MOE_ALIGN = 16
MOE_ROW_TILE = 512
MOE_TOKEN_TILE = 512


def _moe_local_rows(tm):
    return 2 * tm + N_EXPERTS * MOE_ALIGN


def _route_kernel(x_ref, rwt_ref, rbias_ref, prow_ref, pcol_ref, cnt_ref):
    gates = jnp.concatenate(_route(x_ref[...].astype(BF16), rwt_ref, rbias_ref), axis=0)
    tm = gates.shape[1]
    cnt = jnp.sum((gates > 0.0).astype(F32), axis=1, keepdims=True)
    cnt_i = cnt.astype(jnp.int32)
    cnt_ref[...] = jnp.broadcast_to(cnt_i, cnt_ref.shape)
    shift = MOE_ALIGN.bit_length() - 1
    padded = lax.shift_left(lax.shift_right_logical(cnt_i + (MOE_ALIGN - 1), shift), shift).astype(F32)
    starts, run = [], jnp.zeros((1, 1), F32)
    for e in range(N_EXPERTS):
        starts.append(run)
        run = run + padded[e:e + 1, :]
    pos1, pos2, g1, g2 = _local_positions(gates, jnp.concatenate(starts, axis=0))
    prow_ref[...] = jnp.concatenate([pos1, pos2, g1, g2, jnp.zeros((SUBLANES - 4, tm), F32)], axis=0)
    rows = jnp.concatenate([pos1, pos2, g1, g2, jnp.zeros((LANES - 4, tm), F32)], axis=0)
    pcol_ref[...] = rows.T


def _local_positions(gates, seg_start):
    n_e, t = gates.shape
    chosen = gates > 0.0
    ch_bf = chosen.astype(BF16)
    ta = lax.broadcasted_iota(jnp.int32, (t, t), 0)
    tb = lax.broadcasted_iota(jnp.int32, (t, t), 1)
    rank = _dot(ch_bf, (ta < tb).astype(BF16))
    ch = chosen.astype(F32)
    seen = jnp.zeros((1, t), F32)
    firsts = []
    for e in range(n_e):
        row = ch[e:e + 1, :]
        firsts.append(jnp.where(seen == 0.0, row, 0.0))
        seen = seen + row
    first = jnp.concatenate(firsts, axis=0)
    second = ch - first
    pos = seg_start + rank

    def pick(w):
        has = jnp.sum(w, axis=0, keepdims=True) > 0.0
        return (jnp.where(has, jnp.sum(w * pos, axis=0, keepdims=True), -1.0),
                jnp.sum(w * gates, axis=0, keepdims=True))

    pos1, g1 = pick(first)
    pos2, g2 = pick(second)
    return pos1, pos2, g1, g2


def _segment_copies(lstart_ref, far_ref, ngrp_ref, base, make_copy):
    total = jnp.int32(0)
    for e in range(N_EXPERTS):
        n = ngrp_ref[base + e]
        near = lstart_ref[base + e]
        far = far_ref[base + e]

        def body(j, carry, near=near, far=far):
            make_copy(pl.multiple_of(near + j * MOE_ALIGN, MOE_ALIGN),
                      pl.multiple_of(far + j * MOE_ALIGN, MOE_ALIGN)).start()
            return carry

        lax.fori_loop(0, n, body, 0)
        total = total + n
    return total


def _wait_copies(count, make_copy):
    def body(j, carry):
        make_copy(0, 0).wait()
        return carry

    lax.fori_loop(0, count, body, 0)


def _dispatch_kernel(lstart_ref, dst_ref, ngrp_ref, tail_ref, tailn_ref, nused_ref, x_ref, prow_ref, xs_ref,
                     loc_ref, zero_ref, pending_ref, sems):
    t = pl.program_id(0)
    slot = t % 2
    base = t * N_EXPERTS
    tm = x_ref.shape[0]
    rl = loc_ref.shape[1]
    pos = prow_ref[...].astype(jnp.int32)
    r = lax.broadcasted_iota(jnp.int32, (rl, tm), 0)
    perm = jnp.logical_or(r == pos[0:1, :], r == pos[1:2, :]).astype(BF16)
    loc_ref[slot] = _dot(perm, x_ref[...].astype(BF16)).astype(BF16)

    def to_sorted(buf):
        def make(src_row, dst_row):
            return pltpu.make_async_copy(loc_ref.at[buf, pl.ds(src_row, MOE_ALIGN), :],
                                         xs_ref.at[pl.ds(dst_row, MOE_ALIGN), :], sems.at[buf])
        return make

    @pl.when(t > 0)
    def _():
        _wait_copies(pending_ref[0], to_sorted(1 - slot))

    n_copies = _segment_copies(lstart_ref, dst_ref, ngrp_ref, base, to_sorted(slot))
    pending_ref[0] = n_copies
    sem = sems.at[slot]

    @pl.when(t == pl.num_programs(0) - 1)
    def _():
        _wait_copies(n_copies, to_sorted(slot))
        zero_ref[...] = jnp.zeros_like(zero_ref)

        def zero_fill(src_row, dst_row):
            return pltpu.make_async_copy(zero_ref.at[pl.ds(0, MOE_ALIGN), :],
                                         xs_ref.at[pl.ds(dst_row, MOE_ALIGN), :], sem)

        total = jnp.int32(0)
        for e in range(N_EXPERTS):
            n = tailn_ref[e]
            start = tail_ref[e]

            def body(j, carry, start=start):
                zero_fill(0, pl.multiple_of(start + j * MOE_ALIGN, MOE_ALIGN)).start()
                return carry

            lax.fori_loop(0, n, body, 0)
            total = total + n
        _wait_copies(total, zero_fill)

        rt = zero_ref.shape[0]

        def zero_tile(src_row, dst_row):
            return pltpu.make_async_copy(zero_ref, xs_ref.at[pl.ds(dst_row, rt), :], sem)

        n_unused = xs_ref.shape[0] // rt - nused_ref[0]

        def tile_body(j, carry):
            zero_tile(0, pl.multiple_of((nused_ref[0] + j) * rt, rt)).start()
            return carry

        lax.fori_loop(0, n_unused, tile_body, 0)
        _wait_copies(n_unused, zero_tile)


def _experts_kernel(te_ref, nused_ref, xs_ref, wup_ref, wdn_ref, ys_ref):
    d_exp = wdn_ref.shape[2]
    used = pl.program_id(0) < nused_ref[0]

    @pl.when(used)
    def _():
        h = _dot(xs_ref[...], wup_ref[0, 0].astype(BF16))
        act = jax.nn.silu(h[:, :d_exp]) * h[:, d_exp:]
        ys_ref[...] = _dot(act.astype(BF16), wdn_ref[0, 0].astype(BF16)).astype(ys_ref.dtype)

    @pl.when(jnp.logical_not(used))
    def _():
        ys_ref[...] = jnp.zeros_like(ys_ref)


def _combine_kernel(lstart_ref, dst_ref, ngrp_ref, x_ref, pcol_ref, pnext_ref, ys_ref, g_ref, b_ref, *rest,
                    n_prompt_tiles, split):
    if split:
        op_ref, os_ref, loc_ref, sel_ref, pending_ref, sems = rest
    else:
        o_ref, loc_ref, sel_ref, pending_ref, sems = rest
    t = pl.program_id(0)
    base = t * N_EXPERTS
    tm = x_ref.shape[0]
    rl = loc_ref.shape[1]

    def from_sorted(buf):
        def make(dst_row, src_row):
            return pltpu.make_async_copy(ys_ref.at[pl.ds(src_row, MOE_ALIGN), :],
                                         loc_ref.at[buf, pl.ds(dst_row, MOE_ALIGN), :], sems.at[buf])
        return make

    def build_sel(cols, buf):
        lane = lax.broadcasted_iota(jnp.int32, (tm, rl), 1)
        sel_ref[buf] = (jnp.where(lane == cols[:, 0:1].astype(jnp.int32), cols[:, 2:3], 0.0)
                        + jnp.where(lane == cols[:, 1:2].astype(jnp.int32), cols[:, 3:4], 0.0)).astype(BF16)

    @pl.when(t == 0)
    def _():
        loc_ref[...] = jnp.zeros_like(loc_ref)
        pending_ref[0] = _segment_copies(lstart_ref, dst_ref, ngrp_ref, base, from_sorted(0))
        build_sel(pcol_ref[...], 0)

    def step(cur):
        nxt = 1 - cur

        @pl.when(t + 1 < pl.num_programs(0))
        def _():
            pending_ref[nxt] = _segment_copies(lstart_ref, dst_ref, ngrp_ref, base + N_EXPERTS,
                                               from_sorted(nxt))

        _wait_copies(pending_ref[cur], from_sorted(cur))
        build_sel(pnext_ref[...], nxt)
        x = x_ref[...]
        y = _layer_norm(ALPHA * x + _dot(sel_ref[cur], loc_ref[cur]), g_ref[...], b_ref[...])
        if split:
            @pl.when(t < n_prompt_tiles)
            def _():
                op_ref[...] = y

            @pl.when(t >= n_prompt_tiles)
            def _():
                os_ref[...] = y
        else:
            o_ref[...] = y

    for parity in range(2):
        pl.when(t % 2 == parity)(functools.partial(step, parity))


def _moe_layer(x, rwt, rbias, w_up, w_down, ln_g, ln_b, *, layer, n_prompt, split):
    n, d = x.shape
    tm = MOE_TOKEN_TILE
    n_tiles = n // tm
    d_exp = w_down.shape[2]
    n_prompt_tiles = n_prompt // tm
    n_e, al, rt = N_EXPERTS, MOE_ALIGN, MOE_ROW_TILE
    rl = _moe_local_rows(tm)
    n_row_tiles = -(-(2 * n + n_tiles * n_e * (al - 1) + n_e * (rt - al)) // rt)

    pos_rows, pos_cols, cnt_raw = pl.pallas_call(
        _route_kernel,
        grid=(n_tiles,),
        in_specs=[pl.BlockSpec((tm, d), lambda i: (i, 0)),
                  pl.BlockSpec((n_e, d), lambda i: (0, 0)),
                  pl.BlockSpec((n_e, 1), lambda i: (0, 0))],
        out_specs=[pl.BlockSpec((SUBLANES, tm), lambda i: (0, i)),
                   pl.BlockSpec((tm, LANES), lambda i: (i, 0)),
                   pl.BlockSpec((n_e, LANES), lambda i: (i, 0))],
        out_shape=[jax.ShapeDtypeStruct((SUBLANES, n), F32), jax.ShapeDtypeStruct((n, LANES), F32),
                   jax.ShapeDtypeStruct((n_tiles * n_e, LANES), jnp.int32)],
        compiler_params=_params(("parallel",)),
        name="moe_route",
    )(x, rwt, rbias)

    cnt = cnt_raw[:, 0].reshape(n_tiles, n_e)
    pad = (cnt + (al - 1)) // al * al
    lstart = jnp.cumsum(pad, axis=1) - pad
    tot = jnp.sum(pad, axis=0)
    cap = (tot + (rt - 1)) // rt * rt
    cend = jnp.cumsum(cap)
    dst = (cend - cap)[None, :] + jnp.cumsum(pad, axis=0) - pad
    n_used = (cend[-1] // rt).astype(jnp.int32)
    tile_id = jnp.arange(n_row_tiles, dtype=jnp.int32)
    tile_expert = jnp.sum((tile_id[:, None] * rt >= cend[None, :]).astype(jnp.int32), axis=1)
    tile_expert = jnp.take(tile_expert, jnp.minimum(tile_id, n_used - 1))
    i32 = lambda a: a.reshape(-1).astype(jnp.int32)
    lstart, dst, ngrp = i32(lstart), i32(dst), i32(pad // al)
    tail, tailn = i32(cend - cap + tot), i32((cap - tot) // al)

    xs = pl.pallas_call(
        _dispatch_kernel,
        grid_spec=pltpu.PrefetchScalarGridSpec(
            num_scalar_prefetch=6,
            grid=(n_tiles,),
            in_specs=[pl.BlockSpec((tm, d), lambda i, *_: (i, 0)),
                      pl.BlockSpec((SUBLANES, tm), lambda i, *_: (0, i))],
            out_specs=pl.BlockSpec(memory_space=pl.ANY),
            scratch_shapes=[pltpu.VMEM((2, rl, d), BF16), pltpu.VMEM((rt, d), BF16),
                            pltpu.SMEM((1,), jnp.int32), pltpu.SemaphoreType.DMA((2,))],
        ),
        out_shape=jax.ShapeDtypeStruct((n_row_tiles * rt, d), BF16),
        compiler_params=_params(("arbitrary",)),
        name="moe_dispatch",
    )(lstart, dst, ngrp, tail, tailn, n_used.reshape(1), x, pos_rows)

    last_used = lambda i, te, nu: (jnp.minimum(i, nu[0] - 1), 0)
    ys = pl.pallas_call(
        _experts_kernel,
        grid_spec=pltpu.PrefetchScalarGridSpec(
            num_scalar_prefetch=2,
            grid=(n_row_tiles,),
            in_specs=[pl.BlockSpec((rt, d), last_used),
                      pl.BlockSpec((1, 1, d, 2 * d_exp), lambda i, te, nu: (layer, te[i], 0, 0)),
                      pl.BlockSpec((1, 1, d_exp, d), lambda i, te, nu: (layer, te[i], 0, 0))],
            out_specs=pl.BlockSpec((rt, d), lambda i, te, nu: (i, 0)),
        ),
        out_shape=jax.ShapeDtypeStruct((n_row_tiles * rt, d), BF16),
        compiler_params=_params(("arbitrary",)),
        name="moe_experts",
    )(tile_expert, n_used.reshape(1), xs, w_up, w_down)

    if split:
        lo, hi = _lo_index(n_prompt_tiles), _hi_index(n_prompt_tiles)
        out_specs = [pl.BlockSpec((tm, d), lambda i, *_: lo(i)), pl.BlockSpec((tm, d), lambda i, *_: hi(i))]
        out_shape = [jax.ShapeDtypeStruct((n_prompt, d), F32), jax.ShapeDtypeStruct((n - n_prompt, d), F32)]
    else:
        out_specs = pl.BlockSpec((tm, d), lambda i, *_: (i, 0))
        out_shape = jax.ShapeDtypeStruct((n, d), F32)
    return pl.pallas_call(
        functools.partial(_combine_kernel, n_prompt_tiles=n_prompt_tiles, split=split),
        grid_spec=pltpu.PrefetchScalarGridSpec(
            num_scalar_prefetch=3,
            grid=(n_tiles,),
            in_specs=[pl.BlockSpec((tm, d), lambda i, *_: (i, 0)),
                      pl.BlockSpec((tm, LANES), lambda i, *_: (i, 0)),
                      pl.BlockSpec((tm, LANES), lambda i, *_: (jnp.minimum(i + 1, n_tiles - 1), 0)),
                      pl.BlockSpec(memory_space=pl.ANY),
                      pl.BlockSpec((1, d), lambda i, *_: (0, 0)),
                      pl.BlockSpec((1, d), lambda i, *_: (0, 0))],
            out_specs=out_specs,
            scratch_shapes=[pltpu.VMEM((2, rl, d), BF16), pltpu.VMEM((2, tm, rl), BF16),
                            pltpu.SMEM((2,), jnp.int32), pltpu.SemaphoreType.DMA((2,))],
        ),
        out_shape=out_shape,
        compiler_params=_params(("arbitrary",)),
        name="moe_combine",
    )(lstart, dst, ngrp, x, pos_cols, pos_cols, ys, ln_g[None], ln_b[None])


def _kvq_kernel(x_ref, wk_ref, wv_ref, wq_ref, kp_ref, vp_ref, ks_ref, vs_ref, kb_ref, vb_ref, qb_ref,
                *, n_prompt_tiles):
    i = pl.program_id(0)
    xb = x_ref[...].astype(BF16)
    k = _dot(xb, wk_ref[...])
    v = _dot(xb, wv_ref[...])
    tm = k.shape[0]
    vd = k.shape[1] // N_HEADS

    def store_heads(ref, val):
        for h in range(N_HEADS):
            ref[pl.ds(h, tm, stride=N_HEADS), :] = val[:, h * vd:(h + 1) * vd]

    @pl.when(i < n_prompt_tiles)
    def _():
        store_heads(kp_ref, k)
        store_heads(vp_ref, v)
        kb_ref[...] = k.astype(BF16)
        vb_ref[...] = v.astype(BF16)
        qb_ref[...] = (_dot(xb, wq_ref[...]) * Q_SCALE).astype(BF16)

    @pl.when(i >= n_prompt_tiles)
    def _():
        store_heads(ks_ref, k)
        store_heads(vs_ref, v)


def _kvq_proj(x, w_k, w_v, w_q, *, n_prompt):
    n, d = x.shape
    tm = TOKEN_TILE
    n_prompt_tiles = n_prompt // tm
    n_sample = n - n_prompt
    vd = d // N_HEADS
    lo = pl.BlockSpec((tm, d), _lo_index(n_prompt_tiles))
    lo_heads = pl.BlockSpec((tm * N_HEADS, vd), _lo_index(n_prompt_tiles))
    hi_heads = pl.BlockSpec((tm * N_HEADS, vd), _hi_index(n_prompt_tiles))
    wspec = pl.BlockSpec((d, d), lambda i: (0, 0))
    return pl.pallas_call(
        functools.partial(_kvq_kernel, n_prompt_tiles=n_prompt_tiles),
        grid=(n // tm,),
        in_specs=[pl.BlockSpec((tm, d), lambda i: (i, 0)), wspec, wspec, wspec],
        out_specs=[lo_heads, lo_heads, hi_heads, hi_heads, lo, lo, lo],
        out_shape=[jax.ShapeDtypeStruct((n_prompt * N_HEADS, vd), F32),
                   jax.ShapeDtypeStruct((n_prompt * N_HEADS, vd), F32),
                   jax.ShapeDtypeStruct((n_sample * N_HEADS, vd), F32),
                   jax.ShapeDtypeStruct((n_sample * N_HEADS, vd), F32),
                   jax.ShapeDtypeStruct((n_prompt, d), BF16), jax.ShapeDtypeStruct((n_prompt, d), BF16),
                   jax.ShapeDtypeStruct((n_prompt, d), BF16)],
        compiler_params=_params(("arbitrary",)),
        name="kvq_proj",
    )(x, w_k.astype(BF16), w_v.astype(BF16), w_q.astype(BF16))


def _q_kernel(x_ref, wq_ref, qb_ref):
    qb_ref[...] = (_dot(x_ref[...].astype(BF16), wq_ref[...]) * Q_SCALE).astype(BF16)


def _q_proj(x, w_q, n_rows):
    d = x.shape[1]
    tm = TOKEN_TILE
    row = pl.BlockSpec((tm, d), lambda i: (i, 0))
    return pl.pallas_call(
        _q_kernel,
        grid=(n_rows // tm,),
        in_specs=[row, pl.BlockSpec((d, d), lambda i: (0, 0))],
        out_specs=row,
        out_shape=jax.ShapeDtypeStruct((n_rows, d), BF16),
        compiler_params=_params(("parallel",)),
        name="q_proj",
    )(x, w_q.astype(BF16))


def _diff_lambda(lam_ref, lam_init):
    lv = lam_ref[...]
    e1 = jnp.exp(jnp.sum(lv[0:1, :] * lv[1:2, :], axis=1, keepdims=True))
    e2 = jnp.exp(jnp.sum(lv[2:3, :] * lv[3:4, :], axis=1, keepdims=True))
    return e1 - e2 + lam_init


def _lane_tile(stat, n):
    return stat if n == LANES else jnp.concatenate([stat] * (n // LANES), axis=1)


def _softmax_step(s, vc, m_ref, l_ref, acc_ref, rows=None):
    sl = slice(None) if rows is None else rows
    m_old = m_ref[sl, :]
    m_new = jnp.maximum(m_old, jnp.max(s, axis=1, keepdims=True))
    alpha = jnp.exp2(m_old - m_new)
    p = jnp.exp2(s - _lane_tile(m_new, s.shape[1]))
    l_ref[sl, :] = alpha * l_ref[sl, :] + jnp.sum(p, axis=1, keepdims=True)
    acc_ref[sl, :] = alpha * acc_ref[sl, :] + _dot(p.astype(BF16), vc)
    m_ref[sl, :] = m_new


def _head_norm(o, g, lam_init):
    return o * lax.rsqrt(jnp.mean(o * o, axis=-1, keepdims=True) + LN_EPS) * g * (1.0 - lam_init)


def _pattn_kernel(lam_ref, g_ref, q_ref, k_ref, v_ref, o_ref,
                  m1_ref, l1_ref, a1_ref, m2_ref, l2_ref, a2_ref, *, lam_init):
    qi = pl.program_id(2)
    tq = q_ref.shape[0]
    tk = ATTN_K_TILE
    per_tile = tq // tk
    hd = HEAD_DIM
    q = q_ref[...]
    q1, q2 = q[:, :hd], q[:, hd:]
    for ref, init in ((m1_ref, NEG), (l1_ref, 0.0), (a1_ref, 0.0), (m2_ref, NEG), (l2_ref, 0.0), (a2_ref, 0.0)):
        ref[...] = jnp.full_like(ref, init)

    def chunk(start, first_row=None):
        kc = k_ref[pl.ds(start, tk), :]
        vc = v_ref[pl.ds(start, tk), :]
        rows = slice(first_row or 0, tq)
        s1 = _dot_nt(q1[rows], kc[:, :hd])
        s2 = _dot_nt(q2[rows], kc[:, hd:])
        if first_row is not None:
            causal = (lax.broadcasted_iota(jnp.int32, s1.shape, 0) >= lax.broadcasted_iota(jnp.int32, s1.shape, 1))
            s1 = jnp.where(causal, s1, NEG)
            s2 = jnp.where(causal, s2, NEG)
        _softmax_step(s1, vc, m1_ref, l1_ref, a1_ref, rows)
        _softmax_step(s2, vc, m2_ref, l2_ref, a2_ref, rows)

    def body(j, carry):
        chunk(pl.multiple_of(j * tk, tk))
        return carry

    lax.fori_loop(0, qi * per_tile, body, 0)
    for jj in range(per_tile):
        chunk(pl.multiple_of(qi * tq + jj * tk, tk), first_row=jj * tk)

    lam = _diff_lambda(lam_ref, lam_init)
    o = a1_ref[...] / l1_ref[...] - lam * (a2_ref[...] / l2_ref[...])
    o_ref[...] = _head_norm(o, g_ref[...], lam_init).astype(o_ref.dtype)


def _prompt_attention(qb, kb, vb, lam_rows, subln_g, lam_init, *, batch, seq):
    d = qb.shape[1]
    vd = d // N_HEADS
    tq = ATTN_Q_TILE
    nq = seq // tq
    stat = pltpu.VMEM((tq, LANES), F32)
    acc = pltpu.VMEM((tq, vd), F32)
    return pl.pallas_call(
        functools.partial(_pattn_kernel, lam_init=lam_init),
        grid=(batch, N_HEADS, nq),
        in_specs=[
            pl.BlockSpec((4, HEAD_DIM), lambda b, h, i: (0, 0)),
            pl.BlockSpec((1, vd), lambda b, h, i: (0, 0)),
            pl.BlockSpec((tq, vd), lambda b, h, i: (b * nq + i, h)),
            pl.BlockSpec((seq, vd), lambda b, h, i: (b, h)),
            pl.BlockSpec((seq, vd), lambda b, h, i: (b, h)),
        ],
        out_specs=pl.BlockSpec((tq, vd), lambda b, h, i: (b * nq + i, h)),
        out_shape=jax.ShapeDtypeStruct((batch * seq, d), BF16),
        scratch_shapes=[stat, stat, acc, stat, stat, acc],
        compiler_params=_params(("parallel", "parallel", "arbitrary")),
        name="prompt_attention",
    )(lam_rows, subln_g[None], qb, kb, vb)


def _sattn_kernel(pt_ref, lam_ref, g_ref, x_ref, wq_ref, kn_ref, vn_ref, *rest, lam_init, dec_seq):
    npg = SAMPLE_SEQS_PER_STEP * SAMPLE_PAGES_PER_STEP
    k_refs = rest[:npg]
    v_refs = rest[npg:2 * npg]
    o_ref = rest[2 * npg]
    qbd_ref, m_ref, l_ref, acc_ref = rest[2 * npg + 1:]
    c = pl.program_id(1)
    n_chunks = pl.num_programs(1)
    n_rows = SAMPLE_SEQS_PER_STEP * dec_seq
    vd = qbd_ref.shape[1]
    hrows = 2 * n_rows
    half = N_HEADS // 2
    prow = 2 * hrows

    def first_row(h):
        return (h % half) * prow + (h // half) * hrows

    @pl.when(c == 0)
    def _():
        q = _dot(x_ref[...].astype(BF16), wq_ref[...]) * Q_SCALE
        lane = lax.broadcasted_iota(jnp.int32, (n_rows, vd), 1)
        parts = []
        for h in sorted(range(N_HEADS), key=first_row):
            qh = q[:, h * vd:(h + 1) * vd]
            parts += [jnp.where(lane < HEAD_DIM, qh, 0.0), jnp.where(lane >= HEAD_DIM, qh, 0.0)]
        qbd_ref[...] = jnp.concatenate(parts, axis=0).astype(BF16)
        m_ref[...] = jnp.full_like(m_ref, NEG)
        l_ref[...] = jnp.zeros_like(l_ref)
        acc_ref[...] = jnp.zeros_like(acc_ref)

    keys_per_seq = SAMPLE_PAGES_PER_STEP * PAGE_SIZE
    n_cols = 2 * SAMPLE_SEQS_PER_STEP * keys_per_seq
    n_score_rows = N_HEADS * hrows
    rr = lax.broadcasted_iota(jnp.int32, (n_score_rows, n_cols), 0)
    cc = lax.broadcasted_iota(jnp.int32, (n_score_rows, n_cols), 1)
    valid = jnp.logical_and((rr // hrows) % 2 == cc % 2,
                            (rr % n_rows) // dec_seq == cc // (2 * keys_per_seq))

    def pair_rows(refs, j):
        both = pl.ds(j, 2 * PAGE_SIZE, stride=half)
        return jnp.concatenate([r.at[0][both, :] for r in refs], axis=0).astype(BF16)

    s = jnp.concatenate([_dot_nt(qbd_ref[j * prow:(j + 1) * prow, :], pair_rows(k_refs, j))
                         for j in range(half)], axis=0)
    s = jnp.where(valid, s, NEG)
    m_old = m_ref[...]
    m_new = jnp.maximum(m_old, jnp.max(s, axis=1, keepdims=True))
    alpha = jnp.exp2(m_old - m_new)
    p = jnp.exp2(s - _lane_tile(m_new, n_cols))
    l_ref[...] = alpha * l_ref[...] + jnp.sum(p, axis=1, keepdims=True)
    pb = p.astype(BF16)
    pv = jnp.concatenate([_dot(pb[j * prow:(j + 1) * prow, :], pair_rows(v_refs, j))
                          for j in range(half)], axis=0)
    acc_ref[...] = alpha * acc_ref[...] + pv
    m_ref[...] = m_new

    @pl.when(c == n_chunks - 1)
    def _():
        lam = _diff_lambda(lam_ref, lam_init)
        r2 = lax.broadcasted_iota(jnp.int32, (hrows, LANES), 0) % n_rows
        k2 = lax.broadcasted_iota(jnp.int32, (hrows, LANES), 1)
        ok = (k2 < n_rows) & (k2 // dec_seq == r2 // dec_seq) & (k2 % dec_seq <= r2 % dec_seq)
        pad = jnp.zeros((LANES - n_rows, vd), F32)
        outs = []
        for h in range(N_HEADS):
            new_head = pl.ds(h, n_rows, stride=N_HEADS)
            kn = jnp.concatenate([kn_ref[new_head, :], pad], axis=0).astype(BF16)
            vn = jnp.concatenate([vn_ref[new_head, :], pad], axis=0).astype(BF16)
            rows = slice(first_row(h), first_row(h) + hrows)
            sn = jnp.where(ok, _dot_nt(qbd_ref[rows, :], kn), NEG)
            _softmax_step(sn, vn, m_ref, l_ref, acc_ref, rows)
            o = acc_ref[rows, :] / l_ref[rows, :]
            outs.append(_head_norm(o[:n_rows] - lam * o[n_rows:], g_ref[...], lam_init))
        o_ref[...] = jnp.concatenate(outs, axis=1)


def _sample_attention(x, w_q, k_new, v_new, cache_k, cache_v, page_table, lam_rows, subln_g, lam_init,
                      *, n_prompt, dec_seq):
    n, d = x.shape
    vd = d // N_HEADS
    n_seq, n_pg = page_table.shape
    n_rows = SAMPLE_SEQS_PER_STEP * dec_seq
    n_cols = 2 * N_HEADS * n_rows
    pps = SAMPLE_PAGES_PER_STEP
    slab = PAGE_SIZE * N_HEADS
    ck = cache_k.reshape(cache_k.shape[0], slab, vd)
    cv = cache_v.reshape(cache_v.shape[0], slab, vd)
    pt = page_table.reshape(-1)
    base = n_prompt // n_rows

    def page_spec(s, j):
        return pl.BlockSpec((1, slab, vd),
                            lambda b, c, pt: (pt[(b * SAMPLE_SEQS_PER_STEP + s) * n_pg + c * pps + j], 0, 0))

    page_specs = [page_spec(s, j) for s in range(SAMPLE_SEQS_PER_STEP) for j in range(pps)]
    new_rows = pl.BlockSpec((n_rows * N_HEADS, vd), lambda b, c, pt: (b, 0))
    stat = pltpu.VMEM((n_cols, LANES), F32)
    grid_spec = pltpu.PrefetchScalarGridSpec(
        num_scalar_prefetch=1,
        grid=(n_seq // SAMPLE_SEQS_PER_STEP, n_pg // pps),
        in_specs=[
            pl.BlockSpec((4, HEAD_DIM), lambda b, c, pt: (0, 0)),
            pl.BlockSpec((1, vd), lambda b, c, pt: (0, 0)),
            pl.BlockSpec((n_rows, d), lambda b, c, pt: (base + b, 0)),
            pl.BlockSpec((d, d), lambda b, c, pt: (0, 0)),
            new_rows,
            new_rows,
        ] + page_specs + page_specs,
        out_specs=pl.BlockSpec((n_rows, d), lambda b, c, pt: (b, 0)),
        scratch_shapes=[pltpu.VMEM((n_cols, vd), BF16), stat, stat, pltpu.VMEM((n_cols, vd), F32)],
    )
    n_pages_step = SAMPLE_SEQS_PER_STEP * pps
    return pl.pallas_call(
        functools.partial(_sattn_kernel, lam_init=lam_init, dec_seq=dec_seq),
        grid_spec=grid_spec,
        out_shape=jax.ShapeDtypeStruct((n - n_prompt, d), F32),
        compiler_params=_params(("parallel", "arbitrary")),
        name="sample_attention",
    )(pt, lam_rows, subln_g[None], x, w_q.astype(BF16), k_new, v_new,
      *([ck] * n_pages_step), *([cv] * n_pages_step))


def _oproj_kernel(ap_ref, as_ref, w_ref, x_ref, g_ref, b_ref, o_ref, *, n_prompt_tiles):
    i = pl.program_id(0)
    a = jnp.where(i >= n_prompt_tiles, as_ref[...].astype(BF16), ap_ref[...])
    m = _dot(a, w_ref[...])
    o_ref[...] = _layer_norm(ALPHA * x_ref[...] + m, g_ref[...], b_ref[...])


def _oproj_layer(a_prompt, a_sample, w_o, x, ln_g, ln_b):
    n, d = x.shape
    tm = TOKEN_TILE
    n_prompt_tiles = a_prompt.shape[0] // tm
    row = pl.BlockSpec((tm, d), lambda i: (i, 0))
    const2 = lambda i: (0, 0)
    kern = functools.partial(_oproj_kernel, n_prompt_tiles=n_prompt_tiles)
    return pl.pallas_call(
        kern,
        grid=(n // tm,),
        in_specs=[
            pl.BlockSpec((tm, d), _lo_index(n_prompt_tiles)),
            pl.BlockSpec((tm, d), _hi_index(n_prompt_tiles)),
            pl.BlockSpec((d, d), const2),
            row,
            pl.BlockSpec((1, d), const2),
            pl.BlockSpec((1, d), const2),
        ],
        out_specs=row,
        out_shape=jax.ShapeDtypeStruct((n, d), F32),
        compiler_params=_params(("parallel",)),
        name="attn_out_proj",
    )(a_prompt, a_sample, w_o.astype(BF16), x, ln_g[None], ln_b[None])


def kernel(x_prompt, x_sample, cache_k, cache_v, page_table, ln1_g, ln1_b, ln2_g, ln2_b, a_w_in, a_b_in, a_lnv_g, a_lnv_b, a_w_s, a_b_s, a_w_out, w_k, w_v, b_w_q, b_lam_q1, b_lam_k1, b_lam_q2, b_lam_k2, b_subln_g, b_w_o, router_w, router_bias, moe_w_up, moe_w_down):
    batch, seq, d = x_prompt.shape
    dec_batch, dec_seq, _ = x_sample.shape
    n_prompt = batch * seq
    n_sample = dec_batch * dec_seq
    tm = TOKEN_TILE
    assert seq % CHUNK == 0 and CHUNK % dec_seq == 0 and dec_seq <= CHUNK
    assert n_prompt % tm == 0 and n_sample % tm == 0
    assert seq % ATTN_Q_TILE == 0 and ATTN_Q_TILE % ATTN_K_TILE == 0
    assert a_w_s.shape[-1] == CHUNK and cache_k.shape[1] == PAGE_SIZE and cache_k.shape[2] == N_HEADS
    assert dec_batch % SAMPLE_SEQS_PER_STEP == 0 and page_table.shape[1] % SAMPLE_PAGES_PER_STEP == 0
    assert SAMPLE_SEQS_PER_STEP * dec_seq == SUBLANES

    n_prompt_tiles = n_prompt // tm
    rwt = router_w.T.astype(BF16)
    rbias = router_bias.astype(F32)[:, None]
    chunk_v = []
    x = None
    for l in range(DEPTH):
        if l < N_A_LAYERS:
            if l == 0:
                xa, xb = x_prompt.reshape(n_prompt, d), x_sample.reshape(n_sample, d)
                ia, ib = _lo_index(n_prompt_tiles), _hi_index(n_prompt_tiles)
            else:
                xa, xb = x, x
                ia, ib = _lo_index(n_prompt_tiles), (lambda i: (jnp.maximum(i, n_prompt_tiles), 0))
            x, v_rows = _gmlp_layer(xa, xb, ia, ib, a_w_in[l], a_b_in[l], a_lnv_g[l], a_lnv_b[l], a_w_s[l],
                                    a_b_s[l], a_w_out[l], ln1_g[l], ln1_b[l],
                                    n_prompt=n_prompt, n_sample=n_sample, dec_seq=dec_seq)
            chunk_v.append(v_rows.reshape(dec_batch, dec_seq, -1))
        else:
            j = l - N_A_LAYERS
            lam_init = 0.8 - 0.6 * math.exp(-0.3 * l)
            if j == 0:
                kp, vp, ks, vs, kb, vb, qb = _kvq_proj(x, w_k, w_v, b_w_q[j], n_prompt=n_prompt)
            else:
                qb = _q_proj(x, b_w_q[j], n_prompt)
            lam_rows = jnp.stack([b_lam_q1[j], b_lam_k1[j], b_lam_q2[j], b_lam_k2[j]]).astype(F32)
            a_p = _prompt_attention(qb, kb, vb, lam_rows, b_subln_g[j], lam_init, batch=batch, seq=seq)
            a_s = _sample_attention(x, b_w_q[j], ks, vs, cache_k, cache_v, page_table, lam_rows,
                                    b_subln_g[j], lam_init, n_prompt=n_prompt, dec_seq=dec_seq)
            x = _oproj_layer(a_p, a_s, b_w_o[j], x, ln1_g[l], ln1_b[l])
        x = _moe_layer(x, rwt, rbias, moe_w_up, moe_w_down, ln2_g[l], ln2_b[l],
                       layer=l, n_prompt=n_prompt, split=(l == DEPTH - 1))

    yp, ys = x
    vd = d // N_HEADS
    return (yp.reshape(batch, seq, d), ys.reshape(dec_batch, dec_seq, d),
            kp.reshape(batch, seq, N_HEADS, vd), vp.reshape(batch, seq, N_HEADS, vd),
            ks.reshape(dec_batch, dec_seq, N_HEADS, vd), vs.reshape(dec_batch, dec_seq, N_HEADS, vd),
            jnp.stack(chunk_v))
```

```python
import functools
import math

import jax
import jax.numpy as jnp
from jax import lax
from jax.experimental import pallas as pl
from jax.experimental.pallas import tpu as pltpu

F32 = jnp.float32
BF16 = jnp.bfloat16

DEPTH = 4
N_A_LAYERS = DEPTH // 2
CHUNK = 128
A_GROUPS = 8
N_HEADS = 8
HEAD_DIM = 64
N_EXPERTS = 16
N_EXPERT_GROUPS = 4
EXPERTS_PER_GROUP = N_EXPERTS // N_EXPERT_GROUPS
ALPHA = (2.0 * DEPTH) ** 0.25
LN_EPS = 1e-5
NEG = -1e30
PAGE_SIZE = 128
Q_SCALE = HEAD_DIM ** -0.5 * math.log2(math.e)

LANES = 128
SUBLANES = 8
TOKEN_TILE = 512
ATTN_Q_TILE = 1024
ATTN_K_TILE = 512
SAMPLE_SEQS_PER_STEP = 2
SAMPLE_PAGES_PER_STEP = 8
VMEM_LIMIT = 56 * 1024 * 1024


def _params(semantics):
    return pltpu.CompilerParams(dimension_semantics=semantics, vmem_limit_bytes=VMEM_LIMIT)


def _layer_norm(x, g, b):
    mu = jnp.mean(x, axis=-1, keepdims=True)
    xc = x - mu
    var = jnp.mean(xc * xc, axis=-1, keepdims=True)
    return xc * lax.rsqrt(var + LN_EPS) * g + b


def _dot(a, b):
    return jnp.dot(a, b, preferred_element_type=F32)


def _dot_nt(a, b):
    return lax.dot_general(a, b, (((1,), (1,)), ((), ())), preferred_element_type=F32)


def _lo_index(n_prompt_tiles):
    return lambda i: (jnp.minimum(i, n_prompt_tiles - 1), 0)


def _hi_index(n_prompt_tiles):
    return lambda i: (jnp.maximum(i - n_prompt_tiles, 0), 0)


def _gmlp_kernel(xp_ref, xs_ref, win_ref, bin_ref, lvg_ref, lvb_ref, ws_ref, sb_ref,
                 wout_ref, g1_ref, b1_ref, xo_ref, v_ref, *, n_prompt_tiles, dec_seq):
    i = pl.program_id(0)
    is_sample = i >= n_prompt_tiles
    a_inner = wout_ref.shape[0]
    gd = a_inner // A_GROUPS
    tm = xp_ref.shape[0]

    x = jnp.where(is_sample, xs_ref[...], xp_ref[...])
    z = _dot(x.astype(BF16), win_ref[...]) + bin_ref[...]
    z = jax.nn.gelu(z)
    u = z[:, :a_inner]
    v = _layer_norm(z[:, a_inner:], lvg_ref[...], lvb_ref[...])

    @pl.when(is_sample)
    def _():
        v_ref[...] = v

    row = lax.broadcasted_iota(jnp.int32, (CHUNK, CHUNK), 0)
    col = lax.broadcasted_iota(jnp.int32, (CHUNK, CHUNK), 1)
    mask = jnp.logical_and(col <= row, jnp.logical_or(jnp.logical_not(is_sample),
                                                      (row // dec_seq) == (col // dec_seq)))
    vb = v.astype(BF16)
    kind = is_sample.astype(jnp.int32)
    bias = sb_ref[kind]
    s_rows = []
    mixes = [jnp.where(mask, ws_ref[kind, g], 0.0).astype(BF16) for g in range(A_GROUPS)]
    for c in range(tm // CHUNK):
        cols = []
        for g in range(A_GROUPS):
            blk = vb[c * CHUNK:(c + 1) * CHUNK, g * gd:(g + 1) * gd]
            cols.append(_dot(mixes[g], blk))
        s_rows.append(jnp.concatenate(cols, axis=1) + bias)
    s = jnp.concatenate(s_rows, axis=0)
    m = _dot((u * s).astype(BF16), wout_ref[...])
    xo_ref[...] = _layer_norm(ALPHA * x + m, g1_ref[...], b1_ref[...])


def _gmlp_layer(x_prompt, x_sample, prompt_index, sample_index, w_in, b_in, lnv_g, lnv_b, w_s, b_s, w_out,
                ln_g, ln_b, *, n_prompt, n_sample, dec_seq):
    d = x_prompt.shape[1]
    a_inner = w_out.shape[0]
    tm = TOKEN_TILE
    n = n_prompt + n_sample
    n_prompt_tiles = n_prompt // tm
    reps = CHUNK // dec_seq
    ws = jnp.stack([w_s, jnp.tile(w_s[:, :dec_seq, :dec_seq], (1, reps, reps))])
    gd = a_inner // A_GROUPS
    sb_p = jnp.repeat(b_s.T, gd, axis=1)
    sb_s = jnp.repeat(jnp.tile(b_s[:, :dec_seq].T, (reps, 1)), gd, axis=1)
    sb = jnp.stack([sb_p, sb_s])
    const2 = lambda i: (0, 0)
    const3 = lambda i: (0, 0, 0)
    const4 = lambda i: (0, 0, 0, 0)
    kern = functools.partial(_gmlp_kernel, n_prompt_tiles=n_prompt_tiles, dec_seq=dec_seq)
    return pl.pallas_call(
        kern,
        grid=(n // tm,),
        in_specs=[
            pl.BlockSpec((tm, d), prompt_index),
            pl.BlockSpec((tm, d), sample_index),
            pl.BlockSpec((d, 2 * a_inner), const2),
            pl.BlockSpec((1, 2 * a_inner), const2),
            pl.BlockSpec((1, a_inner), const2),
            pl.BlockSpec((1, a_inner), const2),
            pl.BlockSpec((2, A_GROUPS, CHUNK, CHUNK), const4),
            pl.BlockSpec((2, CHUNK, a_inner), const3),
            pl.BlockSpec((a_inner, d), const2),
            pl.BlockSpec((1, d), const2),
            pl.BlockSpec((1, d), const2),
        ],
        out_specs=[
            pl.BlockSpec((tm, d), lambda i: (i, 0)),
            pl.BlockSpec((tm, a_inner), _hi_index(n_prompt_tiles)),
        ],
        out_shape=[jax.ShapeDtypeStruct((n, d), F32), jax.ShapeDtypeStruct((n_sample, a_inner), F32)],
        compiler_params=_params(("arbitrary",)),
        name="gmlp_layer",
    )(x_prompt, x_sample, w_in.astype(BF16), b_in[None], lnv_g[None], lnv_b[None], ws, sb,
      w_out.astype(BF16), ln_g[None], ln_b[None])


def _route(x_bf, rwt_ref, rbias_ref):
    logits = _dot_nt(rwt_ref[...], x_bf)
    mx = jnp.max(logits, axis=0, keepdims=True)
    ex = jnp.exp(logits - mx)
    probs = ex / jnp.sum(ex, axis=0, keepdims=True)
    sel = probs + rbias_ref[...]
    p = [probs[e:e + 1, :] for e in range(N_EXPERTS)]
    s = [sel[e:e + 1, :] for e in range(N_EXPERTS)]
    gscore = []
    for g in range(N_EXPERT_GROUPS):
        a, b, c, d = s[4 * g:4 * g + 4]
        hi1, lo1 = jnp.maximum(a, b), jnp.minimum(a, b)
        hi2, lo2 = jnp.maximum(c, d), jnp.minimum(c, d)
        top1 = jnp.maximum(hi1, hi2)
        top2 = jnp.maximum(jnp.minimum(hi1, hi2), jnp.maximum(lo1, lo2))
        gscore.append(top1 + top2)
    best = gscore[0]
    gbest = jnp.zeros_like(best, dtype=jnp.int32)
    for g in range(1, N_EXPERT_GROUPS):
        better = gscore[g] > best
        best = jnp.where(better, gscore[g], best)
        gbest = jnp.where(better, g, gbest)
    gates = []
    for e in range(N_EXPERTS):
        g = e // EXPERTS_PER_GROUP
        rank = jnp.zeros_like(gbest)
        for f in range(EXPERTS_PER_GROUP * g, EXPERTS_PER_GROUP * (g + 1)):
            if f == e:
                continue
            beats = (s[f] >= s[e]) if f < e else (s[f] > s[e])
            rank = rank + beats.astype(jnp.int32)
        chosen = (gbest == g) & (rank < 2)
        gates.append(jnp.where(chosen, p[e], 0.0))
    denom = gates[0]
    for e in range(1, N_EXPERTS):
        denom = denom + gates[e]
    return [gt / denom for gt in gates]


MOE_ALIGN = 16
MOE_ROW_TILE = 512
MOE_TOKEN_TILE = 512


def _moe_local_rows(tm):
    return 2 * tm + N_EXPERTS * MOE_ALIGN


def _route_kernel(x_ref, rwt_ref, rbias_ref, prow_ref, pcol_ref, cnt_ref):
    gates = jnp.concatenate(_route(x_ref[...].astype(BF16), rwt_ref, rbias_ref), axis=0)
    tm = gates.shape[1]
    cnt = jnp.sum((gates > 0.0).astype(F32), axis=1, keepdims=True)
    cnt_i = cnt.astype(jnp.int32)
    cnt_ref[...] = jnp.broadcast_to(cnt_i, cnt_ref.shape)
    shift = MOE_ALIGN.bit_length() - 1
    padded = lax.shift_left(lax.shift_right_logical(cnt_i + (MOE_ALIGN - 1), shift), shift).astype(F32)
    starts, run = [], jnp.zeros((1, 1), F32)
    for e in range(N_EXPERTS):
        starts.append(run)
        run = run + padded[e:e + 1, :]
    pos1, pos2, g1, g2 = _local_positions(gates, jnp.concatenate(starts, axis=0))
    prow_ref[...] = jnp.concatenate([pos1, pos2, g1, g2, jnp.zeros((SUBLANES - 4, tm), F32)], axis=0)
    rows = jnp.concatenate([pos1, pos2, g1, g2, jnp.zeros((LANES - 4, tm), F32)], axis=0)
    pcol_ref[...] = rows.T


def _local_positions(gates, seg_start):
    n_e, t = gates.shape
    chosen = gates > 0.0
    ch_bf = chosen.astype(BF16)
    ta = lax.broadcasted_iota(jnp.int32, (t, t), 0)
    tb = lax.broadcasted_iota(jnp.int32, (t, t), 1)
    rank = _dot(ch_bf, (ta < tb).astype(BF16))
    ch = chosen.astype(F32)
    seen = jnp.zeros((1, t), F32)
    firsts = []
    for e in range(n_e):
        row = ch[e:e + 1, :]
        firsts.append(jnp.where(seen == 0.0, row, 0.0))
        seen = seen + row
    first = jnp.concatenate(firsts, axis=0)
    second = ch - first
    pos = seg_start + rank

    def pick(w):
        has = jnp.sum(w, axis=0, keepdims=True) > 0.0
        return (jnp.where(has, jnp.sum(w * pos, axis=0, keepdims=True), -1.0),
                jnp.sum(w * gates, axis=0, keepdims=True))

    pos1, g1 = pick(first)
    pos2, g2 = pick(second)
    return pos1, pos2, g1, g2


def _segment_copies(lstart_ref, far_ref, ngrp_ref, base, make_copy):
    total = jnp.int32(0)
    for e in range(N_EXPERTS):
        n = ngrp_ref[base + e]
        near = lstart_ref[base + e]
        far = far_ref[base + e]

        def body(j, carry, near=near, far=far):
            make_copy(pl.multiple_of(near + j * MOE_ALIGN, MOE_ALIGN),
                      pl.multiple_of(far + j * MOE_ALIGN, MOE_ALIGN)).start()
            return carry

        lax.fori_loop(0, n, body, 0)
        total = total + n
    return total


def _wait_copies(count, make_copy):
    def body(j, carry):
        make_copy(0, 0).wait()
        return carry

    lax.fori_loop(0, count, body, 0)


def _dispatch_kernel(lstart_ref, dst_ref, ngrp_ref, tail_ref, tailn_ref, nused_ref, x_ref, prow_ref, xs_ref,
                     loc_ref, zero_ref, pending_ref, sems):
    t = pl.program_id(0)
    slot = t % 2
    base = t * N_EXPERTS
    tm = x_ref.shape[0]
    rl = loc_ref.shape[1]
    pos = prow_ref[...].astype(jnp.int32)
    r = lax.broadcasted_iota(jnp.int32, (rl, tm), 0)
    perm = jnp.logical_or(r == pos[0:1, :], r == pos[1:2, :]).astype(BF16)
    loc_ref[slot] = _dot(perm, x_ref[...].astype(BF16)).astype(BF16)

    def to_sorted(buf):
        def make(src_row, dst_row):
            return pltpu.make_async_copy(loc_ref.at[buf, pl.ds(src_row, MOE_ALIGN), :],
                                         xs_ref.at[pl.ds(dst_row, MOE_ALIGN), :], sems.at[buf])
        return make

    @pl.when(t > 0)
    def _():
        _wait_copies(pending_ref[0], to_sorted(1 - slot))

    n_copies = _segment_copies(lstart_ref, dst_ref, ngrp_ref, base, to_sorted(slot))
    pending_ref[0] = n_copies
    sem = sems.at[slot]

    @pl.when(t == pl.num_programs(0) - 1)
    def _():
        _wait_copies(n_copies, to_sorted(slot))
        zero_ref[...] = jnp.zeros_like(zero_ref)

        def zero_fill(src_row, dst_row):
            return pltpu.make_async_copy(zero_ref.at[pl.ds(0, MOE_ALIGN), :],
                                         xs_ref.at[pl.ds(dst_row, MOE_ALIGN), :], sem)

        total = jnp.int32(0)
        for e in range(N_EXPERTS):
            n = tailn_ref[e]
            start = tail_ref[e]

            def body(j, carry, start=start):
                zero_fill(0, pl.multiple_of(start + j * MOE_ALIGN, MOE_ALIGN)).start()
                return carry

            lax.fori_loop(0, n, body, 0)
            total = total + n
        _wait_copies(total, zero_fill)

        rt = zero_ref.shape[0]

        def zero_tile(src_row, dst_row):
            return pltpu.make_async_copy(zero_ref, xs_ref.at[pl.ds(dst_row, rt), :], sem)

        n_unused = xs_ref.shape[0] // rt - nused_ref[0]

        def tile_body(j, carry):
            zero_tile(0, pl.multiple_of((nused_ref[0] + j) * rt, rt)).start()
            return carry

        lax.fori_loop(0, n_unused, tile_body, 0)
        _wait_copies(n_unused, zero_tile)


def _experts_kernel(te_ref, nused_ref, xs_ref, wup_ref, wdn_ref, ys_ref):
    d_exp = wdn_ref.shape[2]
    used = pl.program_id(0) < nused_ref[0]

    @pl.when(used)
    def _():
        h = _dot(xs_ref[...], wup_ref[0, 0].astype(BF16))
        act = jax.nn.silu(h[:, :d_exp]) * h[:, d_exp:]
        ys_ref[...] = _dot(act.astype(BF16), wdn_ref[0, 0].astype(BF16)).astype(ys_ref.dtype)

    @pl.when(jnp.logical_not(used))
    def _():
        ys_ref[...] = jnp.zeros_like(ys_ref)


def _combine_kernel(lstart_ref, dst_ref, ngrp_ref, x_ref, pcol_ref, pnext_ref, ys_ref, g_ref, b_ref, *rest,
                    n_prompt_tiles, split):
    if split:
        op_ref, os_ref, loc_ref, sel_ref, pending_ref, sems = rest
    else:
        o_ref, loc_ref, sel_ref, pending_ref, sems = rest
    t = pl.program_id(0)
    base = t * N_EXPERTS
    tm = x_ref.shape[0]
    rl = loc_ref.shape[1]

    def from_sorted(buf):
        def make(dst_row, src_row):
            return pltpu.make_async_copy(ys_ref.at[pl.ds(src_row, MOE_ALIGN), :],
                                         loc_ref.at[buf, pl.ds(dst_row, MOE_ALIGN), :], sems.at[buf])
        return make

    def build_sel(cols, buf):
        lane = lax.broadcasted_iota(jnp.int32, (tm, rl), 1)
        sel_ref[buf] = (jnp.where(lane == cols[:, 0:1].astype(jnp.int32), cols[:, 2:3], 0.0)
                        + jnp.where(lane == cols[:, 1:2].astype(jnp.int32), cols[:, 3:4], 0.0)).astype(BF16)

    @pl.when(t == 0)
    def _():
        loc_ref[...] = jnp.zeros_like(loc_ref)
        pending_ref[0] = _segment_copies(lstart_ref, dst_ref, ngrp_ref, base, from_sorted(0))
        build_sel(pcol_ref[...], 0)

    def step(cur):
        nxt = 1 - cur

        @pl.when(t + 1 < pl.num_programs(0))
        def _():
            pending_ref[nxt] = _segment_copies(lstart_ref, dst_ref, ngrp_ref, base + N_EXPERTS,
                                               from_sorted(nxt))

        _wait_copies(pending_ref[cur], from_sorted(cur))
        build_sel(pnext_ref[...], nxt)
        x = x_ref[...]
        y = _layer_norm(ALPHA * x + _dot(sel_ref[cur], loc_ref[cur]), g_ref[...], b_ref[...])
        if split:
            @pl.when(t < n_prompt_tiles)
            def _():
                op_ref[...] = y

            @pl.when(t >= n_prompt_tiles)
            def _():
                os_ref[...] = y
        else:
            o_ref[...] = y

    for parity in range(2):
        pl.when(t % 2 == parity)(functools.partial(step, parity))


def _moe_layer(x, rwt, rbias, w_up, w_down, ln_g, ln_b, *, layer, n_prompt, split):
    n, d = x.shape
    tm = MOE_TOKEN_TILE
    n_tiles = n // tm
    d_exp = w_down.shape[2]
    n_prompt_tiles = n_prompt // tm
    n_e, al, rt = N_EXPERTS, MOE_ALIGN, MOE_ROW_TILE
    rl = _moe_local_rows(tm)
    n_row_tiles = -(-(2 * n + n_tiles * n_e * (al - 1) + n_e * (rt - al)) // rt)

    pos_rows, pos_cols, cnt_raw = pl.pallas_call(
        _route_kernel,
        grid=(n_tiles,),
        in_specs=[pl.BlockSpec((tm, d), lambda i: (i, 0)),
                  pl.BlockSpec((n_e, d), lambda i: (0, 0)),
                  pl.BlockSpec((n_e, 1), lambda i: (0, 0))],
        out_specs=[pl.BlockSpec((SUBLANES, tm), lambda i: (0, i)),
                   pl.BlockSpec((tm, LANES), lambda i: (i, 0)),
                   pl.BlockSpec((n_e, LANES), lambda i: (i, 0))],
        out_shape=[jax.ShapeDtypeStruct((SUBLANES, n), F32), jax.ShapeDtypeStruct((n, LANES), F32),
                   jax.ShapeDtypeStruct((n_tiles * n_e, LANES), jnp.int32)],
        compiler_params=_params(("parallel",)),
        name="moe_route",
    )(x, rwt, rbias)

    cnt = cnt_raw[:, 0].reshape(n_tiles, n_e)
    pad = (cnt + (al - 1)) // al * al
    lstart = jnp.cumsum(pad, axis=1) - pad
    tot = jnp.sum(pad, axis=0)
    cap = (tot + (rt - 1)) // rt * rt
    cend = jnp.cumsum(cap)
    dst = (cend - cap)[None, :] + jnp.cumsum(pad, axis=0) - pad
    n_used = (cend[-1] // rt).astype(jnp.int32)
    tile_id = jnp.arange(n_row_tiles, dtype=jnp.int32)
    tile_expert = jnp.sum((tile_id[:, None] * rt >= cend[None, :]).astype(jnp.int32), axis=1)
    tile_expert = jnp.take(tile_expert, jnp.minimum(tile_id, n_used - 1))
    i32 = lambda a: a.reshape(-1).astype(jnp.int32)
    lstart, dst, ngrp = i32(lstart), i32(dst), i32(pad // al)
    tail, tailn = i32(cend - cap + tot), i32((cap - tot) // al)

    xs = pl.pallas_call(
        _dispatch_kernel,
        grid_spec=pltpu.PrefetchScalarGridSpec(
            num_scalar_prefetch=6,
            grid=(n_tiles,),
            in_specs=[pl.BlockSpec((tm, d), lambda i, *_: (i, 0)),
                      pl.BlockSpec((SUBLANES, tm), lambda i, *_: (0, i))],
            out_specs=pl.BlockSpec(memory_space=pl.ANY),
            scratch_shapes=[pltpu.VMEM((2, rl, d), BF16), pltpu.VMEM((rt, d), BF16),
                            pltpu.SMEM((1,), jnp.int32), pltpu.SemaphoreType.DMA((2,))],
        ),
        out_shape=jax.ShapeDtypeStruct((n_row_tiles * rt, d), BF16),
        compiler_params=_params(("arbitrary",)),
        name="moe_dispatch",
    )(lstart, dst, ngrp, tail, tailn, n_used.reshape(1), x, pos_rows)

    last_used = lambda i, te, nu: (jnp.minimum(i, nu[0] - 1), 0)
    ys = pl.pallas_call(
        _experts_kernel,
        grid_spec=pltpu.PrefetchScalarGridSpec(
            num_scalar_prefetch=2,
            grid=(n_row_tiles,),
            in_specs=[pl.BlockSpec((rt, d), last_used),
                      pl.BlockSpec((1, 1, d, 2 * d_exp), lambda i, te, nu: (layer, te[i], 0, 0)),
                      pl.BlockSpec((1, 1, d_exp, d), lambda i, te, nu: (layer, te[i], 0, 0))],
            out_specs=pl.BlockSpec((rt, d), lambda i, te, nu: (i, 0)),
        ),
        out_shape=jax.ShapeDtypeStruct((n_row_tiles * rt, d), BF16),
        compiler_params=_params(("arbitrary",)),
        name="moe_experts",
    )(tile_expert, n_used.reshape(1), xs, w_up, w_down)

    if split:
        lo, hi = _lo_index(n_prompt_tiles), _hi_index(n_prompt_tiles)
        out_specs = [pl.BlockSpec((tm, d), lambda i, *_: lo(i)), pl.BlockSpec((tm, d), lambda i, *_: hi(i))]
        out_shape = [jax.ShapeDtypeStruct((n_prompt, d), F32), jax.ShapeDtypeStruct((n - n_prompt, d), F32)]
    else:
        out_specs = pl.BlockSpec((tm, d), lambda i, *_: (i, 0))
        out_shape = jax.ShapeDtypeStruct((n, d), F32)
    return pl.pallas_call(
        functools.partial(_combine_kernel, n_prompt_tiles=n_prompt_tiles, split=split),
        grid_spec=pltpu.PrefetchScalarGridSpec(
            num_scalar_prefetch=3,
            grid=(n_tiles,),
            in_specs=[pl.BlockSpec((tm, d), lambda i, *_: (i, 0)),
                      pl.BlockSpec((tm, LANES), lambda i, *_: (i, 0)),
                      pl.BlockSpec((tm, LANES), lambda i, *_: (jnp.minimum(i + 1, n_tiles - 1), 0)),
                      pl.BlockSpec(memory_space=pl.ANY),
                      pl.BlockSpec((1, d), lambda i, *_: (0, 0)),
                      pl.BlockSpec((1, d), lambda i, *_: (0, 0))],
            out_specs=out_specs,
            scratch_shapes=[pltpu.VMEM((2, rl, d), BF16), pltpu.VMEM((2, tm, rl), BF16),
                            pltpu.SMEM((2,), jnp.int32), pltpu.SemaphoreType.DMA((2,))],
        ),
        out_shape=out_shape,
        compiler_params=_params(("arbitrary",)),
        name="moe_combine",
    )(lstart, dst, ngrp, x, pos_cols, pos_cols, ys, ln_g[None], ln_b[None])


def _kvq_kernel(x_ref, wk_ref, wv_ref, wq_ref, kp_ref, vp_ref, ks_ref, vs_ref, kb_ref, vb_ref, qb_ref,
                *, n_prompt_tiles):
    i = pl.program_id(0)
    xb = x_ref[...].astype(BF16)
    k = _dot(xb, wk_ref[...])
    v = _dot(xb, wv_ref[...])
    tm = k.shape[0]
    vd = k.shape[1] // N_HEADS

    def store_heads(ref, val):
        for h in range(N_HEADS):
            ref[pl.ds(h, tm, stride=N_HEADS), :] = val[:, h * vd:(h + 1) * vd]

    @pl.when(i < n_prompt_tiles)
    def _():
        store_heads(kp_ref, k)
        store_heads(vp_ref, v)
        kb_ref[...] = k.astype(BF16)
        vb_ref[...] = v.astype(BF16)
        qb_ref[...] = (_dot(xb, wq_ref[...]) * Q_SCALE).astype(BF16)

    @pl.when(i >= n_prompt_tiles)
    def _():
        store_heads(ks_ref, k)
        store_heads(vs_ref, v)


def _kvq_proj(x, w_k, w_v, w_q, *, n_prompt):
    n, d = x.shape
    tm = TOKEN_TILE
    n_prompt_tiles = n_prompt // tm
    n_sample = n - n_prompt
    vd = d // N_HEADS
    lo = pl.BlockSpec((tm, d), _lo_index(n_prompt_tiles))
    lo_heads = pl.BlockSpec((tm * N_HEADS, vd), _lo_index(n_prompt_tiles))
    hi_heads = pl.BlockSpec((tm * N_HEADS, vd), _hi_index(n_prompt_tiles))
    wspec = pl.BlockSpec((d, d), lambda i: (0, 0))
    return pl.pallas_call(
        functools.partial(_kvq_kernel, n_prompt_tiles=n_prompt_tiles),
        grid=(n // tm,),
        in_specs=[pl.BlockSpec((tm, d), lambda i: (i, 0)), wspec, wspec, wspec],
        out_specs=[lo_heads, lo_heads, hi_heads, hi_heads, lo, lo, lo],
        out_shape=[jax.ShapeDtypeStruct((n_prompt * N_HEADS, vd), F32),
                   jax.ShapeDtypeStruct((n_prompt * N_HEADS, vd), F32),
                   jax.ShapeDtypeStruct((n_sample * N_HEADS, vd), F32),
                   jax.ShapeDtypeStruct((n_sample * N_HEADS, vd), F32),
                   jax.ShapeDtypeStruct((n_prompt, d), BF16), jax.ShapeDtypeStruct((n_prompt, d), BF16),
                   jax.ShapeDtypeStruct((n_prompt, d), BF16)],
        compiler_params=_params(("arbitrary",)),
        name="kvq_proj",
    )(x, w_k.astype(BF16), w_v.astype(BF16), w_q.astype(BF16))


def _q_kernel(x_ref, wq_ref, qb_ref):
    qb_ref[...] = (_dot(x_ref[...].astype(BF16), wq_ref[...]) * Q_SCALE).astype(BF16)


def _q_proj(x, w_q, n_rows):
    d = x.shape[1]
    tm = TOKEN_TILE
    row = pl.BlockSpec((tm, d), lambda i: (i, 0))
    return pl.pallas_call(
        _q_kernel,
        grid=(n_rows // tm,),
        in_specs=[row, pl.BlockSpec((d, d), lambda i: (0, 0))],
        out_specs=row,
        out_shape=jax.ShapeDtypeStruct((n_rows, d), BF16),
        compiler_params=_params(("parallel",)),
        name="q_proj",
    )(x, w_q.astype(BF16))


def _diff_lambda(lam_ref, lam_init):
    lv = lam_ref[...]
    e1 = jnp.exp(jnp.sum(lv[0:1, :] * lv[1:2, :], axis=1, keepdims=True))
    e2 = jnp.exp(jnp.sum(lv[2:3, :] * lv[3:4, :], axis=1, keepdims=True))
    return e1 - e2 + lam_init


def _lane_tile(stat, n):
    return stat if n == LANES else jnp.concatenate([stat] * (n // LANES), axis=1)


def _softmax_step(s, vc, m_ref, l_ref, acc_ref, rows=None):
    sl = slice(None) if rows is None else rows
    m_old = m_ref[sl, :]
    m_new = jnp.maximum(m_old, jnp.max(s, axis=1, keepdims=True))
    alpha = jnp.exp2(m_old - m_new)
    p = jnp.exp2(s - _lane_tile(m_new, s.shape[1]))
    l_ref[sl, :] = alpha * l_ref[sl, :] + jnp.sum(p, axis=1, keepdims=True)
    acc_ref[sl, :] = alpha * acc_ref[sl, :] + _dot(p.astype(BF16), vc)
    m_ref[sl, :] = m_new


def _head_norm(o, g, lam_init):
    return o * lax.rsqrt(jnp.mean(o * o, axis=-1, keepdims=True) + LN_EPS) * g * (1.0 - lam_init)


def _pattn_kernel(lam_ref, g_ref, q_ref, k_ref, v_ref, o_ref,
                  m1_ref, l1_ref, a1_ref, m2_ref, l2_ref, a2_ref, *, lam_init):
    qi = pl.program_id(2)
    tq = q_ref.shape[0]
    tk = ATTN_K_TILE
    per_tile = tq // tk
    hd = HEAD_DIM
    q = q_ref[...]
    q1, q2 = q[:, :hd], q[:, hd:]
    for ref, init in ((m1_ref, NEG), (l1_ref, 0.0), (a1_ref, 0.0), (m2_ref, NEG), (l2_ref, 0.0), (a2_ref, 0.0)):
        ref[...] = jnp.full_like(ref, init)

    def chunk(start, first_row=None):
        kc = k_ref[pl.ds(start, tk), :]
        vc = v_ref[pl.ds(start, tk), :]
        rows = slice(first_row or 0, tq)
        s1 = _dot_nt(q1[rows], kc[:, :hd])
        s2 = _dot_nt(q2[rows], kc[:, hd:])
        if first_row is not None:
            causal = (lax.broadcasted_iota(jnp.int32, s1.shape, 0) >= lax.broadcasted_iota(jnp.int32, s1.shape, 1))
            s1 = jnp.where(causal, s1, NEG)
            s2 = jnp.where(causal, s2, NEG)
        _softmax_step(s1, vc, m1_ref, l1_ref, a1_ref, rows)
        _softmax_step(s2, vc, m2_ref, l2_ref, a2_ref, rows)

    def body(j, carry):
        chunk(pl.multiple_of(j * tk, tk))
        return carry

    lax.fori_loop(0, qi * per_tile, body, 0)
    for jj in range(per_tile):
        chunk(pl.multiple_of(qi * tq + jj * tk, tk), first_row=jj * tk)

    lam = _diff_lambda(lam_ref, lam_init)
    o = a1_ref[...] / l1_ref[...] - lam * (a2_ref[...] / l2_ref[...])
    o_ref[...] = _head_norm(o, g_ref[...], lam_init).astype(o_ref.dtype)


def _prompt_attention(qb, kb, vb, lam_rows, subln_g, lam_init, *, batch, seq):
    d = qb.shape[1]
    vd = d // N_HEADS
    tq = ATTN_Q_TILE
    nq = seq // tq
    stat = pltpu.VMEM((tq, LANES), F32)
    acc = pltpu.VMEM((tq, vd), F32)
    return pl.pallas_call(
        functools.partial(_pattn_kernel, lam_init=lam_init),
        grid=(batch, N_HEADS, nq),
        in_specs=[
            pl.BlockSpec((4, HEAD_DIM), lambda b, h, i: (0, 0)),
            pl.BlockSpec((1, vd), lambda b, h, i: (0, 0)),
            pl.BlockSpec((tq, vd), lambda b, h, i: (b * nq + i, h)),
            pl.BlockSpec((seq, vd), lambda b, h, i: (b, h)),
            pl.BlockSpec((seq, vd), lambda b, h, i: (b, h)),
        ],
        out_specs=pl.BlockSpec((tq, vd), lambda b, h, i: (b * nq + i, h)),
        out_shape=jax.ShapeDtypeStruct((batch * seq, d), BF16),
        scratch_shapes=[stat, stat, acc, stat, stat, acc],
        compiler_params=_params(("parallel", "parallel", "arbitrary")),
        name="prompt_attention",
    )(lam_rows, subln_g[None], qb, kb, vb)


def _sattn_kernel(pt_ref, lam_ref, g_ref, x_ref, wq_ref, kn_ref, vn_ref, *rest, lam_init, dec_seq):
    npg = SAMPLE_SEQS_PER_STEP * SAMPLE_PAGES_PER_STEP
    k_refs = rest[:npg]
    v_refs = rest[npg:2 * npg]
    o_ref = rest[2 * npg]
    qbd_ref, m_ref, l_ref, acc_ref = rest[2 * npg + 1:]
    c = pl.program_id(1)
    n_chunks = pl.num_programs(1)
    n_rows = SAMPLE_SEQS_PER_STEP * dec_seq
    vd = qbd_ref.shape[1]
    hrows = 2 * n_rows
    half = N_HEADS // 2
    prow = 2 * hrows

    def first_row(h):
        return (h % half) * prow + (h // half) * hrows

    @pl.when(c == 0)
    def _():
        q = _dot(x_ref[...].astype(BF16), wq_ref[...]) * Q_SCALE
        lane = lax.broadcasted_iota(jnp.int32, (n_rows, vd), 1)
        parts = []
        for h in sorted(range(N_HEADS), key=first_row):
            qh = q[:, h * vd:(h + 1) * vd]
            parts += [jnp.where(lane < HEAD_DIM, qh, 0.0), jnp.where(lane >= HEAD_DIM, qh, 0.0)]
        qbd_ref[...] = jnp.concatenate(parts, axis=0).astype(BF16)
        m_ref[...] = jnp.full_like(m_ref, NEG)
        l_ref[...] = jnp.zeros_like(l_ref)
        acc_ref[...] = jnp.zeros_like(acc_ref)

    keys_per_seq = SAMPLE_PAGES_PER_STEP * PAGE_SIZE
    n_cols = 2 * SAMPLE_SEQS_PER_STEP * keys_per_seq
    n_score_rows = N_HEADS * hrows
    rr = lax.broadcasted_iota(jnp.int32, (n_score_rows, n_cols), 0)
    cc = lax.broadcasted_iota(jnp.int32, (n_score_rows, n_cols), 1)
    valid = jnp.logical_and((rr // hrows) % 2 == cc % 2,
                            (rr % n_rows) // dec_seq == cc // (2 * keys_per_seq))

    def pair_rows(refs, j):
        both = pl.ds(j, 2 * PAGE_SIZE, stride=half)
        return jnp.concatenate([r.at[0][both, :] for r in refs], axis=0).astype(BF16)

    s = jnp.concatenate([_dot_nt(qbd_ref[j * prow:(j + 1) * prow, :], pair_rows(k_refs, j))
                         for j in range(half)], axis=0)
    s = jnp.where(valid, s, NEG)
    m_old = m_ref[...]
    m_new = jnp.maximum(m_old, jnp.max(s, axis=1, keepdims=True))
    alpha = jnp.exp2(m_old - m_new)
    p = jnp.exp2(s - _lane_tile(m_new, n_cols))
    l_ref[...] = alpha * l_ref[...] + jnp.sum(p, axis=1, keepdims=True)
    pb = p.astype(BF16)
    pv = jnp.concatenate([_dot(pb[j * prow:(j + 1) * prow, :], pair_rows(v_refs, j))
                          for j in range(half)], axis=0)
    acc_ref[...] = alpha * acc_ref[...] + pv
    m_ref[...] = m_new

    @pl.when(c == n_chunks - 1)
    def _():
        lam = _diff_lambda(lam_ref, lam_init)
        r2 = lax.broadcasted_iota(jnp.int32, (hrows, LANES), 0) % n_rows
        k2 = lax.broadcasted_iota(jnp.int32, (hrows, LANES), 1)
        ok = (k2 < n_rows) & (k2 // dec_seq == r2 // dec_seq) & (k2 % dec_seq <= r2 % dec_seq)
        pad = jnp.zeros((LANES - n_rows, vd), F32)
        outs = []
        for h in range(N_HEADS):
            new_head = pl.ds(h, n_rows, stride=N_HEADS)
            kn = jnp.concatenate([kn_ref[new_head, :], pad], axis=0).astype(BF16)
            vn = jnp.concatenate([vn_ref[new_head, :], pad], axis=0).astype(BF16)
            rows = slice(first_row(h), first_row(h) + hrows)
            sn = jnp.where(ok, _dot_nt(qbd_ref[rows, :], kn), NEG)
            _softmax_step(sn, vn, m_ref, l_ref, acc_ref, rows)
            o = acc_ref[rows, :] / l_ref[rows, :]
            outs.append(_head_norm(o[:n_rows] - lam * o[n_rows:], g_ref[...], lam_init))
        o_ref[...] = jnp.concatenate(outs, axis=1)


def _sample_attention(x, w_q, k_new, v_new, cache_k, cache_v, page_table, lam_rows, subln_g, lam_init,
                      *, n_prompt, dec_seq):
    n, d = x.shape
    vd = d // N_HEADS
    n_seq, n_pg = page_table.shape
    n_rows = SAMPLE_SEQS_PER_STEP * dec_seq
    n_cols = 2 * N_HEADS * n_rows
    pps = SAMPLE_PAGES_PER_STEP
    slab = PAGE_SIZE * N_HEADS
    ck = cache_k.reshape(cache_k.shape[0], slab, vd)
    cv = cache_v.reshape(cache_v.shape[0], slab, vd)
    pt = page_table.reshape(-1)
    base = n_prompt // n_rows

    def page_spec(s, j):
        return pl.BlockSpec((1, slab, vd),
                            lambda b, c, pt: (pt[(b * SAMPLE_SEQS_PER_STEP + s) * n_pg + c * pps + j], 0, 0))

    page_specs = [page_spec(s, j) for s in range(SAMPLE_SEQS_PER_STEP) for j in range(pps)]
    new_rows = pl.BlockSpec((n_rows * N_HEADS, vd), lambda b, c, pt: (b, 0))
    stat = pltpu.VMEM((n_cols, LANES), F32)
    grid_spec = pltpu.PrefetchScalarGridSpec(
        num_scalar_prefetch=1,
        grid=(n_seq // SAMPLE_SEQS_PER_STEP, n_pg // pps),
        in_specs=[
            pl.BlockSpec((4, HEAD_DIM), lambda b, c, pt: (0, 0)),
            pl.BlockSpec((1, vd), lambda b, c, pt: (0, 0)),
            pl.BlockSpec((n_rows, d), lambda b, c, pt: (base + b, 0)),
            pl.BlockSpec((d, d), lambda b, c, pt: (0, 0)),
            new_rows,
            new_rows,
        ] + page_specs + page_specs,
        out_specs=pl.BlockSpec((n_rows, d), lambda b, c, pt: (b, 0)),
        scratch_shapes=[pltpu.VMEM((n_cols, vd), BF16), stat, stat, pltpu.VMEM((n_cols, vd), F32)],
    )
    n_pages_step = SAMPLE_SEQS_PER_STEP * pps
    return pl.pallas_call(
        functools.partial(_sattn_kernel, lam_init=lam_init, dec_seq=dec_seq),
        grid_spec=grid_spec,
        out_shape=jax.ShapeDtypeStruct((n - n_prompt, d), F32),
        compiler_params=_params(("parallel", "arbitrary")),
        name="sample_attention",
    )(pt, lam_rows, subln_g[None], x, w_q.astype(BF16), k_new, v_new,
      *([ck] * n_pages_step), *([cv] * n_pages_step))


def _oproj_kernel(ap_ref, as_ref, w_ref, x_ref, g_ref, b_ref, o_ref, *, n_prompt_tiles):
    i = pl.program_id(0)
    a = jnp.where(i >= n_prompt_tiles, as_ref[...].astype(BF16), ap_ref[...])
    m = _dot(a, w_ref[...])
    o_ref[...] = _layer_norm(ALPHA * x_ref[...] + m, g_ref[...], b_ref[...])


def _oproj_layer(a_prompt, a_sample, w_o, x, ln_g, ln_b):
    n, d = x.shape
    tm = TOKEN_TILE
    n_prompt_tiles = a_prompt.shape[0] // tm
    row = pl.BlockSpec((tm, d), lambda i: (i, 0))
    const2 = lambda i: (0, 0)
    kern = functools.partial(_oproj_kernel, n_prompt_tiles=n_prompt_tiles)
    return pl.pallas_call(
        kern,
        grid=(n // tm,),
        in_specs=[
            pl.BlockSpec((tm, d), _lo_index(n_prompt_tiles)),
            pl.BlockSpec((tm, d), _hi_index(n_prompt_tiles)),
            pl.BlockSpec((d, d), const2),
            row,
            pl.BlockSpec((1, d), const2),
            pl.BlockSpec((1, d), const2),
        ],
        out_specs=row,
        out_shape=jax.ShapeDtypeStruct((n, d), F32),
        compiler_params=_params(("parallel",)),
        name="attn_out_proj",
    )(a_prompt, a_sample, w_o.astype(BF16), x, ln_g[None], ln_b[None])


def kernel(x_prompt, x_sample, cache_k, cache_v, page_table, ln1_g, ln1_b, ln2_g, ln2_b, a_w_in, a_b_in, a_lnv_g, a_lnv_b, a_w_s, a_b_s, a_w_out, w_k, w_v, b_w_q, b_lam_q1, b_lam_k1, b_lam_q2, b_lam_k2, b_subln_g, b_w_o, router_w, router_bias, moe_w_up, moe_w_down):
    batch, seq, d = x_prompt.shape
    dec_batch, dec_seq, _ = x_sample.shape
    n_prompt = batch * seq
    n_sample = dec_batch * dec_seq
    tm = TOKEN_TILE
    assert seq % CHUNK == 0 and CHUNK % dec_seq == 0 and dec_seq <= CHUNK
    assert n_prompt % tm == 0 and n_sample % tm == 0
    assert seq % ATTN_Q_TILE == 0 and ATTN_Q_TILE % ATTN_K_TILE == 0
    assert a_w_s.shape[-1] == CHUNK and cache_k.shape[1] == PAGE_SIZE and cache_k.shape[2] == N_HEADS
    assert dec_batch % SAMPLE_SEQS_PER_STEP == 0 and page_table.shape[1] % SAMPLE_PAGES_PER_STEP == 0
    assert SAMPLE_SEQS_PER_STEP * dec_seq == SUBLANES

    n_prompt_tiles = n_prompt // tm
    rwt = router_w.T.astype(BF16)
    rbias = router_bias.astype(F32)[:, None]
    chunk_v = []
    x = None
    for l in range(DEPTH):
        if l < N_A_LAYERS:
            if l == 0:
                xa, xb = x_prompt.reshape(n_prompt, d), x_sample.reshape(n_sample, d)
                ia, ib = _lo_index(n_prompt_tiles), _hi_index(n_prompt_tiles)
            else:
                xa, xb = x, x
                ia, ib = _lo_index(n_prompt_tiles), (lambda i: (jnp.maximum(i, n_prompt_tiles), 0))
            x, v_rows = _gmlp_layer(xa, xb, ia, ib, a_w_in[l], a_b_in[l], a_lnv_g[l], a_lnv_b[l], a_w_s[l],
                                    a_b_s[l], a_w_out[l], ln1_g[l], ln1_b[l],
                                    n_prompt=n_prompt, n_sample=n_sample, dec_seq=dec_seq)
            chunk_v.append(v_rows.reshape(dec_batch, dec_seq, -1))
        else:
            j = l - N_A_LAYERS
            lam_init = 0.8 - 0.6 * math.exp(-0.3 * l)
            if j == 0:
                kp, vp, ks, vs, kb, vb, qb = _kvq_proj(x, w_k, w_v, b_w_q[j], n_prompt=n_prompt)
            else:
                qb = _q_proj(x, b_w_q[j], n_prompt)
            lam_rows = jnp.stack([b_lam_q1[j], b_lam_k1[j], b_lam_q2[j], b_lam_k2[j]]).astype(F32)
            a_p = _prompt_attention(qb, kb, vb, lam_rows, b_subln_g[j], lam_init, batch=batch, seq=seq)
            a_s = _sample_attention(x, b_w_q[j], ks, vs, cache_k, cache_v, page_table, lam_rows,
                                    b_subln_g[j], lam_init, n_prompt=n_prompt, dec_seq=dec_seq)
            x = _oproj_layer(a_p, a_s, b_w_o[j], x, ln1_g[l], ln1_b[l])
        x = _moe_layer(x, rwt, rbias, moe_w_up, moe_w_down, ln2_g[l], ln2_b[l],
                       layer=l, n_prompt=n_prompt, split=(l == DEPTH - 1))

    yp, ys = x
    vd = d // N_HEADS
    return (yp.reshape(batch, seq, d), ys.reshape(dec_batch, dec_seq, d),
            kp.reshape(batch, seq, N_HEADS, vd), vp.reshape(batch, seq, N_HEADS, vd),
            ks.reshape(dec_batch, dec_seq, N_HEADS, vd), vs.reshape(dec_batch, dec_seq, N_HEADS, vd),
            jnp.stack(chunk_v))
```

```python
import functools
import math

import jax
import jax.numpy as jnp
from jax import lax
from jax.experimental import pallas as pl
from jax.experimental.pallas import tpu as pltpu

F32 = jnp.float32
BF16 = jnp.bfloat16

DEPTH = 4
N_A_LAYERS = DEPTH // 2
CHUNK = 128
A_GROUPS = 8
N_HEADS = 8
HEAD_DIM = 64
N_EXPERTS = 16
N_EXPERT_GROUPS = 4
EXPERTS_PER_GROUP = N_EXPERTS // N_EXPERT_GROUPS
ALPHA = (2.0 * DEPTH) ** 0.25
LN_EPS = 1e-5
NEG = -1e30
PAGE_SIZE = 128
Q_SCALE = HEAD_DIM ** -0.5 * math.log2(math.e)

LANES = 128
SUBLANES = 8
TOKEN_TILE = 512
ATTN_Q_TILE = 1024
ATTN_K_TILE = 1024
ATTN_DIAG_TILE = 512
SAMPLE_SEQS_PER_STEP = 2
SAMPLE_PAGES_PER_STEP = 8
VMEM_LIMIT = 56 * 1024 * 1024


def _params(semantics):
    return pltpu.CompilerParams(dimension_semantics=semantics, vmem_limit_bytes=VMEM_LIMIT)


def _layer_norm(x, g, b):
    mu = jnp.mean(x, axis=-1, keepdims=True)
    xc = x - mu
    var = jnp.mean(xc * xc, axis=-1, keepdims=True)
    return xc * lax.rsqrt(var + LN_EPS) * g + b


def _dot(a, b):
    return jnp.dot(a, b, preferred_element_type=F32)


def _dot_nt(a, b):
    return lax.dot_general(a, b, (((1,), (1,)), ((), ())), preferred_element_type=F32)


def _lo_index(n_prompt_tiles):
    return lambda i: (jnp.minimum(i, n_prompt_tiles - 1), 0)


def _hi_index(n_prompt_tiles):
    return lambda i: (jnp.maximum(i - n_prompt_tiles, 0), 0)


def _gmlp_kernel(xp_ref, xs_ref, win_ref, bin_ref, lvg_ref, lvb_ref, ws_ref, sb_ref,
                 wout_ref, g1_ref, b1_ref, xo_ref, v_ref, *, n_prompt_tiles, dec_seq):
    i = pl.program_id(0)
    is_sample = i >= n_prompt_tiles
    a_inner = wout_ref.shape[0]
    gd = a_inner // A_GROUPS
    tm = xp_ref.shape[0]

    x = jnp.where(is_sample, xs_ref[...], xp_ref[...])
    z = _dot(x.astype(BF16), win_ref[...]) + bin_ref[...]
    z = jax.nn.gelu(z)
    u = z[:, :a_inner]
    v = _layer_norm(z[:, a_inner:], lvg_ref[...], lvb_ref[...])

    @pl.when(is_sample)
    def _():
        v_ref[...] = v

    row = lax.broadcasted_iota(jnp.int32, (CHUNK, CHUNK), 0)
    col = lax.broadcasted_iota(jnp.int32, (CHUNK, CHUNK), 1)
    mask = jnp.logical_and(col <= row, jnp.logical_or(jnp.logical_not(is_sample),
                                                      (row // dec_seq) == (col // dec_seq)))
    vb = v.astype(BF16)
    kind = is_sample.astype(jnp.int32)
    bias = sb_ref[kind]
    s_rows = []
    mixes = [jnp.where(mask, ws_ref[kind, g], 0.0).astype(BF16) for g in range(A_GROUPS)]
    for c in range(tm // CHUNK):
        cols = []
        for g in range(A_GROUPS):
            blk = vb[c * CHUNK:(c + 1) * CHUNK, g * gd:(g + 1) * gd]
            cols.append(_dot(mixes[g], blk))
        s_rows.append(jnp.concatenate(cols, axis=1) + bias)
    s = jnp.concatenate(s_rows, axis=0)
    m = _dot((u * s).astype(BF16), wout_ref[...])
    xo_ref[...] = _layer_norm(ALPHA * x + m, g1_ref[...], b1_ref[...])


def _gmlp_layer(x_prompt, x_sample, prompt_index, sample_index, w_in, b_in, lnv_g, lnv_b, w_s, b_s, w_out,
                ln_g, ln_b, *, n_prompt, n_sample, dec_seq):
    d = x_prompt.shape[1]
    a_inner = w_out.shape[0]
    tm = TOKEN_TILE
    n = n_prompt + n_sample
    n_prompt_tiles = n_prompt // tm
    reps = CHUNK // dec_seq
    ws = jnp.stack([w_s, jnp.tile(w_s[:, :dec_seq, :dec_seq], (1, reps, reps))])
    gd = a_inner // A_GROUPS
    sb_p = jnp.repeat(b_s.T, gd, axis=1)
    sb_s = jnp.repeat(jnp.tile(b_s[:, :dec_seq].T, (reps, 1)), gd, axis=1)
    sb = jnp.stack([sb_p, sb_s])
    const2 = lambda i: (0, 0)
    const3 = lambda i: (0, 0, 0)
    const4 = lambda i: (0, 0, 0, 0)
    kern = functools.partial(_gmlp_kernel, n_prompt_tiles=n_prompt_tiles, dec_seq=dec_seq)
    return pl.pallas_call(
        kern,
        grid=(n // tm,),
        in_specs=[
            pl.BlockSpec((tm, d), prompt_index),
            pl.BlockSpec((tm, d), sample_index),
            pl.BlockSpec((d, 2 * a_inner), const2),
            pl.BlockSpec((1, 2 * a_inner), const2),
            pl.BlockSpec((1, a_inner), const2),
            pl.BlockSpec((1, a_inner), const2),
            pl.BlockSpec((2, A_GROUPS, CHUNK, CHUNK), const4),
            pl.BlockSpec((2, CHUNK, a_inner), const3),
            pl.BlockSpec((a_inner, d), const2),
            pl.BlockSpec((1, d), const2),
            pl.BlockSpec((1, d), const2),
        ],
        out_specs=[
            pl.BlockSpec((tm, d), lambda i: (i, 0)),
            pl.BlockSpec((tm, a_inner), _hi_index(n_prompt_tiles)),
        ],
        out_shape=[jax.ShapeDtypeStruct((n, d), F32), jax.ShapeDtypeStruct((n_sample, a_inner), F32)],
        compiler_params=_params(("arbitrary",)),
        name="gmlp_layer",
    )(x_prompt, x_sample, w_in.astype(BF16), b_in[None], lnv_g[None], lnv_b[None], ws, sb,
      w_out.astype(BF16), ln_g[None], ln_b[None])


def _route(x_bf, rwt_ref, rbias_ref):
    logits = _dot_nt(rwt_ref[...], x_bf)
    mx = jnp.max(logits, axis=0, keepdims=True)
    ex = jnp.exp(logits - mx)
    probs = ex / jnp.sum(ex, axis=0, keepdims=True)
    sel = probs + rbias_ref[...]
    p = [probs[e:e + 1, :] for e in range(N_EXPERTS)]
    s = [sel[e:e + 1, :] for e in range(N_EXPERTS)]
    gscore = []
    for g in range(N_EXPERT_GROUPS):
        a, b, c, d = s[4 * g:4 * g + 4]
        hi1, lo1 = jnp.maximum(a, b), jnp.minimum(a, b)
        hi2, lo2 = jnp.maximum(c, d), jnp.minimum(c, d)
        top1 = jnp.maximum(hi1, hi2)
        top2 = jnp.maximum(jnp.minimum(hi1, hi2), jnp.maximum(lo1, lo2))
        gscore.append(top1 + top2)
    best = gscore[0]
    gbest = jnp.zeros_like(best, dtype=jnp.int32)
    for g in range(1, N_EXPERT_GROUPS):
        better = gscore[g] > best
        best = jnp.where(better, gscore[g], best)
        gbest = jnp.where(better, g, gbest)
    gates = []
    for e in range(N_EXPERTS):
        g = e // EXPERTS_PER_GROUP
        rank = jnp.zeros_like(gbest)
        for f in range(EXPERTS_PER_GROUP * g, EXPERTS_PER_GROUP * (g + 1)):
            if f == e:
                continue
            beats = (s[f] >= s[e]) if f < e else (s[f] > s[e])
            rank = rank + beats.astype(jnp.int32)
        chosen = (gbest == g) & (rank < 2)
        gates.append(jnp.where(chosen, p[e], 0.0))
    denom = gates[0]
    for e in range(1, N_EXPERTS):
        denom = denom + gates[e]
    return [gt / denom for gt in gates]


MOE_ALIGN = 16
MOE_ROW_TILE = 512
MOE_TOKEN_TILE = 512


def _moe_local_rows(tm):
    return 2 * tm + N_EXPERTS * MOE_ALIGN


def _route_kernel(x_ref, rwt_ref, rbias_ref, prow_ref, pcol_ref, cnt_ref):
    gates = jnp.concatenate(_route(x_ref[...].astype(BF16), rwt_ref, rbias_ref), axis=0)
    tm = gates.shape[1]
    cnt = jnp.sum((gates > 0.0).astype(F32), axis=1, keepdims=True)
    cnt_i = cnt.astype(jnp.int32)
    cnt_ref[...] = jnp.broadcast_to(cnt_i, cnt_ref.shape)
    shift = MOE_ALIGN.bit_length() - 1
    padded = lax.shift_left(lax.shift_right_logical(cnt_i + (MOE_ALIGN - 1), shift), shift).astype(F32)
    starts, run = [], jnp.zeros((1, 1), F32)
    for e in range(N_EXPERTS):
        starts.append(run)
        run = run + padded[e:e + 1, :]
    pos1, pos2, g1, g2 = _local_positions(gates, jnp.concatenate(starts, axis=0))
    prow_ref[...] = jnp.concatenate([pos1, pos2, g1, g2, jnp.zeros((SUBLANES - 4, tm), F32)], axis=0)
    rows = jnp.concatenate([pos1, pos2, g1, g2, jnp.zeros((LANES - 4, tm), F32)], axis=0)
    pcol_ref[...] = rows.T


def _local_positions(gates, seg_start):
    n_e, t = gates.shape
    chosen = gates > 0.0
    ch_bf = chosen.astype(BF16)
    ta = lax.broadcasted_iota(jnp.int32, (t, t), 0)
    tb = lax.broadcasted_iota(jnp.int32, (t, t), 1)
    rank = _dot(ch_bf, (ta < tb).astype(BF16))
    ch = chosen.astype(F32)
    seen = jnp.zeros((1, t), F32)
    firsts = []
    for e in range(n_e):
        row = ch[e:e + 1, :]
        firsts.append(jnp.where(seen == 0.0, row, 0.0))
        seen = seen + row
    first = jnp.concatenate(firsts, axis=0)
    second = ch - first
    pos = seg_start + rank

    def pick(w):
        has = jnp.sum(w, axis=0, keepdims=True) > 0.0
        return (jnp.where(has, jnp.sum(w * pos, axis=0, keepdims=True), -1.0),
                jnp.sum(w * gates, axis=0, keepdims=True))

    pos1, g1 = pick(first)
    pos2, g2 = pick(second)
    return pos1, pos2, g1, g2


def _segment_copies(lstart_ref, far_ref, ngrp_ref, base, make_copy):
    total = jnp.int32(0)
    for e in range(N_EXPERTS):
        n = ngrp_ref[base + e]
        near = lstart_ref[base + e]
        far = far_ref[base + e]

        def body(j, carry, near=near, far=far):
            make_copy(pl.multiple_of(near + j * MOE_ALIGN, MOE_ALIGN),
                      pl.multiple_of(far + j * MOE_ALIGN, MOE_ALIGN)).start()
            return carry

        lax.fori_loop(0, n, body, 0)
        total = total + n
    return total


def _wait_copies(count, make_copy):
    def body(j, carry):
        make_copy(0, 0).wait()
        return carry

    lax.fori_loop(0, count, body, 0)


def _dispatch_kernel(lstart_ref, dst_ref, ngrp_ref, tail_ref, tailn_ref, nused_ref, x_ref, prow_ref, xs_ref,
                     loc_ref, zero_ref, pending_ref, sems):
    t = pl.program_id(0)
    slot = t % 2
    base = t * N_EXPERTS
    tm = x_ref.shape[0]
    rl = loc_ref.shape[1]
    pos = prow_ref[...].astype(jnp.int32)
    r = lax.broadcasted_iota(jnp.int32, (rl, tm), 0)
    perm = jnp.logical_or(r == pos[0:1, :], r == pos[1:2, :]).astype(BF16)
    loc_ref[slot] = _dot(perm, x_ref[...].astype(BF16)).astype(BF16)

    def to_sorted(buf):
        def make(src_row, dst_row):
            return pltpu.make_async_copy(loc_ref.at[buf, pl.ds(src_row, MOE_ALIGN), :],
                                         xs_ref.at[pl.ds(dst_row, MOE_ALIGN), :], sems.at[buf])
        return make

    @pl.when(t > 0)
    def _():
        _wait_copies(pending_ref[0], to_sorted(1 - slot))

    n_copies = _segment_copies(lstart_ref, dst_ref, ngrp_ref, base, to_sorted(slot))
    pending_ref[0] = n_copies
    sem = sems.at[slot]

    @pl.when(t == pl.num_programs(0) - 1)
    def _():
        _wait_copies(n_copies, to_sorted(slot))
        zero_ref[...] = jnp.zeros_like(zero_ref)

        def zero_fill(src_row, dst_row):
            return pltpu.make_async_copy(zero_ref.at[pl.ds(0, MOE_ALIGN), :],
                                         xs_ref.at[pl.ds(dst_row, MOE_ALIGN), :], sem)

        total = jnp.int32(0)
        for e in range(N_EXPERTS):
            n = tailn_ref[e]
            start = tail_ref[e]

            def body(j, carry, start=start):
                zero_fill(0, pl.multiple_of(start + j * MOE_ALIGN, MOE_ALIGN)).start()
                return carry

            lax.fori_loop(0, n, body, 0)
            total = total + n
        _wait_copies(total, zero_fill)

        rt = zero_ref.shape[0]

        def zero_tile(src_row, dst_row):
            return pltpu.make_async_copy(zero_ref, xs_ref.at[pl.ds(dst_row, rt), :], sem)

        n_unused = xs_ref.shape[0] // rt - nused_ref[0]

        def tile_body(j, carry):
            zero_tile(0, pl.multiple_of((nused_ref[0] + j) * rt, rt)).start()
            return carry

        lax.fori_loop(0, n_unused, tile_body, 0)
        _wait_copies(n_unused, zero_tile)


def _experts_kernel(te_ref, nused_ref, xs_ref, wup_ref, wdn_ref, ys_ref):
    d_exp = wdn_ref.shape[2]
    used = pl.program_id(0) < nused_ref[0]

    @pl.when(used)
    def _():
        h = _dot(xs_ref[...], wup_ref[0, 0].astype(BF16))
        act = jax.nn.silu(h[:, :d_exp]) * h[:, d_exp:]
        ys_ref[...] = _dot(act.astype(BF16), wdn_ref[0, 0].astype(BF16)).astype(ys_ref.dtype)

    @pl.when(jnp.logical_not(used))
    def _():
        ys_ref[...] = jnp.zeros_like(ys_ref)


def _combine_kernel(lstart_ref, dst_ref, ngrp_ref, x_ref, pcol_ref, pnext_ref, ys_ref, g_ref, b_ref, *rest,
                    n_prompt_tiles, split):
    if split:
        op_ref, os_ref, loc_ref, sel_ref, pending_ref, sems = rest
    else:
        o_ref, loc_ref, sel_ref, pending_ref, sems = rest
    t = pl.program_id(0)
    base = t * N_EXPERTS
    tm = x_ref.shape[0]
    rl = loc_ref.shape[1]

    def from_sorted(buf):
        def make(dst_row, src_row):
            return pltpu.make_async_copy(ys_ref.at[pl.ds(src_row, MOE_ALIGN), :],
                                         loc_ref.at[buf, pl.ds(dst_row, MOE_ALIGN), :], sems.at[buf])
        return make

    def build_sel(cols, buf):
        lane = lax.broadcasted_iota(jnp.int32, (tm, rl), 1)
        sel_ref[buf] = (jnp.where(lane == cols[:, 0:1].astype(jnp.int32), cols[:, 2:3], 0.0)
                        + jnp.where(lane == cols[:, 1:2].astype(jnp.int32), cols[:, 3:4], 0.0)).astype(BF16)

    @pl.when(t == 0)
    def _():
        loc_ref[...] = jnp.zeros_like(loc_ref)
        pending_ref[0] = _segment_copies(lstart_ref, dst_ref, ngrp_ref, base, from_sorted(0))
        build_sel(pcol_ref[...], 0)

    def step(cur):
        nxt = 1 - cur

        @pl.when(t + 1 < pl.num_programs(0))
        def _():
            pending_ref[nxt] = _segment_copies(lstart_ref, dst_ref, ngrp_ref, base + N_EXPERTS,
                                               from_sorted(nxt))

        _wait_copies(pending_ref[cur], from_sorted(cur))
        build_sel(pnext_ref[...], nxt)
        x = x_ref[...]
        y = _layer_norm(ALPHA * x + _dot(sel_ref[cur], loc_ref[cur]), g_ref[...], b_ref[...])
        if split:
            @pl.when(t < n_prompt_tiles)
            def _():
                op_ref[...] = y

            @pl.when(t >= n_prompt_tiles)
            def _():
                os_ref[...] = y
        else:
            o_ref[...] = y

    for parity in range(2):
        pl.when(t % 2 == parity)(functools.partial(step, parity))


def _moe_layer(x, rwt, rbias, w_up, w_down, ln_g, ln_b, *, layer, n_prompt, split):
    n, d = x.shape
    tm = MOE_TOKEN_TILE
    n_tiles = n // tm
    d_exp = w_down.shape[2]
    n_prompt_tiles = n_prompt // tm
    n_e, al, rt = N_EXPERTS, MOE_ALIGN, MOE_ROW_TILE
    rl = _moe_local_rows(tm)
    n_row_tiles = -(-(2 * n + n_tiles * n_e * (al - 1) + n_e * (rt - al)) // rt)

    pos_rows, pos_cols, cnt_raw = pl.pallas_call(
        _route_kernel,
        grid=(n_tiles,),
        in_specs=[pl.BlockSpec((tm, d), lambda i: (i, 0)),
                  pl.BlockSpec((n_e, d), lambda i: (0, 0)),
                  pl.BlockSpec((n_e, 1), lambda i: (0, 0))],
        out_specs=[pl.BlockSpec((SUBLANES, tm), lambda i: (0, i)),
                   pl.BlockSpec((tm, LANES), lambda i: (i, 0)),
                   pl.BlockSpec((n_e, LANES), lambda i: (i, 0))],
        out_shape=[jax.ShapeDtypeStruct((SUBLANES, n), F32), jax.ShapeDtypeStruct((n, LANES), F32),
                   jax.ShapeDtypeStruct((n_tiles * n_e, LANES), jnp.int32)],
        compiler_params=_params(("parallel",)),
        name="moe_route",
    )(x, rwt, rbias)

    cnt = cnt_raw[:, 0].reshape(n_tiles, n_e)
    pad = (cnt + (al - 1)) // al * al
    lstart = jnp.cumsum(pad, axis=1) - pad
    tot = jnp.sum(pad, axis=0)
    cap = (tot + (rt - 1)) // rt * rt
    cend = jnp.cumsum(cap)
    dst = (cend - cap)[None, :] + jnp.cumsum(pad, axis=0) - pad
    n_used = (cend[-1] // rt).astype(jnp.int32)
    tile_id = jnp.arange(n_row_tiles, dtype=jnp.int32)
    tile_expert = jnp.sum((tile_id[:, None] * rt >= cend[None, :]).astype(jnp.int32), axis=1)
    tile_expert = jnp.take(tile_expert, jnp.minimum(tile_id, n_used - 1))
    i32 = lambda a: a.reshape(-1).astype(jnp.int32)
    lstart, dst, ngrp = i32(lstart), i32(dst), i32(pad // al)
    tail, tailn = i32(cend - cap + tot), i32((cap - tot) // al)

    xs = pl.pallas_call(
        _dispatch_kernel,
        grid_spec=pltpu.PrefetchScalarGridSpec(
            num_scalar_prefetch=6,
            grid=(n_tiles,),
            in_specs=[pl.BlockSpec((tm, d), lambda i, *_: (i, 0)),
                      pl.BlockSpec((SUBLANES, tm), lambda i, *_: (0, i))],
            out_specs=pl.BlockSpec(memory_space=pl.ANY),
            scratch_shapes=[pltpu.VMEM((2, rl, d), BF16), pltpu.VMEM((rt, d), BF16),
                            pltpu.SMEM((1,), jnp.int32), pltpu.SemaphoreType.DMA((2,))],
        ),
        out_shape=jax.ShapeDtypeStruct((n_row_tiles * rt, d), BF16),
        compiler_params=_params(("arbitrary",)),
        name="moe_dispatch",
    )(lstart, dst, ngrp, tail, tailn, n_used.reshape(1), x, pos_rows)

    last_used = lambda i, te, nu: (jnp.minimum(i, nu[0] - 1), 0)
    ys = pl.pallas_call(
        _experts_kernel,
        grid_spec=pltpu.PrefetchScalarGridSpec(
            num_scalar_prefetch=2,
            grid=(n_row_tiles,),
            in_specs=[pl.BlockSpec((rt, d), last_used),
                      pl.BlockSpec((1, 1, d, 2 * d_exp), lambda i, te, nu: (layer, te[i], 0, 0)),
                      pl.BlockSpec((1, 1, d_exp, d), lambda i, te, nu: (layer, te[i], 0, 0))],
            out_specs=pl.BlockSpec((rt, d), lambda i, te, nu: (i, 0)),
        ),
        out_shape=jax.ShapeDtypeStruct((n_row_tiles * rt, d), BF16),
        compiler_params=_params(("arbitrary",)),
        name="moe_experts",
    )(tile_expert, n_used.reshape(1), xs, w_up, w_down)

    if split:
        lo, hi = _lo_index(n_prompt_tiles), _hi_index(n_prompt_tiles)
        out_specs = [pl.BlockSpec((tm, d), lambda i, *_: lo(i)), pl.BlockSpec((tm, d), lambda i, *_: hi(i))]
        out_shape = [jax.ShapeDtypeStruct((n_prompt, d), F32), jax.ShapeDtypeStruct((n - n_prompt, d), F32)]
    else:
        out_specs = pl.BlockSpec((tm, d), lambda i, *_: (i, 0))
        out_shape = jax.ShapeDtypeStruct((n, d), F32)
    return pl.pallas_call(
        functools.partial(_combine_kernel, n_prompt_tiles=n_prompt_tiles, split=split),
        grid_spec=pltpu.PrefetchScalarGridSpec(
            num_scalar_prefetch=3,
            grid=(n_tiles,),
            in_specs=[pl.BlockSpec((tm, d), lambda i, *_: (i, 0)),
                      pl.BlockSpec((tm, LANES), lambda i, *_: (i, 0)),
                      pl.BlockSpec((tm, LANES), lambda i, *_: (jnp.minimum(i + 1, n_tiles - 1), 0)),
                      pl.BlockSpec(memory_space=pl.ANY),
                      pl.BlockSpec((1, d), lambda i, *_: (0, 0)),
                      pl.BlockSpec((1, d), lambda i, *_: (0, 0))],
            out_specs=out_specs,
            scratch_shapes=[pltpu.VMEM((2, rl, d), BF16), pltpu.VMEM((2, tm, rl), BF16),
                            pltpu.SMEM((2,), jnp.int32), pltpu.SemaphoreType.DMA((2,))],
        ),
        out_shape=out_shape,
        compiler_params=_params(("arbitrary",)),
        name="moe_combine",
    )(lstart, dst, ngrp, x, pos_cols, pos_cols, ys, ln_g[None], ln_b[None])


def _kvq_kernel(x_ref, wk_ref, wv_ref, wq_ref, kp_ref, vp_ref, ks_ref, vs_ref, kb_ref, vb_ref, qb_ref,
                *, n_prompt_tiles):
    i = pl.program_id(0)
    xb = x_ref[...].astype(BF16)
    k = _dot(xb, wk_ref[...])
    v = _dot(xb, wv_ref[...])
    tm = k.shape[0]
    vd = k.shape[1] // N_HEADS

    def store_heads(ref, val):
        for h in range(N_HEADS):
            ref[pl.ds(h, tm, stride=N_HEADS), :] = val[:, h * vd:(h + 1) * vd]

    @pl.when(i < n_prompt_tiles)
    def _():
        store_heads(kp_ref, k)
        store_heads(vp_ref, v)
        kb_ref[...] = k.astype(BF16)
        vb_ref[...] = v.astype(BF16)
        qb_ref[...] = (_dot(xb, wq_ref[...]) * Q_SCALE).astype(BF16)

    @pl.when(i >= n_prompt_tiles)
    def _():
        store_heads(ks_ref, k)
        store_heads(vs_ref, v)


def _kvq_proj(x, w_k, w_v, w_q, *, n_prompt):
    n, d = x.shape
    tm = TOKEN_TILE
    n_prompt_tiles = n_prompt // tm
    n_sample = n - n_prompt
    vd = d // N_HEADS
    lo = pl.BlockSpec((tm, d), _lo_index(n_prompt_tiles))
    lo_heads = pl.BlockSpec((tm * N_HEADS, vd), _lo_index(n_prompt_tiles))
    hi_heads = pl.BlockSpec((tm * N_HEADS, vd), _hi_index(n_prompt_tiles))
    wspec = pl.BlockSpec((d, d), lambda i: (0, 0))
    return pl.pallas_call(
        functools.partial(_kvq_kernel, n_prompt_tiles=n_prompt_tiles),
        grid=(n // tm,),
        in_specs=[pl.BlockSpec((tm, d), lambda i: (i, 0)), wspec, wspec, wspec],
        out_specs=[lo_heads, lo_heads, hi_heads, hi_heads, lo, lo, lo],
        out_shape=[jax.ShapeDtypeStruct((n_prompt * N_HEADS, vd), F32),
                   jax.ShapeDtypeStruct((n_prompt * N_HEADS, vd), F32),
                   jax.ShapeDtypeStruct((n_sample * N_HEADS, vd), F32),
                   jax.ShapeDtypeStruct((n_sample * N_HEADS, vd), F32),
                   jax.ShapeDtypeStruct((n_prompt, d), BF16), jax.ShapeDtypeStruct((n_prompt, d), BF16),
                   jax.ShapeDtypeStruct((n_prompt, d), BF16)],
        compiler_params=_params(("arbitrary",)),
        name="kvq_proj",
    )(x, w_k.astype(BF16), w_v.astype(BF16), w_q.astype(BF16))


def _q_kernel(x_ref, wq_ref, qb_ref):
    qb_ref[...] = (_dot(x_ref[...].astype(BF16), wq_ref[...]) * Q_SCALE).astype(BF16)


def _q_proj(x, w_q, n_rows):
    d = x.shape[1]
    tm = TOKEN_TILE
    row = pl.BlockSpec((tm, d), lambda i: (i, 0))
    return pl.pallas_call(
        _q_kernel,
        grid=(n_rows // tm,),
        in_specs=[row, pl.BlockSpec((d, d), lambda i: (0, 0))],
        out_specs=row,
        out_shape=jax.ShapeDtypeStruct((n_rows, d), BF16),
        compiler_params=_params(("parallel",)),
        name="q_proj",
    )(x, w_q.astype(BF16))


def _diff_lambda(lam_ref, lam_init):
    lv = lam_ref[...]
    e1 = jnp.exp(jnp.sum(lv[0:1, :] * lv[1:2, :], axis=1, keepdims=True))
    e2 = jnp.exp(jnp.sum(lv[2:3, :] * lv[3:4, :], axis=1, keepdims=True))
    return e1 - e2 + lam_init


def _lane_tile(stat, n):
    return stat if n == LANES else jnp.concatenate([stat] * (n // LANES), axis=1)


def _softmax_step(s, vc, m_ref, l_ref, acc_ref, rows=None):
    sl = slice(None) if rows is None else rows
    m_old = m_ref[sl, :]
    m_new = jnp.maximum(m_old, jnp.max(s, axis=1, keepdims=True))
    alpha = jnp.exp2(m_old - m_new)
    p = jnp.exp2(s - _lane_tile(m_new, s.shape[1]))
    l_ref[sl, :] = alpha * l_ref[sl, :] + jnp.sum(p, axis=1, keepdims=True)
    acc_ref[sl, :] = alpha * acc_ref[sl, :] + _dot(p.astype(BF16), vc)
    m_ref[sl, :] = m_new


def _head_norm(o, g, lam_init):
    return o * lax.rsqrt(jnp.mean(o * o, axis=-1, keepdims=True) + LN_EPS) * g * (1.0 - lam_init)


def _pattn_kernel(lam_ref, g_ref, q_ref, k_ref, v_ref, o_ref,
                  m1_ref, l1_ref, a1_ref, m2_ref, l2_ref, a2_ref, *, lam_init):
    qi = pl.program_id(2)
    tq = q_ref.shape[0]
    tk = ATTN_K_TILE
    td = ATTN_DIAG_TILE
    hd = HEAD_DIM
    q = q_ref[...]
    q1, q2 = q[:, :hd], q[:, hd:]
    for ref, init in ((m1_ref, NEG), (l1_ref, 0.0), (a1_ref, 0.0), (m2_ref, NEG), (l2_ref, 0.0), (a2_ref, 0.0)):
        ref[...] = jnp.full_like(ref, init)

    def chunk(start, width, first_row=None):
        kc = k_ref[pl.ds(start, width), :]
        vc = v_ref[pl.ds(start, width), :]
        rows = slice(first_row or 0, tq)
        s1 = _dot_nt(q1[rows], kc[:, :hd])
        s2 = _dot_nt(q2[rows], kc[:, hd:])
        if first_row is not None:
            causal = (lax.broadcasted_iota(jnp.int32, s1.shape, 0) >= lax.broadcasted_iota(jnp.int32, s1.shape, 1))
            s1 = jnp.where(causal, s1, NEG)
            s2 = jnp.where(causal, s2, NEG)
        _softmax_step(s1, vc, m1_ref, l1_ref, a1_ref, rows)
        _softmax_step(s2, vc, m2_ref, l2_ref, a2_ref, rows)

    def body(j, carry):
        chunk(pl.multiple_of(j * tk, tk), tk)
        return carry

    lax.fori_loop(0, qi * (tq // tk), body, 0)
    for jj in range(tq // td):
        chunk(pl.multiple_of(qi * tq + jj * td, td), td, first_row=jj * td)

    lam = _diff_lambda(lam_ref, lam_init)
    o = a1_ref[...] / l1_ref[...] - lam * (a2_ref[...] / l2_ref[...])
    o_ref[...] = _head_norm(o, g_ref[...], lam_init).astype(o_ref.dtype)


def _prompt_attention(qb, kb, vb, lam_rows, subln_g, lam_init, *, batch, seq):
    d = qb.shape[1]
    vd = d // N_HEADS
    tq = ATTN_Q_TILE
    nq = seq // tq
    stat = pltpu.VMEM((tq, LANES), F32)
    acc = pltpu.VMEM((tq, vd), F32)
    return pl.pallas_call(
        functools.partial(_pattn_kernel, lam_init=lam_init),
        grid=(batch, N_HEADS, nq),
        in_specs=[
            pl.BlockSpec((4, HEAD_DIM), lambda b, h, i: (0, 0)),
            pl.BlockSpec((1, vd), lambda b, h, i: (0, 0)),
            pl.BlockSpec((tq, vd), lambda b, h, i: (b * nq + i, h)),
            pl.BlockSpec((seq, vd), lambda b, h, i: (b, h)),
            pl.BlockSpec((seq, vd), lambda b, h, i: (b, h)),
        ],
        out_specs=pl.BlockSpec((tq, vd), lambda b, h, i: (b * nq + i, h)),
        out_shape=jax.ShapeDtypeStruct((batch * seq, d), BF16),
        scratch_shapes=[stat, stat, acc, stat, stat, acc],
        compiler_params=_params(("parallel", "parallel", "arbitrary")),
        name="prompt_attention",
    )(lam_rows, subln_g[None], qb, kb, vb)


def _sattn_kernel(pt_ref, lam_ref, g_ref, x_ref, wq_ref, kn_ref, vn_ref, *rest, lam_init, dec_seq):
    npg = SAMPLE_SEQS_PER_STEP * SAMPLE_PAGES_PER_STEP
    k_refs = rest[:npg]
    v_refs = rest[npg:2 * npg]
    o_ref = rest[2 * npg]
    qbd_ref, m_ref, l_ref, acc_ref = rest[2 * npg + 1:]
    c = pl.program_id(1)
    n_chunks = pl.num_programs(1)
    n_rows = SAMPLE_SEQS_PER_STEP * dec_seq
    vd = qbd_ref.shape[1]
    hrows = 2 * n_rows
    half = N_HEADS // 2
    prow = 2 * hrows

    def first_row(h):
        return (h % half) * prow + (h // half) * hrows

    @pl.when(c == 0)
    def _():
        q = _dot(x_ref[...].astype(BF16), wq_ref[...]) * Q_SCALE
        lane = lax.broadcasted_iota(jnp.int32, (n_rows, vd), 1)
        parts = []
        for h in sorted(range(N_HEADS), key=first_row):
            qh = q[:, h * vd:(h + 1) * vd]
            parts += [jnp.where(lane < HEAD_DIM, qh, 0.0), jnp.where(lane >= HEAD_DIM, qh, 0.0)]
        qbd_ref[...] = jnp.concatenate(parts, axis=0).astype(BF16)
        m_ref[...] = jnp.full_like(m_ref, NEG)
        l_ref[...] = jnp.zeros_like(l_ref)
        acc_ref[...] = jnp.zeros_like(acc_ref)

    keys_per_seq = SAMPLE_PAGES_PER_STEP * PAGE_SIZE
    n_cols = 2 * SAMPLE_SEQS_PER_STEP * keys_per_seq
    n_score_rows = N_HEADS * hrows
    rr = lax.broadcasted_iota(jnp.int32, (n_score_rows, n_cols), 0)
    cc = lax.broadcasted_iota(jnp.int32, (n_score_rows, n_cols), 1)
    valid = jnp.logical_and((rr // hrows) % 2 == cc % 2,
                            (rr % n_rows) // dec_seq == cc // (2 * keys_per_seq))

    def pair_rows(refs, j):
        both = pl.ds(j, 2 * PAGE_SIZE, stride=half)
        return jnp.concatenate([r.at[0][both, :] for r in refs], axis=0).astype(BF16)

    s = jnp.concatenate([_dot_nt(qbd_ref[j * prow:(j + 1) * prow, :], pair_rows(k_refs, j))
                         for j in range(half)], axis=0)
    s = jnp.where(valid, s, NEG)
    m_old = m_ref[...]
    m_new = jnp.maximum(m_old, jnp.max(s, axis=1, keepdims=True))
    alpha = jnp.exp2(m_old - m_new)
    p = jnp.exp2(s - _lane_tile(m_new, n_cols))
    l_ref[...] = alpha * l_ref[...] + jnp.sum(p, axis=1, keepdims=True)
    pb = p.astype(BF16)
    pv = jnp.concatenate([_dot(pb[j * prow:(j + 1) * prow, :], pair_rows(v_refs, j))
                          for j in range(half)], axis=0)
    acc_ref[...] = alpha * acc_ref[...] + pv
    m_ref[...] = m_new

    @pl.when(c == n_chunks - 1)
    def _():
        lam = _diff_lambda(lam_ref, lam_init)
        r2 = lax.broadcasted_iota(jnp.int32, (hrows, LANES), 0) % n_rows
        k2 = lax.broadcasted_iota(jnp.int32, (hrows, LANES), 1)
        ok = (k2 < n_rows) & (k2 // dec_seq == r2 // dec_seq) & (k2 % dec_seq <= r2 % dec_seq)
        pad = jnp.zeros((LANES - n_rows, vd), F32)
        outs = []
        for h in range(N_HEADS):
            new_head = pl.ds(h, n_rows, stride=N_HEADS)
            kn = jnp.concatenate([kn_ref[new_head, :], pad], axis=0).astype(BF16)
            vn = jnp.concatenate([vn_ref[new_head, :], pad], axis=0).astype(BF16)
            rows = slice(first_row(h), first_row(h) + hrows)
            sn = jnp.where(ok, _dot_nt(qbd_ref[rows, :], kn), NEG)
            _softmax_step(sn, vn, m_ref, l_ref, acc_ref, rows)
            o = acc_ref[rows, :] / l_ref[rows, :]
            outs.append(_head_norm(o[:n_rows] - lam * o[n_rows:], g_ref[...], lam_init))
        o_ref[...] = jnp.concatenate(outs, axis=1)


def _sample_attention(x, w_q, k_new, v_new, cache_k, cache_v, page_table, lam_rows, subln_g, lam_init,
                      *, n_prompt, dec_seq):
    n, d = x.shape
    vd = d // N_HEADS
    n_seq, n_pg = page_table.shape
    n_rows = SAMPLE_SEQS_PER_STEP * dec_seq
    n_cols = 2 * N_HEADS * n_rows
    pps = SAMPLE_PAGES_PER_STEP
    slab = PAGE_SIZE * N_HEADS
    ck = cache_k.reshape(cache_k.shape[0], slab, vd)
    cv = cache_v.reshape(cache_v.shape[0], slab, vd)
    pt = page_table.reshape(-1)
    base = n_prompt // n_rows

    def page_spec(s, j):
        return pl.BlockSpec((1, slab, vd),
                            lambda b, c, pt: (pt[(b * SAMPLE_SEQS_PER_STEP + s) * n_pg + c * pps + j], 0, 0))

    page_specs = [page_spec(s, j) for s in range(SAMPLE_SEQS_PER_STEP) for j in range(pps)]
    new_rows = pl.BlockSpec((n_rows * N_HEADS, vd), lambda b, c, pt: (b, 0))
    stat = pltpu.VMEM((n_cols, LANES), F32)
    grid_spec = pltpu.PrefetchScalarGridSpec(
        num_scalar_prefetch=1,
        grid=(n_seq // SAMPLE_SEQS_PER_STEP, n_pg // pps),
        in_specs=[
            pl.BlockSpec((4, HEAD_DIM), lambda b, c, pt: (0, 0)),
            pl.BlockSpec((1, vd), lambda b, c, pt: (0, 0)),
            pl.BlockSpec((n_rows, d), lambda b, c, pt: (base + b, 0)),
            pl.BlockSpec((d, d), lambda b, c, pt: (0, 0)),
            new_rows,
            new_rows,
        ] + page_specs + page_specs,
        out_specs=pl.BlockSpec((n_rows, d), lambda b, c, pt: (b, 0)),
        scratch_shapes=[pltpu.VMEM((n_cols, vd), BF16), stat, stat, pltpu.VMEM((n_cols, vd), F32)],
    )
    n_pages_step = SAMPLE_SEQS_PER_STEP * pps
    return pl.pallas_call(
        functools.partial(_sattn_kernel, lam_init=lam_init, dec_seq=dec_seq),
        grid_spec=grid_spec,
        out_shape=jax.ShapeDtypeStruct((n - n_prompt, d), F32),
        compiler_params=_params(("parallel", "arbitrary")),
        name="sample_attention",
    )(pt, lam_rows, subln_g[None], x, w_q.astype(BF16), k_new, v_new,
      *([ck] * n_pages_step), *([cv] * n_pages_step))


def _oproj_kernel(ap_ref, as_ref, w_ref, x_ref, g_ref, b_ref, o_ref, *, n_prompt_tiles):
    i = pl.program_id(0)
    a = jnp.where(i >= n_prompt_tiles, as_ref[...].astype(BF16), ap_ref[...])
    m = _dot(a, w_ref[...])
    o_ref[...] = _layer_norm(ALPHA * x_ref[...] + m, g_ref[...], b_ref[...])


def _oproj_layer(a_prompt, a_sample, w_o, x, ln_g, ln_b):
    n, d = x.shape
    tm = TOKEN_TILE
    n_prompt_tiles = a_prompt.shape[0] // tm
    row = pl.BlockSpec((tm, d), lambda i: (i, 0))
    const2 = lambda i: (0, 0)
    kern = functools.partial(_oproj_kernel, n_prompt_tiles=n_prompt_tiles)
    return pl.pallas_call(
        kern,
        grid=(n // tm,),
        in_specs=[
            pl.BlockSpec((tm, d), _lo_index(n_prompt_tiles)),
            pl.BlockSpec((tm, d), _hi_index(n_prompt_tiles)),
            pl.BlockSpec((d, d), const2),
            row,
            pl.BlockSpec((1, d), const2),
            pl.BlockSpec((1, d), const2),
        ],
        out_specs=row,
        out_shape=jax.ShapeDtypeStruct((n, d), F32),
        compiler_params=_params(("parallel",)),
        name="attn_out_proj",
    )(a_prompt, a_sample, w_o.astype(BF16), x, ln_g[None], ln_b[None])


def kernel(x_prompt, x_sample, cache_k, cache_v, page_table, ln1_g, ln1_b, ln2_g, ln2_b, a_w_in, a_b_in, a_lnv_g, a_lnv_b, a_w_s, a_b_s, a_w_out, w_k, w_v, b_w_q, b_lam_q1, b_lam_k1, b_lam_q2, b_lam_k2, b_subln_g, b_w_o, router_w, router_bias, moe_w_up, moe_w_down):
    batch, seq, d = x_prompt.shape
    dec_batch, dec_seq, _ = x_sample.shape
    n_prompt = batch * seq
    n_sample = dec_batch * dec_seq
    tm = TOKEN_TILE
    assert seq % CHUNK == 0 and CHUNK % dec_seq == 0 and dec_seq <= CHUNK
    assert n_prompt % tm == 0 and n_sample % tm == 0
    assert seq % ATTN_Q_TILE == 0 and ATTN_Q_TILE % ATTN_K_TILE == 0 and ATTN_Q_TILE % ATTN_DIAG_TILE == 0
    assert a_w_s.shape[-1] == CHUNK and cache_k.shape[1] == PAGE_SIZE and cache_k.shape[2] == N_HEADS
    assert dec_batch % SAMPLE_SEQS_PER_STEP == 0 and page_table.shape[1] % SAMPLE_PAGES_PER_STEP == 0
    assert SAMPLE_SEQS_PER_STEP * dec_seq == SUBLANES

    n_prompt_tiles = n_prompt // tm
    rwt = router_w.T.astype(BF16)
    rbias = router_bias.astype(F32)[:, None]
    chunk_v = []
    x = None
    for l in range(DEPTH):
        if l < N_A_LAYERS:
            if l == 0:
                xa, xb = x_prompt.reshape(n_prompt, d), x_sample.reshape(n_sample, d)
                ia, ib = _lo_index(n_prompt_tiles), _hi_index(n_prompt_tiles)
            else:
                xa, xb = x, x
                ia, ib = _lo_index(n_prompt_tiles), (lambda i: (jnp.maximum(i, n_prompt_tiles), 0))
            x, v_rows = _gmlp_layer(xa, xb, ia, ib, a_w_in[l], a_b_in[l], a_lnv_g[l], a_lnv_b[l], a_w_s[l],
                                    a_b_s[l], a_w_out[l], ln1_g[l], ln1_b[l],
                                    n_prompt=n_prompt, n_sample=n_sample, dec_seq=dec_seq)
            chunk_v.append(v_rows.reshape(dec_batch, dec_seq, -1))
        else:
            j = l - N_A_LAYERS
            lam_init = 0.8 - 0.6 * math.exp(-0.3 * l)
            if j == 0:
                kp, vp, ks, vs, kb, vb, qb = _kvq_proj(x, w_k, w_v, b_w_q[j], n_prompt=n_prompt)
            else:
                qb = _q_proj(x, b_w_q[j], n_prompt)
            lam_rows = jnp.stack([b_lam_q1[j], b_lam_k1[j], b_lam_q2[j], b_lam_k2[j]]).astype(F32)
            a_p = _prompt_attention(qb, kb, vb, lam_rows, b_subln_g[j], lam_init, batch=batch, seq=seq)
            a_s = _sample_attention(x, b_w_q[j], ks, vs, cache_k, cache_v, page_table, lam_rows,
                                    b_subln_g[j], lam_init, n_prompt=n_prompt, dec_seq=dec_seq)
            x = _oproj_layer(a_p, a_s, b_w_o[j], x, ln1_g[l], ln1_b[l])
        x = _moe_layer(x, rwt, rbias, moe_w_up, moe_w_down, ln2_g[l], ln2_b[l],
                       layer=l, n_prompt=n_prompt, split=(l == DEPTH - 1))

    yp, ys = x
    vd = d // N_HEADS
    return (yp.reshape(batch, seq, d), ys.reshape(dec_batch, dec_seq, d),
            kp.reshape(batch, seq, N_HEADS, vd), vp.reshape(batch, seq, N_HEADS, vd),
            ks.reshape(dec_batch, dec_seq, N_HEADS, vd), vs.reshape(dec_batch, dec_seq, N_HEADS, vd),
            jnp.stack(chunk_v))
```

```python
import functools
import math

import jax
import jax.numpy as jnp
from jax import lax
from jax.experimental import pallas as pl
from jax.experimental.pallas import tpu as pltpu

F32 = jnp.float32
BF16 = jnp.bfloat16

DEPTH = 4
N_A_LAYERS = DEPTH // 2
CHUNK = 128
A_GROUPS = 8
N_HEADS = 8
HEAD_DIM = 64
N_EXPERTS = 16
N_EXPERT_GROUPS = 4
EXPERTS_PER_GROUP = N_EXPERTS // N_EXPERT_GROUPS
ALPHA = (2.0 * DEPTH) ** 0.25
LN_EPS = 1e-5
NEG = -1e30
PAGE_SIZE = 128
Q_SCALE = HEAD_DIM ** -0.5 * math.log2(math.e)

LANES = 128
SUBLANES = 8
TOKEN_TILE = 512
ATTN_Q_TILE = 2048
ATTN_K_TILE = 1024
ATTN_DIAG_TILE = 512
SAMPLE_SEQS_PER_STEP = 2
SAMPLE_PAGES_PER_STEP = 8
VMEM_LIMIT = 56 * 1024 * 1024


def _params(semantics):
    return pltpu.CompilerParams(dimension_semantics=semantics, vmem_limit_bytes=VMEM_LIMIT)


def _layer_norm(x, g, b):
    mu = jnp.mean(x, axis=-1, keepdims=True)
    xc = x - mu
    var = jnp.mean(xc * xc, axis=-1, keepdims=True)
    return xc * lax.rsqrt(var + LN_EPS) * g + b


def _dot(a, b):
    return jnp.dot(a, b, preferred_element_type=F32)


def _dot_nt(a, b):
    return lax.dot_general(a, b, (((1,), (1,)), ((), ())), preferred_element_type=F32)


def _lo_index(n_prompt_tiles):
    return lambda i: (jnp.minimum(i, n_prompt_tiles - 1), 0)


def _hi_index(n_prompt_tiles):
    return lambda i: (jnp.maximum(i - n_prompt_tiles, 0), 0)


def _gmlp_kernel(xp_ref, xs_ref, win_ref, bin_ref, lvg_ref, lvb_ref, ws_ref, sb_ref,
                 wout_ref, g1_ref, b1_ref, xo_ref, v_ref, *, n_prompt_tiles, dec_seq):
    i = pl.program_id(0)
    is_sample = i >= n_prompt_tiles
    a_inner = wout_ref.shape[0]
    gd = a_inner // A_GROUPS
    tm = xp_ref.shape[0]

    x = jnp.where(is_sample, xs_ref[...], xp_ref[...])
    z = _dot(x.astype(BF16), win_ref[...]) + bin_ref[...]
    z = jax.nn.gelu(z)
    u = z[:, :a_inner]
    v = _layer_norm(z[:, a_inner:], lvg_ref[...], lvb_ref[...])

    @pl.when(is_sample)
    def _():
        v_ref[...] = v

    row = lax.broadcasted_iota(jnp.int32, (CHUNK, CHUNK), 0)
    col = lax.broadcasted_iota(jnp.int32, (CHUNK, CHUNK), 1)
    mask = jnp.logical_and(col <= row, jnp.logical_or(jnp.logical_not(is_sample),
                                                      (row // dec_seq) == (col // dec_seq)))
    vb = v.astype(BF16)
    kind = is_sample.astype(jnp.int32)
    bias = sb_ref[kind]
    s_rows = []
    mixes = [jnp.where(mask, ws_ref[kind, g], 0.0).astype(BF16) for g in range(A_GROUPS)]
    for c in range(tm // CHUNK):
        cols = []
        for g in range(A_GROUPS):
            blk = vb[c * CHUNK:(c + 1) * CHUNK, g * gd:(g + 1) * gd]
            cols.append(_dot(mixes[g], blk))
        s_rows.append(jnp.concatenate(cols, axis=1) + bias)
    s = jnp.concatenate(s_rows, axis=0)
    m = _dot((u * s).astype(BF16), wout_ref[...])
    xo_ref[...] = _layer_norm(ALPHA * x + m, g1_ref[...], b1_ref[...])


def _gmlp_layer(x_prompt, x_sample, prompt_index, sample_index, w_in, b_in, lnv_g, lnv_b, w_s, b_s, w_out,
                ln_g, ln_b, *, n_prompt, n_sample, dec_seq):
    d = x_prompt.shape[1]
    a_inner = w_out.shape[0]
    tm = TOKEN_TILE
    n = n_prompt + n_sample
    n_prompt_tiles = n_prompt // tm
    reps = CHUNK // dec_seq
    ws = jnp.stack([w_s, jnp.tile(w_s[:, :dec_seq, :dec_seq], (1, reps, reps))])
    gd = a_inner // A_GROUPS
    sb_p = jnp.repeat(b_s.T, gd, axis=1)
    sb_s = jnp.repeat(jnp.tile(b_s[:, :dec_seq].T, (reps, 1)), gd, axis=1)
    sb = jnp.stack([sb_p, sb_s])
    const2 = lambda i: (0, 0)
    const3 = lambda i: (0, 0, 0)
    const4 = lambda i: (0, 0, 0, 0)
    kern = functools.partial(_gmlp_kernel, n_prompt_tiles=n_prompt_tiles, dec_seq=dec_seq)
    return pl.pallas_call(
        kern,
        grid=(n // tm,),
        in_specs=[
            pl.BlockSpec((tm, d), prompt_index),
            pl.BlockSpec((tm, d), sample_index),
            pl.BlockSpec((d, 2 * a_inner), const2),
            pl.BlockSpec((1, 2 * a_inner), const2),
            pl.BlockSpec((1, a_inner), const2),
            pl.BlockSpec((1, a_inner), const2),
            pl.BlockSpec((2, A_GROUPS, CHUNK, CHUNK), const4),
            pl.BlockSpec((2, CHUNK, a_inner), const3),
            pl.BlockSpec((a_inner, d), const2),
            pl.BlockSpec((1, d), const2),
            pl.BlockSpec((1, d), const2),
        ],
        out_specs=[
            pl.BlockSpec((tm, d), lambda i: (i, 0)),
            pl.BlockSpec((tm, a_inner), _hi_index(n_prompt_tiles)),
        ],
        out_shape=[jax.ShapeDtypeStruct((n, d), F32), jax.ShapeDtypeStruct((n_sample, a_inner), F32)],
        compiler_params=_params(("arbitrary",)),
        name="gmlp_layer",
    )(x_prompt, x_sample, w_in.astype(BF16), b_in[None], lnv_g[None], lnv_b[None], ws, sb,
      w_out.astype(BF16), ln_g[None], ln_b[None])


def _route(x_bf, rwt_ref, rbias_ref):
    logits = _dot_nt(rwt_ref[...], x_bf)
    mx = jnp.max(logits, axis=0, keepdims=True)
    ex = jnp.exp(logits - mx)
    probs = ex / jnp.sum(ex, axis=0, keepdims=True)
    sel = probs + rbias_ref[...]
    p = [probs[e:e + 1, :] for e in range(N_EXPERTS)]
    s = [sel[e:e + 1, :] for e in range(N_EXPERTS)]
    gscore = []
    for g in range(N_EXPERT_GROUPS):
        a, b, c, d = s[4 * g:4 * g + 4]
        hi1, lo1 = jnp.maximum(a, b), jnp.minimum(a, b)
        hi2, lo2 = jnp.maximum(c, d), jnp.minimum(c, d)
        top1 = jnp.maximum(hi1, hi2)
        top2 = jnp.maximum(jnp.minimum(hi1, hi2), jnp.maximum(lo1, lo2))
        gscore.append(top1 + top2)
    best = gscore[0]
    gbest = jnp.zeros_like(best, dtype=jnp.int32)
    for g in range(1, N_EXPERT_GROUPS):
        better = gscore[g] > best
        best = jnp.where(better, gscore[g], best)
        gbest = jnp.where(better, g, gbest)
    gates = []
    for e in range(N_EXPERTS):
        g = e // EXPERTS_PER_GROUP
        rank = jnp.zeros_like(gbest)
        for f in range(EXPERTS_PER_GROUP * g, EXPERTS_PER_GROUP * (g + 1)):
            if f == e:
                continue
            beats = (s[f] >= s[e]) if f < e else (s[f] > s[e])
            rank = rank + beats.astype(jnp.int32)
        chosen = (gbest == g) & (rank < 2)
        gates.append(jnp.where(chosen, p[e], 0.0))
    denom = gates[0]
    for e in range(1, N_EXPERTS):
        denom = denom + gates[e]
    return [gt / denom for gt in gates]


MOE_ALIGN = 16
MOE_ROW_TILE = 512
MOE_TOKEN_TILE = 512


def _moe_local_rows(tm):
    return 2 * tm + N_EXPERTS * MOE_ALIGN


def _route_kernel(x_ref, rwt_ref, rbias_ref, prow_ref, pcol_ref, cnt_ref):
    gates = jnp.concatenate(_route(x_ref[...].astype(BF16), rwt_ref, rbias_ref), axis=0)
    tm = gates.shape[1]
    cnt = jnp.sum((gates > 0.0).astype(F32), axis=1, keepdims=True)
    cnt_i = cnt.astype(jnp.int32)
    cnt_ref[...] = jnp.broadcast_to(cnt_i, cnt_ref.shape)
    shift = MOE_ALIGN.bit_length() - 1
    padded = lax.shift_left(lax.shift_right_logical(cnt_i + (MOE_ALIGN - 1), shift), shift).astype(F32)
    starts, run = [], jnp.zeros((1, 1), F32)
    for e in range(N_EXPERTS):
        starts.append(run)
        run = run + padded[e:e + 1, :]
    pos1, pos2, g1, g2 = _local_positions(gates, jnp.concatenate(starts, axis=0))
    prow_ref[...] = jnp.concatenate([pos1, pos2, g1, g2, jnp.zeros((SUBLANES - 4, tm), F32)], axis=0)
    rows = jnp.concatenate([pos1, pos2, g1, g2, jnp.zeros((LANES - 4, tm), F32)], axis=0)
    pcol_ref[...] = rows.T


def _local_positions(gates, seg_start):
    n_e, t = gates.shape
    chosen = gates > 0.0
    ch_bf = chosen.astype(BF16)
    ta = lax.broadcasted_iota(jnp.int32, (t, t), 0)
    tb = lax.broadcasted_iota(jnp.int32, (t, t), 1)
    rank = _dot(ch_bf, (ta < tb).astype(BF16))
    ch = chosen.astype(F32)
    seen = jnp.zeros((1, t), F32)
    firsts = []
    for e in range(n_e):
        row = ch[e:e + 1, :]
        firsts.append(jnp.where(seen == 0.0, row, 0.0))
        seen = seen + row
    first = jnp.concatenate(firsts, axis=0)
    second = ch - first
    pos = seg_start + rank

    def pick(w):
        has = jnp.sum(w, axis=0, keepdims=True) > 0.0
        return (jnp.where(has, jnp.sum(w * pos, axis=0, keepdims=True), -1.0),
                jnp.sum(w * gates, axis=0, keepdims=True))

    pos1, g1 = pick(first)
    pos2, g2 = pick(second)
    return pos1, pos2, g1, g2


def _segment_copies(lstart_ref, far_ref, ngrp_ref, base, make_copy):
    total = jnp.int32(0)
    for e in range(N_EXPERTS):
        n = ngrp_ref[base + e]
        near = lstart_ref[base + e]
        far = far_ref[base + e]

        def body(j, carry, near=near, far=far):
            make_copy(pl.multiple_of(near + j * MOE_ALIGN, MOE_ALIGN),
                      pl.multiple_of(far + j * MOE_ALIGN, MOE_ALIGN)).start()
            return carry

        lax.fori_loop(0, n, body, 0)
        total = total + n
    return total


def _wait_copies(count, make_copy):
    def body(j, carry):
        make_copy(0, 0).wait()
        return carry

    lax.fori_loop(0, count, body, 0)


def _dispatch_kernel(lstart_ref, dst_ref, ngrp_ref, tail_ref, tailn_ref, nused_ref, x_ref, prow_ref, xs_ref,
                     loc_ref, zero_ref, pending_ref, sems):
    t = pl.program_id(0)
    slot = t % 2
    base = t * N_EXPERTS
    tm = x_ref.shape[0]
    rl = loc_ref.shape[1]
    pos = prow_ref[...].astype(jnp.int32)
    r = lax.broadcasted_iota(jnp.int32, (rl, tm), 0)
    perm = jnp.logical_or(r == pos[0:1, :], r == pos[1:2, :]).astype(BF16)
    loc_ref[slot] = _dot(perm, x_ref[...].astype(BF16)).astype(BF16)

    def to_sorted(buf):
        def make(src_row, dst_row):
            return pltpu.make_async_copy(loc_ref.at[buf, pl.ds(src_row, MOE_ALIGN), :],
                                         xs_ref.at[pl.ds(dst_row, MOE_ALIGN), :], sems.at[buf])
        return make

    @pl.when(t > 0)
    def _():
        _wait_copies(pending_ref[0], to_sorted(1 - slot))

    n_copies = _segment_copies(lstart_ref, dst_ref, ngrp_ref, base, to_sorted(slot))
    pending_ref[0] = n_copies
    sem = sems.at[slot]

    @pl.when(t == pl.num_programs(0) - 1)
    def _():
        _wait_copies(n_copies, to_sorted(slot))
        zero_ref[...] = jnp.zeros_like(zero_ref)

        def zero_fill(src_row, dst_row):
            return pltpu.make_async_copy(zero_ref.at[pl.ds(0, MOE_ALIGN), :],
                                         xs_ref.at[pl.ds(dst_row, MOE_ALIGN), :], sem)

        total = jnp.int32(0)
        for e in range(N_EXPERTS):
            n = tailn_ref[e]
            start = tail_ref[e]

            def body(j, carry, start=start):
                zero_fill(0, pl.multiple_of(start + j * MOE_ALIGN, MOE_ALIGN)).start()
                return carry

            lax.fori_loop(0, n, body, 0)
            total = total + n
        _wait_copies(total, zero_fill)

        rt = zero_ref.shape[0]

        def zero_tile(src_row, dst_row):
            return pltpu.make_async_copy(zero_ref, xs_ref.at[pl.ds(dst_row, rt), :], sem)

        n_unused = xs_ref.shape[0] // rt - nused_ref[0]

        def tile_body(j, carry):
            zero_tile(0, pl.multiple_of((nused_ref[0] + j) * rt, rt)).start()
            return carry

        lax.fori_loop(0, n_unused, tile_body, 0)
        _wait_copies(n_unused, zero_tile)


def _experts_kernel(te_ref, nused_ref, xs_ref, wup_ref, wdn_ref, ys_ref):
    d_exp = wdn_ref.shape[2]
    used = pl.program_id(0) < nused_ref[0]

    @pl.when(used)
    def _():
        h = _dot(xs_ref[...], wup_ref[0, 0].astype(BF16))
        act = jax.nn.silu(h[:, :d_exp]) * h[:, d_exp:]
        ys_ref[...] = _dot(act.astype(BF16), wdn_ref[0, 0].astype(BF16)).astype(ys_ref.dtype)

    @pl.when(jnp.logical_not(used))
    def _():
        ys_ref[...] = jnp.zeros_like(ys_ref)


def _combine_kernel(lstart_ref, dst_ref, ngrp_ref, x_ref, pcol_ref, pnext_ref, ys_ref, g_ref, b_ref, *rest,
                    n_prompt_tiles, split):
    if split:
        op_ref, os_ref, loc_ref, sel_ref, pending_ref, sems = rest
    else:
        o_ref, loc_ref, sel_ref, pending_ref, sems = rest
    t = pl.program_id(0)
    base = t * N_EXPERTS
    tm = x_ref.shape[0]
    rl = loc_ref.shape[1]

    def from_sorted(buf):
        def make(dst_row, src_row):
            return pltpu.make_async_copy(ys_ref.at[pl.ds(src_row, MOE_ALIGN), :],
                                         loc_ref.at[buf, pl.ds(dst_row, MOE_ALIGN), :], sems.at[buf])
        return make

    def build_sel(cols, buf):
        lane = lax.broadcasted_iota(jnp.int32, (tm, rl), 1)
        sel_ref[buf] = (jnp.where(lane == cols[:, 0:1].astype(jnp.int32), cols[:, 2:3], 0.0)
                        + jnp.where(lane == cols[:, 1:2].astype(jnp.int32), cols[:, 3:4], 0.0)).astype(BF16)

    @pl.when(t == 0)
    def _():
        loc_ref[...] = jnp.zeros_like(loc_ref)
        pending_ref[0] = _segment_copies(lstart_ref, dst_ref, ngrp_ref, base, from_sorted(0))
        build_sel(pcol_ref[...], 0)

    def step(cur):
        nxt = 1 - cur

        @pl.when(t + 1 < pl.num_programs(0))
        def _():
            pending_ref[nxt] = _segment_copies(lstart_ref, dst_ref, ngrp_ref, base + N_EXPERTS,
                                               from_sorted(nxt))

        _wait_copies(pending_ref[cur], from_sorted(cur))
        build_sel(pnext_ref[...], nxt)
        x = x_ref[...]
        y = _layer_norm(ALPHA * x + _dot(sel_ref[cur], loc_ref[cur]), g_ref[...], b_ref[...])
        if split:
            @pl.when(t < n_prompt_tiles)
            def _():
                op_ref[...] = y

            @pl.when(t >= n_prompt_tiles)
            def _():
                os_ref[...] = y
        else:
            o_ref[...] = y

    for parity in range(2):
        pl.when(t % 2 == parity)(functools.partial(step, parity))


def _moe_layer(x, rwt, rbias, w_up, w_down, ln_g, ln_b, *, layer, n_prompt, split):
    n, d = x.shape
    tm = MOE_TOKEN_TILE
    n_tiles = n // tm
    d_exp = w_down.shape[2]
    n_prompt_tiles = n_prompt // tm
    n_e, al, rt = N_EXPERTS, MOE_ALIGN, MOE_ROW_TILE
    rl = _moe_local_rows(tm)
    n_row_tiles = -(-(2 * n + n_tiles * n_e * (al - 1) + n_e * (rt - al)) // rt)

    pos_rows, pos_cols, cnt_raw = pl.pallas_call(
        _route_kernel,
        grid=(n_tiles,),
        in_specs=[pl.BlockSpec((tm, d), lambda i: (i, 0)),
                  pl.BlockSpec((n_e, d), lambda i: (0, 0)),
                  pl.BlockSpec((n_e, 1), lambda i: (0, 0))],
        out_specs=[pl.BlockSpec((SUBLANES, tm), lambda i: (0, i)),
                   pl.BlockSpec((tm, LANES), lambda i: (i, 0)),
                   pl.BlockSpec((n_e, LANES), lambda i: (i, 0))],
        out_shape=[jax.ShapeDtypeStruct((SUBLANES, n), F32), jax.ShapeDtypeStruct((n, LANES), F32),
                   jax.ShapeDtypeStruct((n_tiles * n_e, LANES), jnp.int32)],
        compiler_params=_params(("parallel",)),
        name="moe_route",
    )(x, rwt, rbias)

    cnt = cnt_raw[:, 0].reshape(n_tiles, n_e)
    pad = (cnt + (al - 1)) // al * al
    lstart = jnp.cumsum(pad, axis=1) - pad
    tot = jnp.sum(pad, axis=0)
    cap = (tot + (rt - 1)) // rt * rt
    cend = jnp.cumsum(cap)
    dst = (cend - cap)[None, :] + jnp.cumsum(pad, axis=0) - pad
    n_used = (cend[-1] // rt).astype(jnp.int32)
    tile_id = jnp.arange(n_row_tiles, dtype=jnp.int32)
    tile_expert = jnp.sum((tile_id[:, None] * rt >= cend[None, :]).astype(jnp.int32), axis=1)
    tile_expert = jnp.take(tile_expert, jnp.minimum(tile_id, n_used - 1))
    i32 = lambda a: a.reshape(-1).astype(jnp.int32)
    lstart, dst, ngrp = i32(lstart), i32(dst), i32(pad // al)
    tail, tailn = i32(cend - cap + tot), i32((cap - tot) // al)

    xs = pl.pallas_call(
        _dispatch_kernel,
        grid_spec=pltpu.PrefetchScalarGridSpec(
            num_scalar_prefetch=6,
            grid=(n_tiles,),
            in_specs=[pl.BlockSpec((tm, d), lambda i, *_: (i, 0)),
                      pl.BlockSpec((SUBLANES, tm), lambda i, *_: (0, i))],
            out_specs=pl.BlockSpec(memory_space=pl.ANY),
            scratch_shapes=[pltpu.VMEM((2, rl, d), BF16), pltpu.VMEM((rt, d), BF16),
                            pltpu.SMEM((1,), jnp.int32), pltpu.SemaphoreType.DMA((2,))],
        ),
        out_shape=jax.ShapeDtypeStruct((n_row_tiles * rt, d), BF16),
        compiler_params=_params(("arbitrary",)),
        name="moe_dispatch",
    )(lstart, dst, ngrp, tail, tailn, n_used.reshape(1), x, pos_rows)

    last_used = lambda i, te, nu: (jnp.minimum(i, nu[0] - 1), 0)
    ys = pl.pallas_call(
        _experts_kernel,
        grid_spec=pltpu.PrefetchScalarGridSpec(
            num_scalar_prefetch=2,
            grid=(n_row_tiles,),
            in_specs=[pl.BlockSpec((rt, d), last_used),
                      pl.BlockSpec((1, 1, d, 2 * d_exp), lambda i, te, nu: (layer, te[i], 0, 0)),
                      pl.BlockSpec((1, 1, d_exp, d), lambda i, te, nu: (layer, te[i], 0, 0))],
            out_specs=pl.BlockSpec((rt, d), lambda i, te, nu: (i, 0)),
        ),
        out_shape=jax.ShapeDtypeStruct((n_row_tiles * rt, d), BF16),
        compiler_params=_params(("arbitrary",)),
        name="moe_experts",
    )(tile_expert, n_used.reshape(1), xs, w_up, w_down)

    if split:
        lo, hi = _lo_index(n_prompt_tiles), _hi_index(n_prompt_tiles)
        out_specs = [pl.BlockSpec((tm, d), lambda i, *_: lo(i)), pl.BlockSpec((tm, d), lambda i, *_: hi(i))]
        out_shape = [jax.ShapeDtypeStruct((n_prompt, d), F32), jax.ShapeDtypeStruct((n - n_prompt, d), F32)]
    else:
        out_specs = pl.BlockSpec((tm, d), lambda i, *_: (i, 0))
        out_shape = jax.ShapeDtypeStruct((n, d), F32)
    return pl.pallas_call(
        functools.partial(_combine_kernel, n_prompt_tiles=n_prompt_tiles, split=split),
        grid_spec=pltpu.PrefetchScalarGridSpec(
            num_scalar_prefetch=3,
            grid=(n_tiles,),
            in_specs=[pl.BlockSpec((tm, d), lambda i, *_: (i, 0)),
                      pl.BlockSpec((tm, LANES), lambda i, *_: (i, 0)),
                      pl.BlockSpec((tm, LANES), lambda i, *_: (jnp.minimum(i + 1, n_tiles - 1), 0)),
                      pl.BlockSpec(memory_space=pl.ANY),
                      pl.BlockSpec((1, d), lambda i, *_: (0, 0)),
                      pl.BlockSpec((1, d), lambda i, *_: (0, 0))],
            out_specs=out_specs,
            scratch_shapes=[pltpu.VMEM((2, rl, d), BF16), pltpu.VMEM((2, tm, rl), BF16),
                            pltpu.SMEM((2,), jnp.int32), pltpu.SemaphoreType.DMA((2,))],
        ),
        out_shape=out_shape,
        compiler_params=_params(("arbitrary",)),
        name="moe_combine",
    )(lstart, dst, ngrp, x, pos_cols, pos_cols, ys, ln_g[None], ln_b[None])


def _kvq_kernel(x_ref, wk_ref, wv_ref, wq_ref, kp_ref, vp_ref, ks_ref, vs_ref, kb_ref, vb_ref, qb_ref,
                *, n_prompt_tiles):
    i = pl.program_id(0)
    xb = x_ref[...].astype(BF16)
    k = _dot(xb, wk_ref[...])
    v = _dot(xb, wv_ref[...])
    tm = k.shape[0]
    vd = k.shape[1] // N_HEADS

    def store_heads(ref, val):
        for h in range(N_HEADS):
            ref[pl.ds(h, tm, stride=N_HEADS), :] = val[:, h * vd:(h + 1) * vd]

    @pl.when(i < n_prompt_tiles)
    def _():
        store_heads(kp_ref, k)
        store_heads(vp_ref, v)
        kb_ref[...] = k.astype(BF16)
        vb_ref[...] = v.astype(BF16)
        qb_ref[...] = (_dot(xb, wq_ref[...]) * Q_SCALE).astype(BF16)

    @pl.when(i >= n_prompt_tiles)
    def _():
        store_heads(ks_ref, k)
        store_heads(vs_ref, v)


def _kvq_proj(x, w_k, w_v, w_q, *, n_prompt):
    n, d = x.shape
    tm = TOKEN_TILE
    n_prompt_tiles = n_prompt // tm
    n_sample = n - n_prompt
    vd = d // N_HEADS
    lo = pl.BlockSpec((tm, d), _lo_index(n_prompt_tiles))
    lo_heads = pl.BlockSpec((tm * N_HEADS, vd), _lo_index(n_prompt_tiles))
    hi_heads = pl.BlockSpec((tm * N_HEADS, vd), _hi_index(n_prompt_tiles))
    wspec = pl.BlockSpec((d, d), lambda i: (0, 0))
    return pl.pallas_call(
        functools.partial(_kvq_kernel, n_prompt_tiles=n_prompt_tiles),
        grid=(n // tm,),
        in_specs=[pl.BlockSpec((tm, d), lambda i: (i, 0)), wspec, wspec, wspec],
        out_specs=[lo_heads, lo_heads, hi_heads, hi_heads, lo, lo, lo],
        out_shape=[jax.ShapeDtypeStruct((n_prompt * N_HEADS, vd), F32),
                   jax.ShapeDtypeStruct((n_prompt * N_HEADS, vd), F32),
                   jax.ShapeDtypeStruct((n_sample * N_HEADS, vd), F32),
                   jax.ShapeDtypeStruct((n_sample * N_HEADS, vd), F32),
                   jax.ShapeDtypeStruct((n_prompt, d), BF16), jax.ShapeDtypeStruct((n_prompt, d), BF16),
                   jax.ShapeDtypeStruct((n_prompt, d), BF16)],
        compiler_params=_params(("arbitrary",)),
        name="kvq_proj",
    )(x, w_k.astype(BF16), w_v.astype(BF16), w_q.astype(BF16))


def _q_kernel(x_ref, wq_ref, qb_ref):
    qb_ref[...] = (_dot(x_ref[...].astype(BF16), wq_ref[...]) * Q_SCALE).astype(BF16)


def _q_proj(x, w_q, n_rows):
    d = x.shape[1]
    tm = TOKEN_TILE
    row = pl.BlockSpec((tm, d), lambda i: (i, 0))
    return pl.pallas_call(
        _q_kernel,
        grid=(n_rows // tm,),
        in_specs=[row, pl.BlockSpec((d, d), lambda i: (0, 0))],
        out_specs=row,
        out_shape=jax.ShapeDtypeStruct((n_rows, d), BF16),
        compiler_params=_params(("parallel",)),
        name="q_proj",
    )(x, w_q.astype(BF16))


def _diff_lambda(lam_ref, lam_init):
    lv = lam_ref[...]
    e1 = jnp.exp(jnp.sum(lv[0:1, :] * lv[1:2, :], axis=1, keepdims=True))
    e2 = jnp.exp(jnp.sum(lv[2:3, :] * lv[3:4, :], axis=1, keepdims=True))
    return e1 - e2 + lam_init


def _lane_tile(stat, n):
    return stat if n == LANES else jnp.concatenate([stat] * (n // LANES), axis=1)


def _softmax_step(s, vc, m_ref, l_ref, acc_ref, rows=None):
    sl = slice(None) if rows is None else rows
    m_old = m_ref[sl, :]
    m_new = jnp.maximum(m_old, jnp.max(s, axis=1, keepdims=True))
    alpha = jnp.exp2(m_old - m_new)
    p = jnp.exp2(s - _lane_tile(m_new, s.shape[1]))
    l_ref[sl, :] = alpha * l_ref[sl, :] + jnp.sum(p, axis=1, keepdims=True)
    acc_ref[sl, :] = alpha * acc_ref[sl, :] + _dot(p.astype(BF16), vc)
    m_ref[sl, :] = m_new


def _head_norm(o, g, lam_init):
    return o * lax.rsqrt(jnp.mean(o * o, axis=-1, keepdims=True) + LN_EPS) * g * (1.0 - lam_init)


def _pattn_kernel(lam_ref, g_ref, q_ref, k_ref, v_ref, o_ref,
                  m1_ref, l1_ref, a1_ref, m2_ref, l2_ref, a2_ref, *, lam_init):
    qi = pl.program_id(2)
    tq = q_ref.shape[0]
    tk = ATTN_K_TILE
    td = ATTN_DIAG_TILE
    hd = HEAD_DIM
    q = q_ref[...]
    q1, q2 = q[:, :hd], q[:, hd:]
    for ref, init in ((m1_ref, NEG), (l1_ref, 0.0), (a1_ref, 0.0), (m2_ref, NEG), (l2_ref, 0.0), (a2_ref, 0.0)):
        ref[...] = jnp.full_like(ref, init)

    def chunk(start, width, first_row=None):
        kc = k_ref[pl.ds(start, width), :]
        vc = v_ref[pl.ds(start, width), :]
        rows = slice(first_row or 0, tq)
        s1 = _dot_nt(q1[rows], kc[:, :hd])
        s2 = _dot_nt(q2[rows], kc[:, hd:])
        if first_row is not None:
            causal = (lax.broadcasted_iota(jnp.int32, s1.shape, 0) >= lax.broadcasted_iota(jnp.int32, s1.shape, 1))
            s1 = jnp.where(causal, s1, NEG)
            s2 = jnp.where(causal, s2, NEG)
        _softmax_step(s1, vc, m1_ref, l1_ref, a1_ref, rows)
        _softmax_step(s2, vc, m2_ref, l2_ref, a2_ref, rows)

    def body(j, carry):
        chunk(pl.multiple_of(j * tk, tk), tk)
        return carry

    lax.fori_loop(0, qi * (tq // tk), body, 0)
    for jj in range(tq // td):
        chunk(pl.multiple_of(qi * tq + jj * td, td), td, first_row=jj * td)

    lam = _diff_lambda(lam_ref, lam_init)
    o = a1_ref[...] / l1_ref[...] - lam * (a2_ref[...] / l2_ref[...])
    o_ref[...] = _head_norm(o, g_ref[...], lam_init).astype(o_ref.dtype)


def _prompt_attention(qb, kb, vb, lam_rows, subln_g, lam_init, *, batch, seq):
    d = qb.shape[1]
    vd = d // N_HEADS
    tq = ATTN_Q_TILE
    nq = seq // tq
    stat = pltpu.VMEM((tq, LANES), F32)
    acc = pltpu.VMEM((tq, vd), F32)
    return pl.pallas_call(
        functools.partial(_pattn_kernel, lam_init=lam_init),
        grid=(batch, N_HEADS, nq),
        in_specs=[
            pl.BlockSpec((4, HEAD_DIM), lambda b, h, i: (0, 0)),
            pl.BlockSpec((1, vd), lambda b, h, i: (0, 0)),
            pl.BlockSpec((tq, vd), lambda b, h, i: (b * nq + i, h)),
            pl.BlockSpec((seq, vd), lambda b, h, i: (b, h)),
            pl.BlockSpec((seq, vd), lambda b, h, i: (b, h)),
        ],
        out_specs=pl.BlockSpec((tq, vd), lambda b, h, i: (b * nq + i, h)),
        out_shape=jax.ShapeDtypeStruct((batch * seq, d), BF16),
        scratch_shapes=[stat, stat, acc, stat, stat, acc],
        compiler_params=_params(("parallel", "parallel", "arbitrary")),
        name="prompt_attention",
    )(lam_rows, subln_g[None], qb, kb, vb)


def _sattn_kernel(pt_ref, lam_ref, g_ref, x_ref, wq_ref, kn_ref, vn_ref, *rest, lam_init, dec_seq):
    npg = SAMPLE_SEQS_PER_STEP * SAMPLE_PAGES_PER_STEP
    k_refs = rest[:npg]
    v_refs = rest[npg:2 * npg]
    o_ref = rest[2 * npg]
    qbd_ref, m_ref, l_ref, acc_ref = rest[2 * npg + 1:]
    c = pl.program_id(1)
    n_chunks = pl.num_programs(1)
    n_rows = SAMPLE_SEQS_PER_STEP * dec_seq
    vd = qbd_ref.shape[1]
    hrows = 2 * n_rows
    half = N_HEADS // 2
    prow = 2 * hrows

    def first_row(h):
        return (h % half) * prow + (h // half) * hrows

    @pl.when(c == 0)
    def _():
        q = _dot(x_ref[...].astype(BF16), wq_ref[...]) * Q_SCALE
        lane = lax.broadcasted_iota(jnp.int32, (n_rows, vd), 1)
        parts = []
        for h in sorted(range(N_HEADS), key=first_row):
            qh = q[:, h * vd:(h + 1) * vd]
            parts += [jnp.where(lane < HEAD_DIM, qh, 0.0), jnp.where(lane >= HEAD_DIM, qh, 0.0)]
        qbd_ref[...] = jnp.concatenate(parts, axis=0).astype(BF16)
        m_ref[...] = jnp.full_like(m_ref, NEG)
        l_ref[...] = jnp.zeros_like(l_ref)
        acc_ref[...] = jnp.zeros_like(acc_ref)

    keys_per_seq = SAMPLE_PAGES_PER_STEP * PAGE_SIZE
    n_cols = 2 * SAMPLE_SEQS_PER_STEP * keys_per_seq
    n_score_rows = N_HEADS * hrows
    rr = lax.broadcasted_iota(jnp.int32, (n_score_rows, n_cols), 0)
    cc = lax.broadcasted_iota(jnp.int32, (n_score_rows, n_cols), 1)
    valid = jnp.logical_and((rr // hrows) % 2 == cc % 2,
                            (rr % n_rows) // dec_seq == cc // (2 * keys_per_seq))

    def pair_rows(refs, j):
        both = pl.ds(j, 2 * PAGE_SIZE, stride=half)
        return jnp.concatenate([r.at[0][both, :] for r in refs], axis=0).astype(BF16)

    s = jnp.concatenate([_dot_nt(qbd_ref[j * prow:(j + 1) * prow, :], pair_rows(k_refs, j))
                         for j in range(half)], axis=0)
    s = jnp.where(valid, s, NEG)
    m_old = m_ref[...]
    m_new = jnp.maximum(m_old, jnp.max(s, axis=1, keepdims=True))
    alpha = jnp.exp2(m_old - m_new)
    p = jnp.exp2(s - _lane_tile(m_new, n_cols))
    l_ref[...] = alpha * l_ref[...] + jnp.sum(p, axis=1, keepdims=True)
    pb = p.astype(BF16)
    pv = jnp.concatenate([_dot(pb[j * prow:(j + 1) * prow, :], pair_rows(v_refs, j))
                          for j in range(half)], axis=0)
    acc_ref[...] = alpha * acc_ref[...] + pv
    m_ref[...] = m_new

    @pl.when(c == n_chunks - 1)
    def _():
        lam = _diff_lambda(lam_ref, lam_init)
        r2 = lax.broadcasted_iota(jnp.int32, (hrows, LANES), 0) % n_rows
        k2 = lax.broadcasted_iota(jnp.int32, (hrows, LANES), 1)
        ok = (k2 < n_rows) & (k2 // dec_seq == r2 // dec_seq) & (k2 % dec_seq <= r2 % dec_seq)
        pad = jnp.zeros((LANES - n_rows, vd), F32)
        outs = []
        for h in range(N_HEADS):
            new_head = pl.ds(h, n_rows, stride=N_HEADS)
            kn = jnp.concatenate([kn_ref[new_head, :], pad], axis=0).astype(BF16)
            vn = jnp.concatenate([vn_ref[new_head, :], pad], axis=0).astype(BF16)
            rows = slice(first_row(h), first_row(h) + hrows)
            sn = jnp.where(ok, _dot_nt(qbd_ref[rows, :], kn), NEG)
            _softmax_step(sn, vn, m_ref, l_ref, acc_ref, rows)
            o = acc_ref[rows, :] / l_ref[rows, :]
            outs.append(_head_norm(o[:n_rows] - lam * o[n_rows:], g_ref[...], lam_init))
        o_ref[...] = jnp.concatenate(outs, axis=1)


def _sample_attention(x, w_q, k_new, v_new, cache_k, cache_v, page_table, lam_rows, subln_g, lam_init,
                      *, n_prompt, dec_seq):
    n, d = x.shape
    vd = d // N_HEADS
    n_seq, n_pg = page_table.shape
    n_rows = SAMPLE_SEQS_PER_STEP * dec_seq
    n_cols = 2 * N_HEADS * n_rows
    pps = SAMPLE_PAGES_PER_STEP
    slab = PAGE_SIZE * N_HEADS
    ck = cache_k.reshape(cache_k.shape[0], slab, vd)
    cv = cache_v.reshape(cache_v.shape[0], slab, vd)
    pt = page_table.reshape(-1)
    base = n_prompt // n_rows

    def page_spec(s, j):
        return pl.BlockSpec((1, slab, vd),
                            lambda b, c, pt: (pt[(b * SAMPLE_SEQS_PER_STEP + s) * n_pg + c * pps + j], 0, 0))

    page_specs = [page_spec(s, j) for s in range(SAMPLE_SEQS_PER_STEP) for j in range(pps)]
    new_rows = pl.BlockSpec((n_rows * N_HEADS, vd), lambda b, c, pt: (b, 0))
    stat = pltpu.VMEM((n_cols, LANES), F32)
    grid_spec = pltpu.PrefetchScalarGridSpec(
        num_scalar_prefetch=1,
        grid=(n_seq // SAMPLE_SEQS_PER_STEP, n_pg // pps),
        in_specs=[
            pl.BlockSpec((4, HEAD_DIM), lambda b, c, pt: (0, 0)),
            pl.BlockSpec((1, vd), lambda b, c, pt: (0, 0)),
            pl.BlockSpec((n_rows, d), lambda b, c, pt: (base + b, 0)),
            pl.BlockSpec((d, d), lambda b, c, pt: (0, 0)),
            new_rows,
            new_rows,
        ] + page_specs + page_specs,
        out_specs=pl.BlockSpec((n_rows, d), lambda b, c, pt: (b, 0)),
        scratch_shapes=[pltpu.VMEM((n_cols, vd), BF16), stat, stat, pltpu.VMEM((n_cols, vd), F32)],
    )
    n_pages_step = SAMPLE_SEQS_PER_STEP * pps
    return pl.pallas_call(
        functools.partial(_sattn_kernel, lam_init=lam_init, dec_seq=dec_seq),
        grid_spec=grid_spec,
        out_shape=jax.ShapeDtypeStruct((n - n_prompt, d), F32),
        compiler_params=_params(("parallel", "arbitrary")),
        name="sample_attention",
    )(pt, lam_rows, subln_g[None], x, w_q.astype(BF16), k_new, v_new,
      *([ck] * n_pages_step), *([cv] * n_pages_step))


def _oproj_kernel(ap_ref, as_ref, w_ref, x_ref, g_ref, b_ref, o_ref, *, n_prompt_tiles):
    i = pl.program_id(0)
    a = jnp.where(i >= n_prompt_tiles, as_ref[...].astype(BF16), ap_ref[...])
    m = _dot(a, w_ref[...])
    o_ref[...] = _layer_norm(ALPHA * x_ref[...] + m, g_ref[...], b_ref[...])


def _oproj_layer(a_prompt, a_sample, w_o, x, ln_g, ln_b):
    n, d = x.shape
    tm = TOKEN_TILE
    n_prompt_tiles = a_prompt.shape[0] // tm
    row = pl.BlockSpec((tm, d), lambda i: (i, 0))
    const2 = lambda i: (0, 0)
    kern = functools.partial(_oproj_kernel, n_prompt_tiles=n_prompt_tiles)
    return pl.pallas_call(
        kern,
        grid=(n // tm,),
        in_specs=[
            pl.BlockSpec((tm, d), _lo_index(n_prompt_tiles)),
            pl.BlockSpec((tm, d), _hi_index(n_prompt_tiles)),
            pl.BlockSpec((d, d), const2),
            row,
            pl.BlockSpec((1, d), const2),
            pl.BlockSpec((1, d), const2),
        ],
        out_specs=row,
        out_shape=jax.ShapeDtypeStruct((n, d), F32),
        compiler_params=_params(("parallel",)),
        name="attn_out_proj",
    )(a_prompt, a_sample, w_o.astype(BF16), x, ln_g[None], ln_b[None])


def kernel(x_prompt, x_sample, cache_k, cache_v, page_table, ln1_g, ln1_b, ln2_g, ln2_b, a_w_in, a_b_in, a_lnv_g, a_lnv_b, a_w_s, a_b_s, a_w_out, w_k, w_v, b_w_q, b_lam_q1, b_lam_k1, b_lam_q2, b_lam_k2, b_subln_g, b_w_o, router_w, router_bias, moe_w_up, moe_w_down):
    batch, seq, d = x_prompt.shape
    dec_batch, dec_seq, _ = x_sample.shape
    n_prompt = batch * seq
    n_sample = dec_batch * dec_seq
    tm = TOKEN_TILE
    assert seq % CHUNK == 0 and CHUNK % dec_seq == 0 and dec_seq <= CHUNK
    assert n_prompt % tm == 0 and n_sample % tm == 0
    assert seq % ATTN_Q_TILE == 0 and ATTN_Q_TILE % ATTN_K_TILE == 0 and ATTN_Q_TILE % ATTN_DIAG_TILE == 0
    assert a_w_s.shape[-1] == CHUNK and cache_k.shape[1] == PAGE_SIZE and cache_k.shape[2] == N_HEADS
    assert dec_batch % SAMPLE_SEQS_PER_STEP == 0 and page_table.shape[1] % SAMPLE_PAGES_PER_STEP == 0
    assert SAMPLE_SEQS_PER_STEP * dec_seq == SUBLANES

    n_prompt_tiles = n_prompt // tm
    rwt = router_w.T.astype(BF16)
    rbias = router_bias.astype(F32)[:, None]
    chunk_v = []
    x = None
    for l in range(DEPTH):
        if l < N_A_LAYERS:
            if l == 0:
                xa, xb = x_prompt.reshape(n_prompt, d), x_sample.reshape(n_sample, d)
                ia, ib = _lo_index(n_prompt_tiles), _hi_index(n_prompt_tiles)
            else:
                xa, xb = x, x
                ia, ib = _lo_index(n_prompt_tiles), (lambda i: (jnp.maximum(i, n_prompt_tiles), 0))
            x, v_rows = _gmlp_layer(xa, xb, ia, ib, a_w_in[l], a_b_in[l], a_lnv_g[l], a_lnv_b[l], a_w_s[l],
                                    a_b_s[l], a_w_out[l], ln1_g[l], ln1_b[l],
                                    n_prompt=n_prompt, n_sample=n_sample, dec_seq=dec_seq)
            chunk_v.append(v_rows.reshape(dec_batch, dec_seq, -1))
        else:
            j = l - N_A_LAYERS
            lam_init = 0.8 - 0.6 * math.exp(-0.3 * l)
            if j == 0:
                kp, vp, ks, vs, kb, vb, qb = _kvq_proj(x, w_k, w_v, b_w_q[j], n_prompt=n_prompt)
            else:
                qb = _q_proj(x, b_w_q[j], n_prompt)
            lam_rows = jnp.stack([b_lam_q1[j], b_lam_k1[j], b_lam_q2[j], b_lam_k2[j]]).astype(F32)
            a_p = _prompt_attention(qb, kb, vb, lam_rows, b_subln_g[j], lam_init, batch=batch, seq=seq)
            a_s = _sample_attention(x, b_w_q[j], ks, vs, cache_k, cache_v, page_table, lam_rows,
                                    b_subln_g[j], lam_init, n_prompt=n_prompt, dec_seq=dec_seq)
            x = _oproj_layer(a_p, a_s, b_w_o[j], x, ln1_g[l], ln1_b[l])
        x = _moe_layer(x, rwt, rbias, moe_w_up, moe_w_down, ln2_g[l], ln2_b[l],
                       layer=l, n_prompt=n_prompt, split=(l == DEPTH - 1))

    yp, ys = x
    vd = d // N_HEADS
    return (yp.reshape(batch, seq, d), ys.reshape(dec_batch, dec_seq, d),
            kp.reshape(batch, seq, N_HEADS, vd), vp.reshape(batch, seq, N_HEADS, vd),
            ks.reshape(dec_batch, dec_seq, N_HEADS, vd), vs.reshape(dec_batch, dec_seq, N_HEADS, vd),
            jnp.stack(chunk_v))
```
